```python
import jax, jax.numpy as jnp
from jax import lax
import numpy as np

D_MODEL = 1024
BATCH = 32
SEQ = 2048
DEPTH = 1

CHUNK = 64
Q_BLOCK = 128
EPS = 1e-6
MLA_HEADS = 8
Q_LORA = 384
KV_LORA = 256
QK_NOPE = 64
QK_ROPE = 32
V_DIM = 64
QK_DIM = QK_NOPE + QK_ROPE
ROPE_THETA = 10000.0
SB_HEADS = 8
SB_DIM = 64
N_EXPERTS = 32
TOP_K = 4
D_EXPERT = D_MODEL
SWIGLU_LIMIT = 7.0
SWIGLU_ALPHA = 1.702
EXPERT_BLOCK = 256
SPLIT_WIDTHS = (Q_LORA, KV_LORA, QK_ROPE, 3 * SB_HEADS * SB_DIM, D_MODEL, D_MODEL)
SPLIT_POINTS = tuple(int(v) for v in np.cumsum(SPLIT_WIDTHS)[:-1])
D_IN = int(sum(SPLIT_WIDTHS))

kernel_name = 'hybrid_mla_stickbreaking_moe_block'


def rmsnorm(x, g):
    xf = x.astype(jnp.float32)
    y = xf * lax.rsqrt(jnp.mean(xf * xf, axis=-1, keepdims=True) + EPS)
    return (y * g.astype(jnp.float32)).astype(x.dtype)


def modulate(h, shift, scale):
    return h * (1 + scale[:, None, :]) + shift[:, None, :]


def rope_tables(positions):
    inv_freq = 1.0 / (ROPE_THETA ** (jnp.arange(0, QK_ROPE, 2, dtype=jnp.float32) / QK_ROPE))
    ang = positions.astype(jnp.float32)[..., None] * inv_freq
    return jnp.cos(ang)[:, :, None, :], jnp.sin(ang)[:, :, None, :]


def apply_rope(x, cos, sin):
    xf = x.astype(jnp.float32)
    x1, x2 = xf[..., :QK_ROPE // 2], xf[..., QK_ROPE // 2:]
    return jnp.concatenate([x1 * cos - x2 * sin, x2 * cos + x1 * sin], axis=-1).astype(x.dtype)


def mla_attention(q, k, v):
    S = q.shape[2]
    scale = QK_DIM ** -0.5
    outs = []
    for i in range(S // Q_BLOCK):
        s0, s1 = i * Q_BLOCK, (i + 1) * Q_BLOCK
        sc = jnp.einsum('bhqd,bhkd->bhqk', q[:, :, s0:s1], k[:, :, :s1]).astype(jnp.float32) * scale
        q_chunk = (s0 + jnp.arange(Q_BLOCK)) // CHUNK
        k_chunk = jnp.arange(s1) // CHUNK
        mask = k_chunk[None, :] <= q_chunk[:, None]
        p = jax.nn.softmax(jnp.where(mask, sc, -jnp.inf), axis=-1)
        outs.append(jnp.einsum('bhqk,bhkd->bhqd', p.astype(v.dtype), v[:, :, :s1]))
    return jnp.concatenate(outs, axis=2)


def stick_breaking_attention(q, k, v):
    S = q.shape[2]
    scale = SB_DIM ** -0.5
    outs = []
    for i in range(S // Q_BLOCK):
        s0, s1 = i * Q_BLOCK, (i + 1) * Q_BLOCK
        z = jnp.einsum('bhqd,bhkd->bhqk', q[:, :, s0:s1], k[:, :, :s1]).astype(jnp.float32) * scale
        q_pos = s0 + jnp.arange(Q_BLOCK)
        k_pos = jnp.arange(s1)
        mask = k_pos[None, :] < q_pos[:, None]
        log_keep = jnp.where(mask, jax.nn.log_sigmoid(-z), 0.0)
        suffix = lax.cumsum(log_keep, axis=3, reverse=True)
        log_a = jax.nn.log_sigmoid(z) + suffix - log_keep
        a = jnp.where(mask, jnp.exp(log_a), 0.0)
        outs.append(jnp.einsum('bhqk,bhkd->bhqd', a.astype(v.dtype), v[:, :, :s1]))
    return jnp.concatenate(outs, axis=2)


def moe_ffn(h, w_router, b_router, w_gate_up, b_gate_up, w_down, b_down):
    B, S, D = h.shape
    T = B * S
    xf = h.reshape(T, D)
    logits = (xf @ w_router).astype(jnp.float32) + b_router.astype(jnp.float32)
    top_val, top_idx = lax.top_k(logits, TOP_K)
    top_w = jax.nn.softmax(top_val, axis=-1)
    n_assign = T * TOP_K
    flat_e = top_idx.reshape(-1).astype(jnp.int32)
    flat_tok = jnp.repeat(jnp.arange(T, dtype=jnp.int32), TOP_K)
    flat_w = top_w.reshape(-1)
    order = jnp.argsort(flat_e)
    sorted_e = flat_e[order]
    counts = jnp.bincount(flat_e, length=N_EXPERTS).astype(jnp.int32)
    padded = (counts + EXPERT_BLOCK - 1) // EXPERT_BLOCK * EXPERT_BLOCK
    start = jnp.cumsum(counts) - counts
    pend = jnp.cumsum(padded)
    pstart = pend - padded
    dest = pstart[sorted_e] + (jnp.arange(n_assign, dtype=jnp.int32) - start[sorted_e])
    P = ((n_assign + EXPERT_BLOCK - 1) // EXPERT_BLOCK) * EXPERT_BLOCK + N_EXPERTS * EXPERT_BLOCK
    NB = P // EXPERT_BLOCK
    buf_tok = jnp.full((P,), T, dtype=jnp.int32).at[dest].set(flat_tok[order])
    buf_w = jnp.zeros((P,), jnp.float32).at[dest].set(flat_w[order])
    block_e = jnp.minimum(
        jnp.searchsorted(pend, jnp.arange(NB, dtype=jnp.int32) * EXPERT_BLOCK, side='right'),
        N_EXPERTS - 1).astype(jnp.int32)
    xpad = jnp.concatenate([xf, jnp.zeros((1, D), xf.dtype)], axis=0)

    def expert_block(args):
        tok, e, wgt = args
        xb = xpad[tok]
        gu = xb @ w_gate_up[e] + b_gate_up[e]
        gate = jnp.minimum(gu[:, :D_EXPERT], SWIGLU_LIMIT)
        up = jnp.clip(gu[:, D_EXPERT:], -SWIGLU_LIMIT, SWIGLU_LIMIT)
        hid = (up + 1) * (gate * jax.nn.sigmoid(SWIGLU_ALPHA * gate))
        yb = hid @ w_down[e] + b_down[e]
        return yb.astype(jnp.float32) * wgt[:, None]

    ys = lax.map(expert_block, (buf_tok.reshape(NB, EXPERT_BLOCK), block_e,
                                buf_w.reshape(NB, EXPERT_BLOCK)))
    out = jnp.zeros((T + 1, D), jnp.float32).at[buf_tok].add(ys.reshape(P, D))
    return out[:T].reshape(B, S, D).astype(h.dtype)


def hybrid_layer(x, c, cos, sin, w_ada, b_ada, g_norm1, w_in, g_q_lat, w_uq, g_kv_lat, w_ukv,
                 g_qk_q, g_qk_k, w_o_mla, w_o_sb, w_out, g_norm2, w_router, b_router,
                 w_gate_up, b_gate_up, w_down, b_down):
    B, S, _ = x.shape
    mod = jax.nn.silu(c) @ w_ada + b_ada
    sh1, sc1, ga1, sh2, sc2, ga2 = jnp.split(mod, 6, axis=-1)

    h = modulate(rmsnorm(x, g_norm1), sh1, sc1)
    proj = h @ w_in
    q_lat, kv_lat, k_pe, sb_qkv, gate_a, gate_b = jnp.split(proj, SPLIT_POINTS, axis=-1)

    q = (rmsnorm(q_lat, g_q_lat) @ w_uq).reshape(B, S, MLA_HEADS, QK_DIM)
    kv = (rmsnorm(kv_lat, g_kv_lat) @ w_ukv).reshape(B, S, MLA_HEADS, QK_NOPE + V_DIM)
    k_nope, v = kv[..., :QK_NOPE], kv[..., QK_NOPE:]
    k = jnp.concatenate(
        [k_nope, jnp.broadcast_to(k_pe[:, :, None, :], (B, S, MLA_HEADS, QK_ROPE))], axis=-1)
    q = rmsnorm(q, g_qk_q)
    k = rmsnorm(k, g_qk_k)
    q = jnp.concatenate([q[..., :QK_NOPE], apply_rope(q[..., QK_NOPE:], cos, sin)], axis=-1)
    k = jnp.concatenate([k[..., :QK_NOPE], apply_rope(k[..., QK_NOPE:], cos, sin)], axis=-1)
    o_mla = mla_attention(q.transpose(0, 2, 1, 3), k.transpose(0, 2, 1, 3), v.transpose(0, 2, 1, 3))
    o_mla = o_mla.transpose(0, 2, 1, 3).reshape(B, S, MLA_HEADS * V_DIM)

    sb = sb_qkv.reshape(B, S, 3, SB_HEADS, SB_DIM).transpose(2, 0, 3, 1, 4)
    o_sb = stick_breaking_attention(sb[0], sb[1], sb[2])
    o_sb = o_sb.transpose(0, 2, 1, 3).reshape(B, S, SB_HEADS * SB_DIM)

    merged = jax.nn.sigmoid(gate_a) * (o_mla @ w_o_mla) + jax.nn.sigmoid(gate_b) * (o_sb @ w_o_sb)
    x = x + ga1[:, None, :] * (merged @ w_out)

    h2 = modulate(rmsnorm(x, g_norm2), sh2, sc2)
    x = x + ga2[:, None, :] * moe_ffn(h2, w_router, b_router, w_gate_up, b_gate_up, w_down, b_down)
    return x


def setup_inputs(seed: int = 0) -> dict:
    key = jax.random.key(seed)
    ks = jax.random.split(key, 26)
    f32 = jnp.float32
    L = DEPTH

    def nrm(k, shape, scale):
        return jax.random.normal(k, shape, f32) * scale

    def gain(k, shape):
        return 1.0 + 0.02 * jax.random.normal(k, shape, f32)

    x = jax.random.normal(ks[0], (BATCH, SEQ, D_MODEL), f32)
    c = jax.random.normal(ks[1], (BATCH, D_MODEL), f32)
    offset = jax.random.randint(ks[2], (BATCH, 1), 0, 8192, dtype=jnp.int32)
    positions = offset + jnp.arange(SEQ, dtype=jnp.int32)[None, :]
    return {
        'x': x,
        'c': c,
        'positions': positions,
        'w_ada': nrm(ks[3], (L, D_MODEL, 6 * D_MODEL), 0.5 * D_MODEL ** -0.5),
        'b_ada': nrm(ks[4], (L, 6 * D_MODEL), 0.02),
        'g_norm1': gain(ks[5], (L, D_MODEL)),
        'w_in': nrm(ks[6], (L, D_MODEL, D_IN), D_MODEL ** -0.5),
        'g_q_lat': gain(ks[7], (L, Q_LORA)),
        'w_uq': nrm(ks[8], (L, Q_LORA, MLA_HEADS * QK_DIM), Q_LORA ** -0.5),
        'g_kv_lat': gain(ks[9], (L, KV_LORA)),
        'w_ukv': nrm(ks[10], (L, KV_LORA, MLA_HEADS * (QK_NOPE + V_DIM)), KV_LORA ** -0.5),
        'g_qk_q': gain(ks[11], (L, QK_DIM)),
        'g_qk_k': gain(ks[12], (L, QK_DIM)),
        'w_o_mla': nrm(ks[13], (L, MLA_HEADS * V_DIM, D_MODEL), (MLA_HEADS * V_DIM) ** -0.5),
        'w_o_sb': nrm(ks[14], (L, SB_HEADS * SB_DIM, D_MODEL), (SB_HEADS * SB_DIM) ** -0.5),
        'w_out': nrm(ks[15], (L, D_MODEL, D_MODEL), D_MODEL ** -0.5),
        'g_norm2': gain(ks[16], (L, D_MODEL)),
        'w_router': nrm(ks[17], (L, D_MODEL, N_EXPERTS), D_MODEL ** -0.5),
        'b_router': nrm(ks[18], (L, N_EXPERTS), 0.01),
        'w_gate_up': nrm(ks[19], (L, N_EXPERTS, D_MODEL, 2 * D_EXPERT), D_MODEL ** -0.5),
        'b_gate_up': nrm(ks[20], (L, N_EXPERTS, 2 * D_EXPERT), 0.02),
        'w_down': nrm(ks[21], (L, N_EXPERTS, D_EXPERT, D_MODEL), D_EXPERT ** -0.5),
        'b_down': nrm(ks[22], (L, N_EXPERTS, D_MODEL), 0.02),
    }


def reference(x, c, positions, w_ada, b_ada, g_norm1, w_in, g_q_lat, w_uq, g_kv_lat, w_ukv,
              g_qk_q, g_qk_k, w_o_mla, w_o_sb, w_out, g_norm2, w_router, b_router,
              w_gate_up, b_gate_up, w_down, b_down):
    cos, sin = rope_tables(positions)
    for l in range(DEPTH):
        x = hybrid_layer(x, c, cos, sin, w_ada[l], b_ada[l], g_norm1[l], w_in[l], g_q_lat[l],
                         w_uq[l], g_kv_lat[l], w_ukv[l], g_qk_q[l], g_qk_k[l], w_o_mla[l],
                         w_o_sb[l], w_out[l], g_norm2[l], w_router[l], b_router[l],
                         w_gate_up[l], b_gate_up[l], w_down[l], b_down[l])
    return x
```

```python
import functools

import jax
import jax.numpy as jnp
import numpy as np
from jax import lax
from jax.experimental import pallas as pl
from jax.experimental.pallas import tpu as pltpu

F32 = jnp.float32
BF16 = jnp.bfloat16
I32 = jnp.int32
U32 = jnp.uint32

D_MODEL = 1024
EPS = 1e-6
CHUNK = 64
MLA_HEADS = 8
Q_LORA = 384
KV_LORA = 256
QK_NOPE = 64
QK_ROPE = 32
V_DIM = 64
QK_DIM = QK_NOPE + QK_ROPE
ROPE_THETA = 10000.0
SB_HEADS = 8
SB_DIM = 64
N_EXPERTS = 32
TOP_K = 4
D_EXPERT = D_MODEL
SWIGLU_LIMIT = 7.0
SWIGLU_ALPHA = 1.702

LANES = 128
HEAD_PAD = LANES
TOK_BLOCK = 512
ATT_BLOCK = 256
ROUTE_BLOCK = 512
ROW_BLOCK = 256
EXPERT_TILE = 512
NEG_BIG = -1e30
VMEM_LIMIT = 48 * 1024 * 1024


def _cparams(sem):
    return pltpu.CompilerParams(dimension_semantics=sem, vmem_limit_bytes=VMEM_LIMIT)


def _ada_body(c_ref, w_ref, b_ref, o_ref):
    c = c_ref[...]
    s = c * jax.nn.sigmoid(c)
    o_ref[...] = jnp.dot(s, w_ref[...], preferred_element_type=F32,
                         precision=lax.Precision.HIGHEST) + b_ref[...]


def _ada_call(c, w_ada, b_ada):
    B = c.shape[0]
    n = w_ada.shape[1]
    bn = 512
    return pl.pallas_call(
        _ada_body,
        grid=(n // bn,),
        in_specs=[pl.BlockSpec((B, D_MODEL), lambda j: (0, 0)),
                  pl.BlockSpec((D_MODEL, bn), lambda j: (0, j)),
                  pl.BlockSpec((1, bn), lambda j: (0, j))],
        out_specs=pl.BlockSpec((B, bn), lambda j: (0, j)),
        out_shape=jax.ShapeDtypeStruct((B, n), F32),
        compiler_params=_cparams(("arbitrary",)),
        name="ada_mod",
    )(c, w_ada, b_ada.reshape(1, n))


def _rms(v, width):
    return lax.rsqrt(jnp.sum(v * v, axis=-1, keepdims=True) * (1.0 / width) + EPS)


def _modulated_norm(x, g, sh, sc):
    h = x * _rms(x, D_MODEL) * g
    return h * (1.0 + sc) + sh


def _proj_body(x_ref, sh_ref, sc_ref, pos_ref, gn_ref, wql_ref, wkvl_ref, wkpe_ref,
               wsq_ref, wsk_ref, wsv_ref, gql_ref, wuq_ref, gkvl_ref, wuk_ref, wv_ref,
               gq_ref, gk_ref, invf_ref,
               q_ref, k_ref, v_ref, sq_ref, sk_ref, sv_ref):
    x = x_ref[...]
    h = _modulated_norm(x, gn_ref[...], sh_ref[0], sc_ref[0]).astype(BF16)

    sq_ref[...] = jnp.dot(h, wsq_ref[...], preferred_element_type=F32).astype(BF16)
    sk_ref[...] = jnp.dot(h, wsk_ref[...], preferred_element_type=F32).astype(BF16)
    sv_ref[...] = jnp.dot(h, wsv_ref[...], preferred_element_type=F32).astype(BF16)

    q_lat = jnp.dot(h, wql_ref[...], preferred_element_type=F32)
    kv_lat = jnp.dot(h, wkvl_ref[...], preferred_element_type=F32)
    kpe = jnp.dot(h, wkpe_ref[...], preferred_element_type=F32)

    qn = (q_lat * _rms(q_lat, Q_LORA) * gql_ref[...]).astype(BF16)
    kvn = (kv_lat * _rms(kv_lat, KV_LORA) * gkvl_ref[...]).astype(BF16)
    q = jnp.dot(qn, wuq_ref[...], preferred_element_type=F32)
    kn = jnp.dot(kvn, wuk_ref[...], preferred_element_type=F32)
    v_ref[...] = jnp.dot(kvn, wv_ref[...], preferred_element_type=F32).astype(BF16)

    tm = x.shape[0]
    lane = lax.broadcasted_iota(I32, (tm, LANES), 1)
    ang = pos_ref[...] * invf_ref[...]
    cosf = jnp.cos(ang)
    sinf = jnp.sin(ang)
    half = QK_ROPE // 2
    s_lo = jnp.where((lane >= QK_NOPE) & (lane < QK_NOPE + half), -sinf, 0.0)
    s_hi = jnp.where((lane >= QK_NOPE + half) & (lane < QK_DIM), sinf, 0.0)

    def rope(t):
        return (t * cosf + pltpu.roll(t, LANES - half, 1) * s_lo
                + pltpu.roll(t, half, 1) * s_hi)

    gq = gq_ref[...]
    gk = gk_ref[...]
    kpe_sq = jnp.sum(kpe * kpe, axis=-1, keepdims=True)
    kr = rope(kpe * gk)
    for hh in range(MLA_HEADS):
        sl = slice(hh * HEAD_PAD, (hh + 1) * HEAD_PAD)
        qh = q[:, sl]
        qh = qh * _rms(qh, QK_DIM) * gq
        q_ref[:, sl] = rope(qh).astype(BF16)
        kh = kn[:, sl]
        rk = lax.rsqrt((jnp.sum(kh * kh, axis=-1, keepdims=True) + kpe_sq) * (1.0 / QK_DIM) + EPS)
        k_ref[:, sl] = ((kh * gk + kr) * rk).astype(BF16)


def _proj_call(x2, sh1, sc1, posf, p, B, S):
    T = B * S
    tm = TOK_BLOCK
    nb = S // tm
    row = lambda i: (i, 0)
    per_b = lambda i: (i // nb, 0, 0)
    full = lambda i: (0, 0)

    def wspec(a):
        return pl.BlockSpec(a.shape, full)

    weights = [p["g_norm1"], p["w_ql"], p["w_kvl"], p["w_kpe"], p["w_sq"], p["w_sk"], p["w_sv"],
               p["g_q_lat"], p["w_uq"], p["g_kv_lat"], p["w_uk"], p["w_v"], p["g_qk_q"],
               p["g_qk_k"], p["invf"]]
    out_w = [MLA_HEADS * HEAD_PAD, MLA_HEADS * HEAD_PAD, MLA_HEADS * V_DIM,
             SB_HEADS * SB_DIM, SB_HEADS * SB_DIM, SB_HEADS * SB_DIM]
    return pl.pallas_call(
        _proj_body,
        grid=(T // tm,),
        in_specs=[pl.BlockSpec((tm, D_MODEL), row),
                  pl.BlockSpec((1, 1, D_MODEL), per_b),
                  pl.BlockSpec((1, 1, D_MODEL), per_b),
                  pl.BlockSpec((tm, 1), row)] + [wspec(a) for a in weights],
        out_specs=[pl.BlockSpec((tm, w), row) for w in out_w],
        out_shape=[jax.ShapeDtypeStruct((T, w), BF16) for w in out_w],
        compiler_params=_cparams(("arbitrary",)),
        name="mix_proj",
    )(x2, sh1, sc1, posf, *weights)


def _mla_body(q_ref, k_ref, v_ref, o_ref):
    tq = q_ref.shape[0]
    tk = tq
    i = pl.program_id(2)
    scale = QK_DIM ** -0.5
    row = lax.broadcasted_iota(I32, (tq, tk), 0)
    col = lax.broadcasted_iota(I32, (tq, tk), 1)
    diag_mask = (col // CHUNK) <= (row // CHUNK)
    lane = lax.broadcasted_iota(I32, (tq, LANES), 1)
    dn = (((1,), (1,)), ((), ()))
    outs = []
    for hh in range(2):
        sl = slice(hh * HEAD_PAD, (hh + 1) * HEAD_PAD)
        qh = q_ref[:, sl]

        def block(j, carry, masked):
            m, l, acc = carry
            off = pl.multiple_of(j * tk, tk)
            kb = k_ref[pl.ds(off, tk), sl]
            vb = v_ref[pl.ds(off, tk), :]
            s = lax.dot_general(qh, kb, dn, preferred_element_type=F32) * scale
            if masked:
                s = jnp.where(diag_mask, s, NEG_BIG)
            m_new = jnp.maximum(m, jnp.max(s, axis=-1, keepdims=True))
            alpha = jnp.exp(m - m_new)
            pr = jnp.exp(s - m_new)
            l = alpha * l + jnp.sum(pr, axis=-1, keepdims=True)
            acc = alpha * acc + jnp.dot(pr.astype(BF16), vb, preferred_element_type=F32)
            return m_new, l, acc

        init = (jnp.full((tq, 1), NEG_BIG, F32), jnp.zeros((tq, 1), F32),
                jnp.zeros((tq, LANES), F32))
        carry = lax.fori_loop(0, i, lambda j, c: block(j, c, False), init)
        m, l, acc = block(i, carry, True)
        outs.append(acc / l)
    o_ref[...] = jnp.where(lane < V_DIM, outs[0], outs[1]).astype(BF16)


def _mla_call(q, k, v, B, S):
    tq = ATT_BLOCK
    q3 = q.reshape(B, S, MLA_HEADS * HEAD_PAD)
    k3 = k.reshape(B, S, MLA_HEADS * HEAD_PAD)
    v3 = v.reshape(B, S, MLA_HEADS * V_DIM)
    out = pl.pallas_call(
        _mla_body,
        grid=(B, MLA_HEADS // 2, S // tq),
        in_specs=[pl.BlockSpec((None, tq, 2 * HEAD_PAD), lambda b, hp, i: (b, i, hp)),
                  pl.BlockSpec((None, S, 2 * HEAD_PAD), lambda b, hp, i: (b, 0, hp)),
                  pl.BlockSpec((None, S, 2 * V_DIM), lambda b, hp, i: (b, 0, hp))],
        out_specs=pl.BlockSpec((None, tq, 2 * V_DIM), lambda b, hp, i: (b, i, hp)),
        out_shape=jax.ShapeDtypeStruct((B, S, MLA_HEADS * V_DIM), BF16),
        compiler_params=_cparams(("arbitrary", "arbitrary", "arbitrary")),
        name="mla_attn",
    )(q3, k3, v3)
    return out.reshape(B * S, MLA_HEADS * V_DIM)


def _sb_body(q_ref, k_ref, v_ref, o_ref):
    tq = q_ref.shape[0]
    tk = tq
    i = pl.program_id(2)
    scale = SB_DIM ** -0.5
    row = lax.broadcasted_iota(I32, (tq, tk), 0)
    col = lax.broadcasted_iota(I32, (tq, tk), 1)
    diag_mask = col < row
    suffix_ones = jnp.where(row >= col, 1.0, 0.0).astype(BF16)
    lane = lax.broadcasted_iota(I32, (tq, LANES), 1)
    dn = (((1,), (1,)), ((), ()))
    q = q_ref[...]
    outs = []
    for hh in range(2):
        in_head = (lane >= hh * SB_DIM) & (lane < (hh + 1) * SB_DIM)
        qh = jnp.where(in_head, q, jnp.zeros_like(q))

        def block(j, carry, masked):
            run, acc = carry
            off = pl.multiple_of(j * tk, tk)
            kb = k_ref[pl.ds(off, tk), :]
            vb = v_ref[pl.ds(off, tk), :]
            z = lax.dot_general(qh, kb, dn, preferred_element_type=F32) * scale
            lk = -(jnp.maximum(z, 0.0) + jnp.log1p(jnp.exp(-jnp.abs(z))))
            if masked:
                lk = jnp.where(diag_mask, lk, 0.0)
            lk_hi = lk.astype(BF16)
            lk_lo = (lk - lk_hi.astype(F32)).astype(BF16)
            suf = (jnp.dot(lk_hi, suffix_ones, preferred_element_type=F32)
                   + jnp.dot(lk_lo, suffix_ones, preferred_element_type=F32))
            a = jnp.exp(z + suf + run)
            if masked:
                a = jnp.where(diag_mask, a, 0.0)
            acc = acc + jnp.dot(a.astype(BF16), vb, preferred_element_type=F32)
            return run + suf[:, 0:1], acc

        init = (jnp.zeros((tq, 1), F32), jnp.zeros((tq, LANES), F32))
        carry = block(i, init, True)
        _, acc = lax.fori_loop(0, i, lambda t, c: block(i - 1 - t, c, False), carry)
        outs.append(acc)
    o_ref[...] = jnp.where(lane < SB_DIM, outs[0], outs[1]).astype(BF16)


def _sb_call(sq, sk, sv, B, S):
    tq = ATT_BLOCK
    w = SB_HEADS * SB_DIM
    args = [a.reshape(B, S, w) for a in (sq, sk, sv)]
    out = pl.pallas_call(
        _sb_body,
        grid=(B, SB_HEADS // 2, S // tq),
        in_specs=[pl.BlockSpec((None, tq, 2 * SB_DIM), lambda b, hp, i: (b, i, hp)),
                  pl.BlockSpec((None, S, 2 * SB_DIM), lambda b, hp, i: (b, 0, hp)),
                  pl.BlockSpec((None, S, 2 * SB_DIM), lambda b, hp, i: (b, 0, hp))],
        out_specs=pl.BlockSpec((None, tq, 2 * SB_DIM), lambda b, hp, i: (b, i, hp)),
        out_shape=jax.ShapeDtypeStruct((B, S, w), BF16),
        compiler_params=_cparams(("arbitrary", "arbitrary", "arbitrary")),
        name="sb_attn",
    )(*args)
    return out.reshape(B * S, w)


def _pack_pairs(h):
    n = h.shape[1] // 2
    lo = pltpu.bitcast(h[:, :n].astype(BF16).astype(F32), U32)
    hi = pltpu.bitcast(h[:, n:].astype(BF16).astype(F32), U32)
    return (lo >> 16) | (hi & jnp.uint32(0xFFFF0000))


def _unpack_pairs(w):
    lo = pltpu.bitcast(w << 16, F32).astype(BF16)
    hi = pltpu.bitcast(w & jnp.uint32(0xFFFF0000), F32).astype(BF16)
    return lo, hi


def _merge_body(x_ref, sh1_ref, sc1_ref, ga1_ref, sh2_ref, sc2_ref, om_ref, os_ref,
                gn1_ref, wga_ref, wgb_ref, wom_ref, wos_ref, wout_ref, gn2_ref,
                wrh_ref, wrl_ref, br_ref,
                x1_ref, h2_ref, lg_ref):
    x = x_ref[...]
    h = _modulated_norm(x, gn1_ref[...], sh1_ref[0], sc1_ref[0]).astype(BF16)
    ga = jax.nn.sigmoid(jnp.dot(h, wga_ref[...], preferred_element_type=F32))
    gb = jax.nn.sigmoid(jnp.dot(h, wgb_ref[...], preferred_element_type=F32))
    merged = (ga * jnp.dot(om_ref[...], wom_ref[...], preferred_element_type=F32)
              + gb * jnp.dot(os_ref[...], wos_ref[...], preferred_element_type=F32))
    y = jnp.dot(merged.astype(BF16), wout_ref[...], preferred_element_type=F32)
    x1 = x + ga1_ref[0] * y
    x1_ref[...] = x1
    h2 = _modulated_norm(x1, gn2_ref[...], sh2_ref[0], sc2_ref[0])
    h2_ref[...] = _pack_pairs(h2)
    h2_hi = h2.astype(BF16)
    h2_lo = (h2 - h2_hi.astype(F32)).astype(BF16)
    lg_ref[...] = (jnp.dot(h2_hi, wrh_ref[...], preferred_element_type=F32)
                   + jnp.dot(h2_hi, wrl_ref[...], preferred_element_type=F32)
                   + jnp.dot(h2_lo, wrh_ref[...], preferred_element_type=F32)
                   + br_ref[...])


def _merge_call(x2, mods, o_mla, o_sb, p, B, S):
    T = B * S
    tm = TOK_BLOCK
    nb = S // tm
    row = lambda i: (i, 0)
    per_b = lambda i: (i // nb, 0, 0)
    full = lambda i: (0, 0)
    weights = [p["g_norm1"], p["w_ga"], p["w_gb"], p["w_o_mla"], p["w_o_sb"], p["w_out"],
               p["g_norm2"], p["w_r_hi"], p["w_r_lo"], p["b_r"]]
    return pl.pallas_call(
        _merge_body,
        grid=(T // tm,),
        in_specs=[pl.BlockSpec((tm, D_MODEL), row)]
        + [pl.BlockSpec((1, 1, D_MODEL), per_b)] * 5
        + [pl.BlockSpec((tm, MLA_HEADS * V_DIM), row), pl.BlockSpec((tm, SB_HEADS * SB_DIM), row)]
        + [pl.BlockSpec(a.shape, full) for a in weights],
        out_specs=[pl.BlockSpec((tm, D_MODEL), row), pl.BlockSpec((tm, D_MODEL // 2), row),
                   pl.BlockSpec((tm, LANES), row)],
        out_shape=[jax.ShapeDtypeStruct((T, D_MODEL), F32),
                   jax.ShapeDtypeStruct((T, D_MODEL // 2), U32),
                   jax.ShapeDtypeStruct((T, LANES), F32)],
        compiler_params=_cparams(("arbitrary",)),
        name="merge_norm",
    )(x2, *mods, o_mla, o_sb, *weights)


def _route_body(lg_ref, tri_ref, e_ref, r_ref, w_ref, cnt_ref, run_ref):
    @pl.when(pl.program_id(0) == 0)
    def _():
        run_ref[...] = jnp.zeros_like(run_ref)

    v = lg_ref[...]
    tm = v.shape[0]
    lane_i = lax.broadcasted_iota(I32, (tm, LANES), 1)
    lane = lane_i.astype(F32)
    vals, idxs = [], []
    for _ in range(TOP_K):
        m = jnp.max(v, axis=-1, keepdims=True)
        idx = jnp.min(jnp.where(v == m, lane, float(LANES)), axis=-1, keepdims=True)
        vals.append(m)
        idxs.append(idx)
        v = jnp.where(lane == idx, NEG_BIG, v)
    ex = [jnp.exp(t - vals[0]) for t in vals]
    denom = ex[0] + ex[1] + ex[2] + ex[3]
    onehots = [lane == idx for idx in idxs]
    oh = jnp.zeros((tm, LANES), F32)
    for o in onehots:
        oh = oh + jnp.where(o, 1.0, 0.0)
    before = jnp.dot(tri_ref[...], oh.astype(BF16), preferred_element_type=F32) + run_ref[...]
    e_out = jnp.zeros((tm, LANES), I32)
    r_out = jnp.zeros((tm, LANES), I32)
    w_out = jnp.zeros((tm, LANES), F32)
    for kk in range(TOP_K):
        rank = jnp.sum(jnp.where(onehots[kk], before, 0.0), axis=-1, keepdims=True)
        e_out = jnp.where(lane_i == kk, idxs[kk].astype(I32), e_out)
        r_out = jnp.where(lane_i == kk, rank.astype(I32), r_out)
        w_out = jnp.where(lane_i == kk, ex[kk] / denom, w_out)
    e_ref[...] = e_out[:, :TOP_K]
    r_ref[...] = r_out[:, :TOP_K]
    w_ref[...] = w_out[:, :TOP_K]
    run_ref[...] = run_ref[...] + jnp.sum(oh, axis=0, keepdims=True)
    cnt_ref[...] = run_ref[...]


def _route_call(logits, T):
    tm = ROUTE_BLOCK
    tri = (np.arange(tm)[:, None] > np.arange(tm)[None, :]).astype(np.float32)
    tri = jnp.asarray(tri, dtype=BF16)
    row = lambda i: (i, 0)
    return pl.pallas_call(
        _route_body,
        grid=(T // tm,),
        in_specs=[pl.BlockSpec((tm, LANES), row), pl.BlockSpec((tm, tm), lambda i: (0, 0))],
        out_specs=[pl.BlockSpec((tm, TOP_K), row), pl.BlockSpec((tm, TOP_K), row),
                   pl.BlockSpec((tm, TOP_K), row), pl.BlockSpec((1, LANES), lambda i: (0, 0))],
        out_shape=[jax.ShapeDtypeStruct((T, TOP_K), I32), jax.ShapeDtypeStruct((T, TOP_K), I32),
                   jax.ShapeDtypeStruct((T, TOP_K), F32), jax.ShapeDtypeStruct((1, LANES), F32)],
        scratch_shapes=[pltpu.VMEM((1, LANES), F32)],
        compiler_params=_cparams(("arbitrary",)),
        name="route_topk",
    )(logits, tri)


def _dispatch_body(ps_ref, e_ref, r_ref, h_hbm, z_hbm, xs_hbm, sem):
    del z_hbm
    base = pl.program_id(0) * ROW_BLOCK

    def row_copy(src_row, dst_row):
        return pltpu.make_async_copy(h_hbm.at[pl.ds(src_row, 1)], xs_hbm.at[pl.ds(dst_row, 1)], sem)

    def issue(t, c):
        for kk in range(TOP_K):
            a = t * TOP_K + kk
            row_copy(base + t, ps_ref[e_ref[a]] + r_ref[a]).start()
        return c

    lax.fori_loop(0, ROW_BLOCK, issue, 0)

    def drain(t, c):
        row_copy(0, 0).wait()
        return c

    lax.fori_loop(0, ROW_BLOCK * TOP_K, drain, 0)


def _dispatch_call(pstart, e_flat, r_flat, h2p, n_rows):
    T = h2p.shape[0]
    nblk = ROW_BLOCK * TOP_K
    zeros = jnp.zeros((n_rows, D_MODEL // 2), U32)
    smem_blk = pl.BlockSpec((nblk,), lambda i, ps: (i,), memory_space=pltpu.SMEM)
    return pl.pallas_call(
        _dispatch_body,
        grid_spec=pltpu.PrefetchScalarGridSpec(
            num_scalar_prefetch=1,
            grid=(T // ROW_BLOCK,),
            in_specs=[smem_blk, smem_blk,
                      pl.BlockSpec(memory_space=pl.ANY), pl.BlockSpec(memory_space=pl.ANY)],
            out_specs=pl.BlockSpec(memory_space=pl.ANY),
            scratch_shapes=[pltpu.SemaphoreType.DMA(())]),
        out_shape=jax.ShapeDtypeStruct((n_rows, D_MODEL // 2), U32),
        input_output_aliases={4: 0},
        compiler_params=_cparams(("arbitrary",)),
        name="moe_dispatch",
    )(pstart, e_flat, r_flat, h2p, zeros)


def _expert_body(te_ref, tv_ref, xs_ref, wgu_ref, bgu_ref, wd_ref, bd_ref, y_ref):
    i = pl.program_id(0)

    @pl.when(tv_ref[i] == 1)
    def _():
        lo, hi = _unpack_pairs(xs_ref[...])
        half = D_MODEL // 2
        gu = (jnp.dot(lo, wgu_ref[:half, :], preferred_element_type=F32)
              + jnp.dot(hi, wgu_ref[half:, :], preferred_element_type=F32) + bgu_ref[...])
        gate = jnp.minimum(gu[:, :D_EXPERT], SWIGLU_LIMIT)
        up = jnp.clip(gu[:, D_EXPERT:], -SWIGLU_LIMIT, SWIGLU_LIMIT)
        hid = (up + 1.0) * (gate * jax.nn.sigmoid(SWIGLU_ALPHA * gate))
        y_ref[...] = jnp.dot(hid.astype(BF16), wd_ref[...], preferred_element_type=F32) + bd_ref[...]

    @pl.when(tv_ref[i] == 0)
    def _():
        y_ref[...] = jnp.zeros_like(y_ref)


def _expert_call(tile_e, tile_v, xs, p):
    n_rows = xs.shape[0]
    tmx = EXPERT_TILE
    by_e = lambda i, te, tv: (te[i], 0, 0)
    return pl.pallas_call(
        _expert_body,
        grid_spec=pltpu.PrefetchScalarGridSpec(
            num_scalar_prefetch=2,
            grid=(n_rows // tmx,),
            in_specs=[pl.BlockSpec((tmx, D_MODEL // 2), lambda i, te, tv: (i, 0)),
                      pl.BlockSpec((None, D_MODEL, 2 * D_EXPERT), by_e),
                      pl.BlockSpec((None, 1, 2 * D_EXPERT), by_e),
                      pl.BlockSpec((None, D_EXPERT, D_MODEL), by_e),
                      pl.BlockSpec((None, 1, D_MODEL), by_e)],
            out_specs=pl.BlockSpec((tmx, D_MODEL), lambda i, te, tv: (i, 0))),
        out_shape=jax.ShapeDtypeStruct((n_rows, D_MODEL), F32),
        compiler_params=_cparams(("arbitrary",)),
        name="moe_experts",
    )(tile_e, tile_v, xs, p["w_gate_up"], p["b_gate_up"], p["w_down"], p["b_down"])


def _combine_body(ps_ref, e_ref, r_ref, x1_ref, w_ref, ga2_ref, ys_hbm, o_ref, buf, sem):
    def row_copy(src_row, kk, t):
        return pltpu.make_async_copy(ys_hbm.at[pl.ds(src_row, 1)], buf.at[kk, pl.ds(t, 1)], sem)

    def issue(t, c):
        for kk in range(TOP_K):
            a = t * TOP_K + kk
            row_copy(ps_ref[e_ref[a]] + r_ref[a], kk, t).start()
        return c

    lax.fori_loop(0, ROW_BLOCK, issue, 0)

    def drain(t, c):
        row_copy(0, 0, 0).wait()
        return c

    lax.fori_loop(0, ROW_BLOCK * TOP_K, drain, 0)

    w = w_ref[...]
    acc = w[:, 0:1] * buf[0]
    for kk in range(1, TOP_K):
        acc = acc + w[:, kk:kk + 1] * buf[kk]
    o_ref[...] = x1_ref[...] + ga2_ref[0] * acc


def _combine_call(pstart, e_flat, r_flat, x1, wts, ga2, ys, B, S):
    T = B * S
    nblk = ROW_BLOCK * TOP_K
    nb = S // ROW_BLOCK
    smem_blk = pl.BlockSpec((nblk,), lambda i, ps: (i,), memory_space=pltpu.SMEM)
    return pl.pallas_call(
        _combine_body,
        grid_spec=pltpu.PrefetchScalarGridSpec(
            num_scalar_prefetch=1,
            grid=(T // ROW_BLOCK,),
            in_specs=[smem_blk, smem_blk,
                      pl.BlockSpec((ROW_BLOCK, D_MODEL), lambda i, ps: (i, 0)),
                      pl.BlockSpec((ROW_BLOCK, TOP_K), lambda i, ps: (i, 0)),
                      pl.BlockSpec((1, 1, D_MODEL), lambda i, ps: (i // nb, 0, 0)),
                      pl.BlockSpec(memory_space=pl.ANY)],
            out_specs=pl.BlockSpec((ROW_BLOCK, D_MODEL), lambda i, ps: (i, 0)),
            scratch_shapes=[pltpu.VMEM((TOP_K, ROW_BLOCK, D_MODEL), F32),
                            pltpu.SemaphoreType.DMA(())]),
        out_shape=jax.ShapeDtypeStruct((T, D_MODEL), F32),
        compiler_params=_cparams(("arbitrary",)),
        name="moe_combine",
    )(pstart, e_flat, r_flat, x1, wts, ga2, ys)


def _pad_heads(w, per_head, n_heads):
    k = w.shape[0]
    w = w.reshape(k, n_heads, per_head)
    w = jnp.pad(w, ((0, 0), (0, 0), (0, HEAD_PAD - per_head)))
    return w.reshape(k, n_heads * HEAD_PAD)


def _prepare(w_in, g_norm1, g_q_lat, w_uq, g_kv_lat, w_ukv, g_qk_q, g_qk_k, w_o_mla, w_o_sb,
             w_out, g_norm2, w_router, b_router, w_gate_up, b_gate_up, w_down, b_down):
    c0 = Q_LORA
    c1 = c0 + KV_LORA
    c2 = c1 + QK_ROPE
    sbw = SB_HEADS * SB_DIM
    c3 = c2 + 3 * sbw
    c4 = c3 + D_MODEL
    p = {}
    p["g_norm1"] = g_norm1.reshape(1, D_MODEL)
    p["g_norm2"] = g_norm2.reshape(1, D_MODEL)
    p["w_ql"] = w_in[:, :c0].astype(BF16)
    p["w_kvl"] = w_in[:, c0:c1].astype(BF16)
    p["w_kpe"] = jnp.pad(w_in[:, c1:c2], ((0, 0), (QK_NOPE, HEAD_PAD - QK_DIM))).astype(BF16)
    p["w_sq"] = w_in[:, c2:c2 + sbw].astype(BF16)
    p["w_sk"] = w_in[:, c2 + sbw:c2 + 2 * sbw].astype(BF16)
    p["w_sv"] = w_in[:, c2 + 2 * sbw:c3].astype(BF16)
    p["w_ga"] = w_in[:, c3:c4].astype(BF16)
    p["w_gb"] = w_in[:, c4:].astype(BF16)
    p["g_q_lat"] = g_q_lat.reshape(1, Q_LORA)
    p["g_kv_lat"] = g_kv_lat.reshape(1, KV_LORA)
    p["w_uq"] = _pad_heads(w_uq, QK_DIM, MLA_HEADS).astype(BF16)
    kv = w_ukv.reshape(KV_LORA, MLA_HEADS, QK_NOPE + V_DIM)
    p["w_uk"] = _pad_heads(kv[:, :, :QK_NOPE].reshape(KV_LORA, MLA_HEADS * QK_NOPE),
                           QK_NOPE, MLA_HEADS).astype(BF16)
    p["w_v"] = kv[:, :, QK_NOPE:].reshape(KV_LORA, MLA_HEADS * V_DIM).astype(BF16)
    p["g_qk_q"] = jnp.pad(g_qk_q, (0, HEAD_PAD - QK_DIM)).reshape(1, HEAD_PAD)
    p["g_qk_k"] = jnp.pad(g_qk_k, (0, HEAD_PAD - QK_DIM)).reshape(1, HEAD_PAD)
    inv_freq = 1.0 / (ROPE_THETA ** (jnp.arange(0, QK_ROPE, 2, dtype=F32) / QK_ROPE))
    p["invf"] = jnp.concatenate([jnp.zeros((QK_NOPE,), F32), inv_freq, inv_freq,
                                 jnp.zeros((HEAD_PAD - QK_DIM,), F32)]).reshape(1, HEAD_PAD)
    p["w_o_mla"] = w_o_mla.astype(BF16)
    p["w_o_sb"] = w_o_sb.astype(BF16)
    p["w_out"] = w_out.astype(BF16)
    wr = jnp.pad(w_router, ((0, 0), (0, LANES - N_EXPERTS)))
    p["w_r_hi"] = wr.astype(BF16)
    p["w_r_lo"] = (wr - p["w_r_hi"].astype(F32)).astype(BF16)
    p["b_r"] = jnp.concatenate([b_router, jnp.full((LANES - N_EXPERTS,), NEG_BIG, F32)]).reshape(1, LANES)
    p["w_gate_up"] = w_gate_up.astype(BF16)
    p["b_gate_up"] = b_gate_up.reshape(N_EXPERTS, 1, 2 * D_EXPERT)
    p["w_down"] = w_down.astype(BF16)
    p["b_down"] = b_down.reshape(N_EXPERTS, 1, D_MODEL)
    return p


def _layer(x2, c, posf, B, S, w_ada, b_ada, *layer_weights):
    T = B * S
    p = _prepare(*layer_weights)
    mod = _ada_call(c, w_ada, b_ada)
    sh1, sc1, ga1, sh2, sc2, ga2 = [m.reshape(B, 1, D_MODEL) for m in jnp.split(mod, 6, axis=-1)]

    q, k, v, sq, sk, sv = _proj_call(x2, sh1, sc1, posf, p, B, S)
    o_mla = _mla_call(q, k, v, B, S)
    o_sb = _sb_call(sq, sk, sv, B, S)
    x1, h2p, logits = _merge_call(x2, (sh1, sc1, ga1, sh2, sc2), o_mla, o_sb, p, B, S)

    e_idx, rank, wts, counts = _route_call(logits, T)
    cnt = counts[0, :N_EXPERTS].astype(I32)
    tiles_e = (cnt + EXPERT_TILE - 1) // EXPERT_TILE
    tile_end = jnp.cumsum(tiles_e)
    pstart = ((tile_end - tiles_e) * EXPERT_TILE).astype(I32)
    n_tiles = (T * TOP_K) // EXPERT_TILE + N_EXPERTS
    tile_ids = jnp.arange(n_tiles, dtype=I32)
    tile_e = jnp.minimum(jnp.searchsorted(tile_end, tile_ids, side="right"), N_EXPERTS - 1).astype(I32)
    tile_v = (tile_ids < tile_end[-1]).astype(I32)

    e_flat = e_idx.reshape(T * TOP_K)
    r_flat = rank.reshape(T * TOP_K)
    xs = _dispatch_call(pstart, e_flat, r_flat, h2p, n_tiles * EXPERT_TILE)
    ys = _expert_call(tile_e, tile_v, xs, p)
    return _combine_call(pstart, e_flat, r_flat, x1, wts, ga2, ys, B, S)


def kernel(x, c, positions, w_ada, b_ada, g_norm1, w_in, g_q_lat, w_uq, g_kv_lat, w_ukv, g_qk_q,
           g_qk_k, w_o_mla, w_o_sb, w_out, g_norm2, w_router, b_router, w_gate_up, b_gate_up,
           w_down, b_down):
    B, S, D = x.shape
    x2 = x.reshape(B * S, D)
    posf = positions.astype(F32).reshape(B * S, 1)
    for l in range(w_ada.shape[0]):
        x2 = _layer(x2, c, posf, B, S, w_ada[l], b_ada[l], w_in[l], g_norm1[l], g_q_lat[l],
                    w_uq[l], g_kv_lat[l], w_ukv[l], g_qk_q[l], g_qk_k[l], w_o_mla[l], w_o_sb[l],
                    w_out[l], g_norm2[l], w_router[l], b_router[l], w_gate_up[l], b_gate_up[l],
                    w_down[l], b_down[l])
    return x2.reshape(B, S, D)
```

```python
import functools

import jax
import jax.numpy as jnp
import numpy as np
from jax import lax
from jax.experimental import pallas as pl
from jax.experimental.pallas import tpu as pltpu

F32 = jnp.float32
BF16 = jnp.bfloat16
I32 = jnp.int32
U32 = jnp.uint32

D_MODEL = 1024
EPS = 1e-6
CHUNK = 64
MLA_HEADS = 8
Q_LORA = 384
KV_LORA = 256
QK_NOPE = 64
QK_ROPE = 32
V_DIM = 64
QK_DIM = QK_NOPE + QK_ROPE
ROPE_THETA = 10000.0
SB_HEADS = 8
SB_DIM = 64
N_EXPERTS = 32
TOP_K = 4
D_EXPERT = D_MODEL
SWIGLU_LIMIT = 7.0
SWIGLU_ALPHA = 1.702

LANES = 128
HEAD_PAD = LANES
TOK_BLOCK = 512
ATT_BLOCK = 256
ROUTE_BLOCK = 512
ROW_BLOCK = 256
EXPERT_TILE = 512
NEG_BIG = -1e30
VMEM_LIMIT = 48 * 1024 * 1024


def _cparams(sem):
    return pltpu.CompilerParams(dimension_semantics=sem, vmem_limit_bytes=VMEM_LIMIT)


def _ada_body(c_ref, w_ref, b_ref, o_ref):
    c = c_ref[...]
    s = c * jax.nn.sigmoid(c)
    o_ref[...] = jnp.dot(s, w_ref[...], preferred_element_type=F32,
                         precision=lax.Precision.HIGHEST) + b_ref[...]


def _ada_call(c, w_ada, b_ada):
    B = c.shape[0]
    n = w_ada.shape[1]
    bn = 512
    return pl.pallas_call(
        _ada_body,
        grid=(n // bn,),
        in_specs=[pl.BlockSpec((B, D_MODEL), lambda j: (0, 0)),
                  pl.BlockSpec((D_MODEL, bn), lambda j: (0, j)),
                  pl.BlockSpec((1, bn), lambda j: (0, j))],
        out_specs=pl.BlockSpec((B, bn), lambda j: (0, j)),
        out_shape=jax.ShapeDtypeStruct((B, n), F32),
        compiler_params=_cparams(("arbitrary",)),
        name="ada_mod",
    )(c, w_ada, b_ada.reshape(1, n))


def _rms(v, width):
    return lax.rsqrt(jnp.sum(v * v, axis=-1, keepdims=True) * (1.0 / width) + EPS)


def _modulated_norm(x, g, sh, sc):
    h = x * _rms(x, D_MODEL) * g
    return h * (1.0 + sc) + sh


def _proj_body(x_ref, sh_ref, sc_ref, pos_ref, gn_ref, wql_ref, wkvl_ref, wkpe_ref,
               wsq_ref, wsk_ref, wsv_ref, gql_ref, wuq_ref, gkvl_ref, wuk_ref, wv_ref,
               gq_ref, gk_ref, invf_ref,
               q_ref, k_ref, v_ref, sq_ref, sk_ref, sv_ref):
    x = x_ref[...]
    h = _modulated_norm(x, gn_ref[...], sh_ref[0], sc_ref[0]).astype(BF16)

    sq_ref[...] = jnp.dot(h, wsq_ref[...], preferred_element_type=F32).astype(BF16)
    sk_ref[...] = jnp.dot(h, wsk_ref[...], preferred_element_type=F32).astype(BF16)
    sv_ref[...] = jnp.dot(h, wsv_ref[...], preferred_element_type=F32).astype(BF16)

    q_lat = jnp.dot(h, wql_ref[...], preferred_element_type=F32)
    kv_lat = jnp.dot(h, wkvl_ref[...], preferred_element_type=F32)
    kpe = jnp.dot(h, wkpe_ref[...], preferred_element_type=F32)

    qn = (q_lat * _rms(q_lat, Q_LORA) * gql_ref[...]).astype(BF16)
    kvn = (kv_lat * _rms(kv_lat, KV_LORA) * gkvl_ref[...]).astype(BF16)
    q = jnp.dot(qn, wuq_ref[...], preferred_element_type=F32)
    kn = jnp.dot(kvn, wuk_ref[...], preferred_element_type=F32)
    v_ref[...] = jnp.dot(kvn, wv_ref[...], preferred_element_type=F32).astype(BF16)

    tm = x.shape[0]
    lane = lax.broadcasted_iota(I32, (tm, LANES), 1)
    ang = pos_ref[...] * invf_ref[...]
    cosf = jnp.cos(ang)
    sinf = jnp.sin(ang)
    half = QK_ROPE // 2
    s_lo = jnp.where((lane >= QK_NOPE) & (lane < QK_NOPE + half), -sinf, 0.0)
    s_hi = jnp.where((lane >= QK_NOPE + half) & (lane < QK_DIM), sinf, 0.0)

    def rope(t):
        return (t * cosf + pltpu.roll(t, LANES - half, 1) * s_lo
                + pltpu.roll(t, half, 1) * s_hi)

    gq = gq_ref[...]
    gk = gk_ref[...]
    kpe_sq = jnp.sum(kpe * kpe, axis=-1, keepdims=True)
    kr = rope(kpe * gk)
    for hh in range(MLA_HEADS):
        sl = slice(hh * HEAD_PAD, (hh + 1) * HEAD_PAD)
        qh = q[:, sl]
        qh = qh * _rms(qh, QK_DIM) * gq
        q_ref[:, sl] = rope(qh).astype(BF16)
        kh = kn[:, sl]
        rk = lax.rsqrt((jnp.sum(kh * kh, axis=-1, keepdims=True) + kpe_sq) * (1.0 / QK_DIM) + EPS)
        k_ref[:, sl] = ((kh * gk + kr) * rk).astype(BF16)


def _proj_call(x2, sh1, sc1, posf, p, B, S):
    T = B * S
    tm = TOK_BLOCK
    nb = S // tm
    row = lambda i: (i, 0)
    per_b = lambda i: (i // nb, 0, 0)
    full = lambda i: (0, 0)

    def wspec(a):
        return pl.BlockSpec(a.shape, full)

    weights = [p["g_norm1"], p["w_ql"], p["w_kvl"], p["w_kpe"], p["w_sq"], p["w_sk"], p["w_sv"],
               p["g_q_lat"], p["w_uq"], p["g_kv_lat"], p["w_uk"], p["w_v"], p["g_qk_q"],
               p["g_qk_k"], p["invf"]]
    out_w = [MLA_HEADS * HEAD_PAD, MLA_HEADS * HEAD_PAD, MLA_HEADS * V_DIM,
             SB_HEADS * SB_DIM, SB_HEADS * SB_DIM, SB_HEADS * SB_DIM]
    return pl.pallas_call(
        _proj_body,
        grid=(T // tm,),
        in_specs=[pl.BlockSpec((tm, D_MODEL), row),
                  pl.BlockSpec((1, 1, D_MODEL), per_b),
                  pl.BlockSpec((1, 1, D_MODEL), per_b),
                  pl.BlockSpec((tm, 1), row)] + [wspec(a) for a in weights],
        out_specs=[pl.BlockSpec((tm, w), row) for w in out_w],
        out_shape=[jax.ShapeDtypeStruct((T, w), BF16) for w in out_w],
        compiler_params=_cparams(("arbitrary",)),
        name="mix_proj",
    )(x2, sh1, sc1, posf, *weights)


_NT = (((1,), (1,)), ((), ()))


def _mla_rows(qh, k_ref, v_ref, sl, i, tq):
    kend = (i + 1) * tq
    s = lax.dot_general(qh, k_ref[0:kend, sl], _NT, preferred_element_type=F32) * (QK_DIM ** -0.5)
    row = lax.broadcasted_iota(I32, (tq, tq), 0)
    col = lax.broadcasted_iota(I32, (tq, tq), 1)
    s_diag = jnp.where((col // CHUNK) <= (row // CHUNK), s[:, kend - tq:], NEG_BIG)
    s = s_diag if i == 0 else jnp.concatenate([s[:, :kend - tq], s_diag], axis=1)
    pr = jnp.exp(s - jnp.max(s, axis=-1, keepdims=True))
    l = jnp.sum(pr, axis=-1, keepdims=True)
    return jnp.dot(pr.astype(BF16), v_ref[0:kend, :], preferred_element_type=F32) / l


def _mla_body(q_ref, k_ref, v_ref, o_ref):
    tq = q_ref.shape[0]
    i = pl.program_id(2)
    lane = lax.broadcasted_iota(I32, (tq, LANES), 1)
    for ii in range(k_ref.shape[0] // tq):
        @pl.when(i == ii)
        def _():
            outs = []
            for hh in range(2):
                sl = slice(hh * HEAD_PAD, (hh + 1) * HEAD_PAD)
                outs.append(_mla_rows(q_ref[:, sl], k_ref, v_ref, sl, ii, tq))
            o_ref[...] = jnp.where(lane < V_DIM, outs[0], outs[1]).astype(BF16)


def _mla_call(q, k, v, B, S):
    tq = ATT_BLOCK
    q3 = q.reshape(B, S, MLA_HEADS * HEAD_PAD)
    k3 = k.reshape(B, S, MLA_HEADS * HEAD_PAD)
    v3 = v.reshape(B, S, MLA_HEADS * V_DIM)
    out = pl.pallas_call(
        _mla_body,
        grid=(B, MLA_HEADS // 2, S // tq),
        in_specs=[pl.BlockSpec((None, tq, 2 * HEAD_PAD), lambda b, hp, i: (b, i, hp)),
                  pl.BlockSpec((None, S, 2 * HEAD_PAD), lambda b, hp, i: (b, 0, hp)),
                  pl.BlockSpec((None, S, 2 * V_DIM), lambda b, hp, i: (b, 0, hp))],
        out_specs=pl.BlockSpec((None, tq, 2 * V_DIM), lambda b, hp, i: (b, i, hp)),
        out_shape=jax.ShapeDtypeStruct((B, S, MLA_HEADS * V_DIM), BF16),
        compiler_params=_cparams(("arbitrary", "arbitrary", "arbitrary")),
        name="mla_attn",
    )(q3, k3, v3)
    return out.reshape(B * S, MLA_HEADS * V_DIM)


def _sb_rows(qh, k_ref, v_ref, i, tq):
    kend = (i + 1) * tq
    z = lax.dot_general(qh, k_ref[0:kend, :], _NT, preferred_element_type=F32) * (SB_DIM ** -0.5)
    lk = -(jnp.maximum(z, 0.0) + jnp.log(1.0 + jnp.exp(-jnp.abs(z))))
    row = lax.broadcasted_iota(I32, (tq, tq), 0)
    col = lax.broadcasted_iota(I32, (tq, tq), 1)
    strict = col < row
    suffix_ones = jnp.where(row >= col, 1.0, 0.0).astype(BF16)
    later = jnp.zeros((tq, 1), F32)
    a_blocks = [None] * (i + 1)
    for j in range(i, -1, -1):
        cols = slice(j * tq, (j + 1) * tq)
        lkj = lk[:, cols]
        if j == i:
            lkj = jnp.where(strict, lkj, 0.0)
        lk_hi = lkj.astype(BF16)
        lk_lo = (lkj - lk_hi.astype(F32)).astype(BF16)
        suf = (jnp.dot(lk_hi, suffix_ones, preferred_element_type=F32)
               + jnp.dot(lk_lo, suffix_ones, preferred_element_type=F32))
        a = jnp.exp(z[:, cols] + suf + later)
        if j == i:
            a = jnp.where(strict, a, 0.0)
        a_blocks[j] = a.astype(BF16)
        later = later + suf[:, 0:1]
    a = a_blocks[0] if i == 0 else jnp.concatenate(a_blocks, axis=1)
    return jnp.dot(a, v_ref[0:kend, :], preferred_element_type=F32)


def _sb_body(q_ref, k_ref, v_ref, o_ref):
    tq = q_ref.shape[0]
    i = pl.program_id(2)
    lane = lax.broadcasted_iota(I32, (tq, LANES), 1)
    for ii in range(k_ref.shape[0] // tq):
        @pl.when(i == ii)
        def _():
            q = q_ref[...]
            outs = []
            for hh in range(2):
                in_head = (lane >= hh * SB_DIM) & (lane < (hh + 1) * SB_DIM)
                qh = jnp.where(in_head, q, jnp.zeros_like(q))
                outs.append(_sb_rows(qh, k_ref, v_ref, ii, tq))
            o_ref[...] = jnp.where(lane < SB_DIM, outs[0], outs[1]).astype(BF16)


def _sb_call(sq, sk, sv, B, S):
    tq = ATT_BLOCK
    w = SB_HEADS * SB_DIM
    args = [a.reshape(B, S, w) for a in (sq, sk, sv)]
    out = pl.pallas_call(
        _sb_body,
        grid=(B, SB_HEADS // 2, S // tq),
        in_specs=[pl.BlockSpec((None, tq, 2 * SB_DIM), lambda b, hp, i: (b, i, hp)),
                  pl.BlockSpec((None, S, 2 * SB_DIM), lambda b, hp, i: (b, 0, hp)),
                  pl.BlockSpec((None, S, 2 * SB_DIM), lambda b, hp, i: (b, 0, hp))],
        out_specs=pl.BlockSpec((None, tq, 2 * SB_DIM), lambda b, hp, i: (b, i, hp)),
        out_shape=jax.ShapeDtypeStruct((B, S, w), BF16),
        compiler_params=_cparams(("arbitrary", "arbitrary", "arbitrary")),
        name="sb_attn",
    )(*args)
    return out.reshape(B * S, w)


def _pack_pairs(h):
    n = h.shape[1] // 2
    lo = pltpu.bitcast(h[:, :n].astype(BF16).astype(F32), U32)
    hi = pltpu.bitcast(h[:, n:].astype(BF16).astype(F32), U32)
    return (lo >> 16) | (hi & jnp.uint32(0xFFFF0000))


def _unpack_pairs(w):
    lo = pltpu.bitcast(w << 16, F32).astype(BF16)
    hi = pltpu.bitcast(w & jnp.uint32(0xFFFF0000), F32).astype(BF16)
    return lo, hi


def _merge_body(x_ref, sh1_ref, sc1_ref, ga1_ref, sh2_ref, sc2_ref, om_ref, os_ref,
                gn1_ref, wga_ref, wgb_ref, wom_ref, wos_ref, wout_ref, gn2_ref,
                wrh_ref, wrl_ref, br_ref,
                x1_ref, h2_ref, lg_ref):
    x = x_ref[...]
    h = _modulated_norm(x, gn1_ref[...], sh1_ref[0], sc1_ref[0]).astype(BF16)
    ga = jax.nn.sigmoid(jnp.dot(h, wga_ref[...], preferred_element_type=F32))
    gb = jax.nn.sigmoid(jnp.dot(h, wgb_ref[...], preferred_element_type=F32))
    merged = (ga * jnp.dot(om_ref[...], wom_ref[...], preferred_element_type=F32)
              + gb * jnp.dot(os_ref[...], wos_ref[...], preferred_element_type=F32))
    y = jnp.dot(merged.astype(BF16), wout_ref[...], preferred_element_type=F32)
    x1 = x + ga1_ref[0] * y
    x1_ref[...] = x1
    h2 = _modulated_norm(x1, gn2_ref[...], sh2_ref[0], sc2_ref[0])
    h2_ref[...] = _pack_pairs(h2)
    h2_hi = h2.astype(BF16)
    h2_lo = (h2 - h2_hi.astype(F32)).astype(BF16)
    lg_ref[...] = (jnp.dot(h2_hi, wrh_ref[...], preferred_element_type=F32)
                   + jnp.dot(h2_hi, wrl_ref[...], preferred_element_type=F32)
                   + jnp.dot(h2_lo, wrh_ref[...], preferred_element_type=F32)
                   + br_ref[...])


def _merge_call(x2, mods, o_mla, o_sb, p, B, S):
    T = B * S
    tm = TOK_BLOCK
    nb = S // tm
    row = lambda i: (i, 0)
    per_b = lambda i: (i // nb, 0, 0)
    full = lambda i: (0, 0)
    weights = [p["g_norm1"], p["w_ga"], p["w_gb"], p["w_o_mla"], p["w_o_sb"], p["w_out"],
               p["g_norm2"], p["w_r_hi"], p["w_r_lo"], p["b_r"]]
    return pl.pallas_call(
        _merge_body,
        grid=(T // tm,),
        in_specs=[pl.BlockSpec((tm, D_MODEL), row)]
        + [pl.BlockSpec((1, 1, D_MODEL), per_b)] * 5
        + [pl.BlockSpec((tm, MLA_HEADS * V_DIM), row), pl.BlockSpec((tm, SB_HEADS * SB_DIM), row)]
        + [pl.BlockSpec(a.shape, full) for a in weights],
        out_specs=[pl.BlockSpec((tm, D_MODEL), row), pl.BlockSpec((tm, D_MODEL // 2), row),
                   pl.BlockSpec((tm, LANES), row)],
        out_shape=[jax.ShapeDtypeStruct((T, D_MODEL), F32),
                   jax.ShapeDtypeStruct((T, D_MODEL // 2), U32),
                   jax.ShapeDtypeStruct((T, LANES), F32)],
        compiler_params=_cparams(("arbitrary",)),
        name="merge_norm",
    )(x2, *mods, o_mla, o_sb, *weights)


def _route_body(lg_ref, tri_ref, e_ref, r_ref, w_ref, cnt_ref, run_ref):
    @pl.when(pl.program_id(0) == 0)
    def _():
        run_ref[...] = jnp.zeros_like(run_ref)

    v = lg_ref[...]
    tm = v.shape[0]
    lane_i = lax.broadcasted_iota(I32, (tm, LANES), 1)
    lane = lane_i.astype(F32)
    vals, idxs = [], []
    for _ in range(TOP_K):
        m = jnp.max(v, axis=-1, keepdims=True)
        idx = jnp.min(jnp.where(v == m, lane, float(LANES)), axis=-1, keepdims=True)
        vals.append(m)
        idxs.append(idx)
        v = jnp.where(lane == idx, NEG_BIG, v)
    ex = [jnp.exp(t - vals[0]) for t in vals]
    denom = ex[0] + ex[1] + ex[2] + ex[3]
    onehots = [lane == idx for idx in idxs]
    oh = jnp.zeros((tm, LANES), F32)
    for o in onehots:
        oh = oh + jnp.where(o, 1.0, 0.0)
    before = jnp.dot(tri_ref[...], oh.astype(BF16), preferred_element_type=F32) + run_ref[...]
    e_out = jnp.zeros((tm, LANES), I32)
    r_out = jnp.zeros((tm, LANES), I32)
    w_out = jnp.zeros((tm, LANES), F32)
    for kk in range(TOP_K):
        rank = jnp.sum(jnp.where(onehots[kk], before, 0.0), axis=-1, keepdims=True)
        e_out = jnp.where(lane_i == kk, idxs[kk].astype(I32), e_out)
        r_out = jnp.where(lane_i == kk, rank.astype(I32), r_out)
        w_out = jnp.where(lane_i == kk, ex[kk] / denom, w_out)
    e_ref[...] = e_out[:, :TOP_K]
    r_ref[...] = r_out[:, :TOP_K]
    w_ref[...] = w_out[:, :TOP_K]
    run_ref[...] = run_ref[...] + jnp.sum(oh, axis=0, keepdims=True)
    cnt_ref[...] = run_ref[...]


def _route_call(logits, T):
    tm = ROUTE_BLOCK
    tri = (np.arange(tm)[:, None] > np.arange(tm)[None, :]).astype(np.float32)
    tri = jnp.asarray(tri, dtype=BF16)
    row = lambda i: (i, 0)
    return pl.pallas_call(
        _route_body,
        grid=(T // tm,),
        in_specs=[pl.BlockSpec((tm, LANES), row), pl.BlockSpec((tm, tm), lambda i: (0, 0))],
        out_specs=[pl.BlockSpec((tm, TOP_K), row), pl.BlockSpec((tm, TOP_K), row),
                   pl.BlockSpec((tm, TOP_K), row), pl.BlockSpec((1, LANES), lambda i: (0, 0))],
        out_shape=[jax.ShapeDtypeStruct((T, TOP_K), I32), jax.ShapeDtypeStruct((T, TOP_K), I32),
                   jax.ShapeDtypeStruct((T, TOP_K), F32), jax.ShapeDtypeStruct((1, LANES), F32)],
        scratch_shapes=[pltpu.VMEM((1, LANES), F32)],
        compiler_params=_cparams(("arbitrary",)),
        name="route_topk",
    )(logits, tri)


def _dispatch_body(ps_ref, e_ref, r_ref, h_ref, z_hbm, xs_hbm, sem):
    del z_hbm

    def issue(t, c):
        for kk in range(TOP_K):
            a = t * TOP_K + kk
            dst = ps_ref[e_ref[a]] + r_ref[a]
            pltpu.make_async_copy(h_ref.at[pl.ds(t, 1)], xs_hbm.at[pl.ds(dst, 1)], sem).start()
        return c

    lax.fori_loop(0, ROW_BLOCK, issue, 0)
    for _ in range(TOP_K):
        pltpu.make_async_copy(h_ref, xs_hbm.at[pl.ds(0, ROW_BLOCK)], sem).wait()


def _dispatch_call(pstart, e_flat, r_flat, h2p, n_rows):
    T = h2p.shape[0]
    nblk = ROW_BLOCK * TOP_K
    zeros = jnp.zeros((n_rows, D_MODEL // 2), U32)
    smem_blk = pl.BlockSpec((nblk,), lambda i, ps: (i,), memory_space=pltpu.SMEM)
    return pl.pallas_call(
        _dispatch_body,
        grid_spec=pltpu.PrefetchScalarGridSpec(
            num_scalar_prefetch=1,
            grid=(T // ROW_BLOCK,),
            in_specs=[smem_blk, smem_blk,
                      pl.BlockSpec((ROW_BLOCK, D_MODEL // 2), lambda i, ps: (i, 0)),
                      pl.BlockSpec(memory_space=pl.ANY)],
            out_specs=pl.BlockSpec(memory_space=pl.ANY),
            scratch_shapes=[pltpu.SemaphoreType.DMA(())]),
        out_shape=jax.ShapeDtypeStruct((n_rows, D_MODEL // 2), U32),
        input_output_aliases={4: 0},
        compiler_params=_cparams(("arbitrary",)),
        name="moe_dispatch",
    )(pstart, e_flat, r_flat, h2p, zeros)


def _expert_body(te_ref, tv_ref, xs_ref, wgu_ref, bgu_ref, wd_ref, bd_ref, y_ref):
    i = pl.program_id(0)

    @pl.when(tv_ref[i] == 1)
    def _():
        lo, hi = _unpack_pairs(xs_ref[...])
        half = D_MODEL // 2
        gu = (jnp.dot(lo, wgu_ref[:half, :], preferred_element_type=F32)
              + jnp.dot(hi, wgu_ref[half:, :], preferred_element_type=F32) + bgu_ref[...])
        gate = jnp.minimum(gu[:, :D_EXPERT], SWIGLU_LIMIT)
        up = jnp.clip(gu[:, D_EXPERT:], -SWIGLU_LIMIT, SWIGLU_LIMIT)
        hid = (up + 1.0) * (gate * jax.nn.sigmoid(SWIGLU_ALPHA * gate))
        y_ref[...] = jnp.dot(hid.astype(BF16), wd_ref[...], preferred_element_type=F32) + bd_ref[...]

    @pl.when(tv_ref[i] == 0)
    def _():
        y_ref[...] = jnp.zeros_like(y_ref)


def _expert_call(tile_e, tile_v, xs, p):
    n_rows = xs.shape[0]
    tmx = EXPERT_TILE
    by_e = lambda i, te, tv: (te[i], 0, 0)
    return pl.pallas_call(
        _expert_body,
        grid_spec=pltpu.PrefetchScalarGridSpec(
            num_scalar_prefetch=2,
            grid=(n_rows // tmx,),
            in_specs=[pl.BlockSpec((tmx, D_MODEL // 2), lambda i, te, tv: (i, 0)),
                      pl.BlockSpec((None, D_MODEL, 2 * D_EXPERT), by_e),
                      pl.BlockSpec((None, 1, 2 * D_EXPERT), by_e),
                      pl.BlockSpec((None, D_EXPERT, D_MODEL), by_e),
                      pl.BlockSpec((None, 1, D_MODEL), by_e)],
            out_specs=pl.BlockSpec((tmx, D_MODEL), lambda i, te, tv: (i, 0))),
        out_shape=jax.ShapeDtypeStruct((n_rows, D_MODEL), F32),
        compiler_params=_cparams(("arbitrary",)),
        name="moe_experts",
    )(tile_e, tile_v, xs, p["w_gate_up"], p["b_gate_up"], p["w_down"], p["b_down"])


def _combine_body(ps_ref, e_ref, r_ref, x1_ref, w_ref, ga2_ref, ys_hbm, o_ref, buf, sem):
    def row_copy(src_row, kk, t):
        return pltpu.make_async_copy(ys_hbm.at[pl.ds(src_row, 1)], buf.at[kk, pl.ds(t, 1)], sem)

    def issue(t, c):
        for kk in range(TOP_K):
            a = t * TOP_K + kk
            row_copy(ps_ref[e_ref[a]] + r_ref[a], kk, t).start()
        return c

    lax.fori_loop(0, ROW_BLOCK, issue, 0)
    for kk in range(TOP_K):
        pltpu.make_async_copy(ys_hbm.at[pl.ds(0, ROW_BLOCK)], buf.at[kk], sem).wait()

    w = w_ref[...]
    acc = w[:, 0:1] * buf[0]
    for kk in range(1, TOP_K):
        acc = acc + w[:, kk:kk + 1] * buf[kk]
    o_ref[...] = x1_ref[...] + ga2_ref[0] * acc


def _combine_call(pstart, e_flat, r_flat, x1, wts, ga2, ys, B, S):
    T = B * S
    nblk = ROW_BLOCK * TOP_K
    nb = S // ROW_BLOCK
    smem_blk = pl.BlockSpec((nblk,), lambda i, ps: (i,), memory_space=pltpu.SMEM)
    return pl.pallas_call(
        _combine_body,
        grid_spec=pltpu.PrefetchScalarGridSpec(
            num_scalar_prefetch=1,
            grid=(T // ROW_BLOCK,),
            in_specs=[smem_blk, smem_blk,
                      pl.BlockSpec((ROW_BLOCK, D_MODEL), lambda i, ps: (i, 0)),
                      pl.BlockSpec((ROW_BLOCK, TOP_K), lambda i, ps: (i, 0)),
                      pl.BlockSpec((1, 1, D_MODEL), lambda i, ps: (i // nb, 0, 0)),
                      pl.BlockSpec(memory_space=pl.ANY)],
            out_specs=pl.BlockSpec((ROW_BLOCK, D_MODEL), lambda i, ps: (i, 0)),
            scratch_shapes=[pltpu.VMEM((TOP_K, ROW_BLOCK, D_MODEL), F32),
                            pltpu.SemaphoreType.DMA(())]),
        out_shape=jax.ShapeDtypeStruct((T, D_MODEL), F32),
        compiler_params=_cparams(("arbitrary",)),
        name="moe_combine",
    )(pstart, e_flat, r_flat, x1, wts, ga2, ys)


def _pad_heads(w, per_head, n_heads):
    k = w.shape[0]
    w = w.reshape(k, n_heads, per_head)
    w = jnp.pad(w, ((0, 0), (0, 0), (0, HEAD_PAD - per_head)))
    return w.reshape(k, n_heads * HEAD_PAD)


def _prepare(w_in, g_norm1, g_q_lat, w_uq, g_kv_lat, w_ukv, g_qk_q, g_qk_k, w_o_mla, w_o_sb,
             w_out, g_norm2, w_router, b_router, w_gate_up, b_gate_up, w_down, b_down):
    c0 = Q_LORA
    c1 = c0 + KV_LORA
    c2 = c1 + QK_ROPE
    sbw = SB_HEADS * SB_DIM
    c3 = c2 + 3 * sbw
    c4 = c3 + D_MODEL
    p = {}
    p["g_norm1"] = g_norm1.reshape(1, D_MODEL)
    p["g_norm2"] = g_norm2.reshape(1, D_MODEL)
    p["w_ql"] = w_in[:, :c0].astype(BF16)
    p["w_kvl"] = w_in[:, c0:c1].astype(BF16)
    p["w_kpe"] = jnp.pad(w_in[:, c1:c2], ((0, 0), (QK_NOPE, HEAD_PAD - QK_DIM))).astype(BF16)
    p["w_sq"] = w_in[:, c2:c2 + sbw].astype(BF16)
    p["w_sk"] = w_in[:, c2 + sbw:c2 + 2 * sbw].astype(BF16)
    p["w_sv"] = w_in[:, c2 + 2 * sbw:c3].astype(BF16)
    p["w_ga"] = w_in[:, c3:c4].astype(BF16)
    p["w_gb"] = w_in[:, c4:].astype(BF16)
    p["g_q_lat"] = g_q_lat.reshape(1, Q_LORA)
    p["g_kv_lat"] = g_kv_lat.reshape(1, KV_LORA)
    p["w_uq"] = _pad_heads(w_uq, QK_DIM, MLA_HEADS).astype(BF16)
    kv = w_ukv.reshape(KV_LORA, MLA_HEADS, QK_NOPE + V_DIM)
    p["w_uk"] = _pad_heads(kv[:, :, :QK_NOPE].reshape(KV_LORA, MLA_HEADS * QK_NOPE),
                           QK_NOPE, MLA_HEADS).astype(BF16)
    p["w_v"] = kv[:, :, QK_NOPE:].reshape(KV_LORA, MLA_HEADS * V_DIM).astype(BF16)
    p["g_qk_q"] = jnp.pad(g_qk_q, (0, HEAD_PAD - QK_DIM)).reshape(1, HEAD_PAD)
    p["g_qk_k"] = jnp.pad(g_qk_k, (0, HEAD_PAD - QK_DIM)).reshape(1, HEAD_PAD)
    inv_freq = 1.0 / (ROPE_THETA ** (jnp.arange(0, QK_ROPE, 2, dtype=F32) / QK_ROPE))
    p["invf"] = jnp.concatenate([jnp.zeros((QK_NOPE,), F32), inv_freq, inv_freq,
                                 jnp.zeros((HEAD_PAD - QK_DIM,), F32)]).reshape(1, HEAD_PAD)
    p["w_o_mla"] = w_o_mla.astype(BF16)
    p["w_o_sb"] = w_o_sb.astype(BF16)
    p["w_out"] = w_out.astype(BF16)
    wr = jnp.pad(w_router, ((0, 0), (0, LANES - N_EXPERTS)))
    p["w_r_hi"] = wr.astype(BF16)
    p["w_r_lo"] = (wr - p["w_r_hi"].astype(F32)).astype(BF16)
    p["b_r"] = jnp.concatenate([b_router, jnp.full((LANES - N_EXPERTS,), NEG_BIG, F32)]).reshape(1, LANES)
    p["w_gate_up"] = w_gate_up.astype(BF16)
    p["b_gate_up"] = b_gate_up.reshape(N_EXPERTS, 1, 2 * D_EXPERT)
    p["w_down"] = w_down.astype(BF16)
    p["b_down"] = b_down.reshape(N_EXPERTS, 1, D_MODEL)
    return p


def _layer(x2, c, posf, B, S, w_ada, b_ada, *layer_weights):
    T = B * S
    p = _prepare(*layer_weights)
    mod = _ada_call(c, w_ada, b_ada)
    sh1, sc1, ga1, sh2, sc2, ga2 = [m.reshape(B, 1, D_MODEL) for m in jnp.split(mod, 6, axis=-1)]

    q, k, v, sq, sk, sv = _proj_call(x2, sh1, sc1, posf, p, B, S)
    o_mla = _mla_call(q, k, v, B, S)
    o_sb = _sb_call(sq, sk, sv, B, S)
    x1, h2p, logits = _merge_call(x2, (sh1, sc1, ga1, sh2, sc2), o_mla, o_sb, p, B, S)

    e_idx, rank, wts, counts = _route_call(logits, T)
    cnt = counts[0, :N_EXPERTS].astype(I32)
    tiles_e = (cnt + EXPERT_TILE - 1) // EXPERT_TILE
    tile_end = jnp.cumsum(tiles_e)
    pstart = ((tile_end - tiles_e) * EXPERT_TILE).astype(I32)
    n_tiles = (T * TOP_K) // EXPERT_TILE + N_EXPERTS
    tile_ids = jnp.arange(n_tiles, dtype=I32)
    tile_e = jnp.minimum(jnp.sum((tile_ids[:, None] >= tile_end[None, :]).astype(I32), axis=1),
                         N_EXPERTS - 1).astype(I32)
    tile_v = (tile_ids < tile_end[-1]).astype(I32)

    e_flat = e_idx.reshape(T * TOP_K)
    r_flat = rank.reshape(T * TOP_K)
    xs = _dispatch_call(pstart, e_flat, r_flat, h2p, n_tiles * EXPERT_TILE)
    ys = _expert_call(tile_e, tile_v, xs, p)
    return _combine_call(pstart, e_flat, r_flat, x1, wts, ga2, ys, B, S)


def kernel(x, c, positions, w_ada, b_ada, g_norm1, w_in, g_q_lat, w_uq, g_kv_lat, w_ukv, g_qk_q,
           g_qk_k, w_o_mla, w_o_sb, w_out, g_norm2, w_router, b_router, w_gate_up, b_gate_up,
           w_down, b_down):
    B, S, D = x.shape
    x2 = x.reshape(B * S, D)
    posf = positions.astype(F32).reshape(B * S, 1)
    for l in range(w_ada.shape[0]):
        x2 = _layer(x2, c, posf, B, S, w_ada[l], b_ada[l], w_in[l], g_norm1[l], g_q_lat[l],
                    w_uq[l], g_kv_lat[l], w_ukv[l], g_qk_q[l], g_qk_k[l], w_o_mla[l], w_o_sb[l],
                    w_out[l], g_norm2[l], w_router[l], b_router[l], w_gate_up[l], b_gate_up[l],
                    w_down[l], b_down[l])
    return x2.reshape(B, S, D)
```

```python
import functools

import jax
import jax.numpy as jnp
import numpy as np
from jax import lax
from jax.experimental import pallas as pl
from jax.experimental.pallas import tpu as pltpu

F32 = jnp.float32
BF16 = jnp.bfloat16
I32 = jnp.int32
U32 = jnp.uint32

D_MODEL = 1024
EPS = 1e-6
CHUNK = 64
MLA_HEADS = 8
Q_LORA = 384
KV_LORA = 256
QK_NOPE = 64
QK_ROPE = 32
V_DIM = 64
QK_DIM = QK_NOPE + QK_ROPE
ROPE_THETA = 10000.0
SB_HEADS = 8
SB_DIM = 64
N_EXPERTS = 32
TOP_K = 4
D_EXPERT = D_MODEL
SWIGLU_LIMIT = 7.0
SWIGLU_ALPHA = 1.702

LANES = 128
HEAD_PAD = LANES
TOK_BLOCK = 512
ATT_BLOCK = 256
ROUTE_BLOCK = 512
ROW_BLOCK = 256
EXPERT_TILE = 512
NEG_BIG = -1e30
LOG2E = 1.4426950408889634
MLA_LOGIT_SCALE = QK_DIM ** -0.5 * LOG2E
SB_LOGIT_SCALE = SB_DIM ** -0.5 * LOG2E
VMEM_LIMIT = 48 * 1024 * 1024


def _cparams(sem):
    return pltpu.CompilerParams(dimension_semantics=sem, vmem_limit_bytes=VMEM_LIMIT)


def _ada_body(c_ref, w_ref, b_ref, o_ref):
    c = c_ref[...]
    s = c * jax.nn.sigmoid(c)
    o_ref[...] = jnp.dot(s, w_ref[...], preferred_element_type=F32,
                         precision=lax.Precision.HIGHEST) + b_ref[...]


def _ada_call(c, w_ada, b_ada):
    B = c.shape[0]
    n = w_ada.shape[1]
    bn = 512
    return pl.pallas_call(
        _ada_body,
        grid=(n // bn,),
        in_specs=[pl.BlockSpec((B, D_MODEL), lambda j: (0, 0)),
                  pl.BlockSpec((D_MODEL, bn), lambda j: (0, j)),
                  pl.BlockSpec((1, bn), lambda j: (0, j))],
        out_specs=pl.BlockSpec((B, bn), lambda j: (0, j)),
        out_shape=jax.ShapeDtypeStruct((B, n), F32),
        compiler_params=_cparams(("arbitrary",)),
        name="ada_mod",
    )(c, w_ada, b_ada.reshape(1, n))


def _rms(v, width):
    return lax.rsqrt(jnp.sum(v * v, axis=-1, keepdims=True) * (1.0 / width) + EPS)


def _modulated_norm(x, g, sh, sc):
    h = x * _rms(x, D_MODEL) * g
    return h * (1.0 + sc) + sh


def _proj_body(x_ref, sh_ref, sc_ref, pos_ref, gn_ref, wql_ref, wkvl_ref, wkpe_ref,
               wsq_ref, wsk_ref, wsv_ref, gql_ref, wuq_ref, gkvl_ref, wuk_ref, wv_ref,
               gq_ref, gk_ref, invf_ref,
               q_ref, k_ref, v_ref, sq_ref, sk_ref, sv_ref):
    x = x_ref[...]
    h = _modulated_norm(x, gn_ref[...], sh_ref[0], sc_ref[0]).astype(BF16)

    sq_ref[...] = jnp.dot(h, wsq_ref[...], preferred_element_type=F32).astype(BF16)
    sk_ref[...] = jnp.dot(h, wsk_ref[...], preferred_element_type=F32).astype(BF16)
    sv_ref[...] = jnp.dot(h, wsv_ref[...], preferred_element_type=F32).astype(BF16)

    q_lat = jnp.dot(h, wql_ref[...], preferred_element_type=F32)
    kv_lat = jnp.dot(h, wkvl_ref[...], preferred_element_type=F32)
    kpe = jnp.dot(h, wkpe_ref[...], preferred_element_type=F32)

    qn = (q_lat * _rms(q_lat, Q_LORA) * gql_ref[...]).astype(BF16)
    kvn = (kv_lat * _rms(kv_lat, KV_LORA) * gkvl_ref[...]).astype(BF16)
    q = jnp.dot(qn, wuq_ref[...], preferred_element_type=F32)
    kn = jnp.dot(kvn, wuk_ref[...], preferred_element_type=F32)
    v_ref[...] = jnp.dot(kvn, wv_ref[...], preferred_element_type=F32).astype(BF16)

    tm = x.shape[0]
    lane = lax.broadcasted_iota(I32, (tm, LANES), 1)
    ang = pos_ref[...] * invf_ref[...]
    cosf = jnp.cos(ang)
    sinf = jnp.sin(ang)
    half = QK_ROPE // 2
    s_lo = jnp.where((lane >= QK_NOPE) & (lane < QK_NOPE + half), -sinf, 0.0)
    s_hi = jnp.where((lane >= QK_NOPE + half) & (lane < QK_DIM), sinf, 0.0)

    def rope(t):
        return (t * cosf + pltpu.roll(t, LANES - half, 1) * s_lo
                + pltpu.roll(t, half, 1) * s_hi)

    gq = gq_ref[...]
    gk = gk_ref[...]
    kpe_sq = jnp.sum(kpe * kpe, axis=-1, keepdims=True)
    kr = rope(kpe * gk)
    for hh in range(MLA_HEADS):
        sl = slice(hh * HEAD_PAD, (hh + 1) * HEAD_PAD)
        qh = q[:, sl]
        qh = qh * _rms(qh, QK_DIM) * gq
        q_ref[:, sl] = (rope(qh) * MLA_LOGIT_SCALE).astype(BF16)
        kh = kn[:, sl]
        rk = lax.rsqrt((jnp.sum(kh * kh, axis=-1, keepdims=True) + kpe_sq) * (1.0 / QK_DIM) + EPS)
        k_ref[:, sl] = ((kh * gk + kr) * rk).astype(BF16)


def _proj_call(x2, sh1, sc1, posf, p, B, S):
    T = B * S
    tm = TOK_BLOCK
    nb = S // tm
    row = lambda i: (i, 0)
    per_b = lambda i: (i // nb, 0, 0)
    full = lambda i: (0, 0)

    def wspec(a):
        return pl.BlockSpec(a.shape, full)

    weights = [p["g_norm1"], p["w_ql"], p["w_kvl"], p["w_kpe"], p["w_sq"], p["w_sk"], p["w_sv"],
               p["g_q_lat"], p["w_uq"], p["g_kv_lat"], p["w_uk"], p["w_v"], p["g_qk_q"],
               p["g_qk_k"], p["invf"]]
    out_w = [MLA_HEADS * HEAD_PAD, MLA_HEADS * HEAD_PAD, MLA_HEADS * V_DIM,
             SB_HEADS * SB_DIM, SB_HEADS * SB_DIM, SB_HEADS * SB_DIM]
    return pl.pallas_call(
        _proj_body,
        grid=(T // tm,),
        in_specs=[pl.BlockSpec((tm, D_MODEL), row),
                  pl.BlockSpec((1, 1, D_MODEL), per_b),
                  pl.BlockSpec((1, 1, D_MODEL), per_b),
                  pl.BlockSpec((tm, 1), row)] + [wspec(a) for a in weights],
        out_specs=[pl.BlockSpec((tm, w), row) for w in out_w],
        out_shape=[jax.ShapeDtypeStruct((T, w), BF16) for w in out_w],
        compiler_params=_cparams(("arbitrary",)),
        name="mix_proj",
    )(x2, sh1, sc1, posf, *weights)


_NT = (((1,), (1,)), ((), ()))


def _mla_rows(qh, k_ref, v_ref, sl, i, tq):
    kend = (i + 1) * tq
    s = lax.dot_general(qh, k_ref[0:kend, sl], _NT, preferred_element_type=F32)
    row = lax.broadcasted_iota(I32, (tq, tq), 0)
    col = lax.broadcasted_iota(I32, (tq, tq), 1)
    s_diag = jnp.where((col // CHUNK) <= (row // CHUNK), s[:, kend - tq:], NEG_BIG)
    s = s_diag if i == 0 else jnp.concatenate([s[:, :kend - tq], s_diag], axis=1)
    pr = jnp.exp2(s - jnp.max(s, axis=-1, keepdims=True))
    l = jnp.sum(pr, axis=-1, keepdims=True)
    return jnp.dot(pr.astype(BF16), v_ref[0:kend, :], preferred_element_type=F32) / l


def _mla_body(q_ref, k_ref, v_ref, o_ref):
    tq = q_ref.shape[0]
    i = pl.program_id(2)
    lane = lax.broadcasted_iota(I32, (tq, LANES), 1)
    for ii in range(k_ref.shape[0] // tq):
        @pl.when(i == ii)
        def _():
            outs = []
            for hh in range(2):
                sl = slice(hh * HEAD_PAD, (hh + 1) * HEAD_PAD)
                outs.append(_mla_rows(q_ref[:, sl], k_ref, v_ref, sl, ii, tq))
            o_ref[...] = jnp.where(lane < V_DIM, outs[0], outs[1]).astype(BF16)


def _mla_call(q, k, v, B, S):
    tq = ATT_BLOCK
    q3 = q.reshape(B, S, MLA_HEADS * HEAD_PAD)
    k3 = k.reshape(B, S, MLA_HEADS * HEAD_PAD)
    v3 = v.reshape(B, S, MLA_HEADS * V_DIM)
    out = pl.pallas_call(
        _mla_body,
        grid=(B, MLA_HEADS // 2, S // tq),
        in_specs=[pl.BlockSpec((None, tq, 2 * HEAD_PAD), lambda b, hp, i: (b, i, hp)),
                  pl.BlockSpec((None, S, 2 * HEAD_PAD), lambda b, hp, i: (b, 0, hp)),
                  pl.BlockSpec((None, S, 2 * V_DIM), lambda b, hp, i: (b, 0, hp))],
        out_specs=pl.BlockSpec((None, tq, 2 * V_DIM), lambda b, hp, i: (b, i, hp)),
        out_shape=jax.ShapeDtypeStruct((B, S, MLA_HEADS * V_DIM), BF16),
        compiler_params=_cparams(("arbitrary", "arbitrary", "arbitrary")),
        name="mla_attn",
    )(q3, k3, v3)
    return out.reshape(B * S, MLA_HEADS * V_DIM)


def _sb_rows(qh, k_ref, v_ref, i, tq):
    kend = (i + 1) * tq
    z = lax.dot_general(qh, k_ref[0:kend, :], _NT, preferred_element_type=F32) * SB_LOGIT_SCALE
    sp = jnp.maximum(z, 0.0) + jnp.log2(1.0 + jnp.exp2(jnp.minimum(z, -z)))
    row = lax.broadcasted_iota(I32, (tq, tq), 0)
    col = lax.broadcasted_iota(I32, (tq, tq), 1)
    strict = col < row
    suffix_ones = jnp.where(row >= col, 1.0, 0.0).astype(BF16)
    later = jnp.zeros((tq, 1), F32)
    a_blocks = [None] * (i + 1)
    for j in range(i, -1, -1):
        cols = slice(j * tq, (j + 1) * tq)
        spj = sp[:, cols]
        if j == i:
            spj = jnp.where(strict, spj, 0.0)
        suf = jnp.dot(spj.astype(BF16), suffix_ones, preferred_element_type=F32)
        a = jnp.exp2(z[:, cols] - suf - later)
        if j == i:
            a = jnp.where(strict, a, 0.0)
        a_blocks[j] = a.astype(BF16)
        later = later + suf[:, 0:1]
    a = a_blocks[0] if i == 0 else jnp.concatenate(a_blocks, axis=1)
    return jnp.dot(a, v_ref[0:kend, :], preferred_element_type=F32)


def _sb_body(q_ref, k_ref, v_ref, o_ref):
    tq = q_ref.shape[0]
    i = pl.program_id(2)
    lane = lax.broadcasted_iota(I32, (tq, LANES), 1)
    for ii in range(k_ref.shape[0] // tq):
        @pl.when(i == ii)
        def _():
            q = q_ref[...]
            outs = []
            for hh in range(2):
                in_head = (lane >= hh * SB_DIM) & (lane < (hh + 1) * SB_DIM)
                qh = jnp.where(in_head, q, jnp.zeros_like(q))
                outs.append(_sb_rows(qh, k_ref, v_ref, ii, tq))
            o_ref[...] = jnp.where(lane < SB_DIM, outs[0], outs[1]).astype(BF16)


def _sb_call(sq, sk, sv, B, S):
    tq = ATT_BLOCK
    w = SB_HEADS * SB_DIM
    args = [a.reshape(B, S, w) for a in (sq, sk, sv)]
    out = pl.pallas_call(
        _sb_body,
        grid=(B, SB_HEADS // 2, S // tq),
        in_specs=[pl.BlockSpec((None, tq, 2 * SB_DIM), lambda b, hp, i: (b, i, hp)),
                  pl.BlockSpec((None, S, 2 * SB_DIM), lambda b, hp, i: (b, 0, hp)),
                  pl.BlockSpec((None, S, 2 * SB_DIM), lambda b, hp, i: (b, 0, hp))],
        out_specs=pl.BlockSpec((None, tq, 2 * SB_DIM), lambda b, hp, i: (b, i, hp)),
        out_shape=jax.ShapeDtypeStruct((B, S, w), BF16),
        compiler_params=_cparams(("arbitrary", "arbitrary", "arbitrary")),
        name="sb_attn",
    )(*args)
    return out.reshape(B * S, w)


def _pack_pairs(h):
    n = h.shape[1] // 2
    lo = pltpu.bitcast(h[:, :n].astype(BF16).astype(F32), U32)
    hi = pltpu.bitcast(h[:, n:].astype(BF16).astype(F32), U32)
    return (lo >> 16) | (hi & jnp.uint32(0xFFFF0000))


def _unpack_pairs(w):
    lo = pltpu.bitcast(w << 16, F32).astype(BF16)
    hi = pltpu.bitcast(w & jnp.uint32(0xFFFF0000), F32).astype(BF16)
    return lo, hi


def _merge_body(x_ref, sh1_ref, sc1_ref, ga1_ref, sh2_ref, sc2_ref, om_ref, os_ref,
                gn1_ref, wga_ref, wgb_ref, wom_ref, wos_ref, wout_ref, gn2_ref,
                wrh_ref, wrl_ref, br_ref,
                x1_ref, h2_ref, lg_ref):
    x = x_ref[...]
    h = _modulated_norm(x, gn1_ref[...], sh1_ref[0], sc1_ref[0]).astype(BF16)
    ga = jax.nn.sigmoid(jnp.dot(h, wga_ref[...], preferred_element_type=F32))
    gb = jax.nn.sigmoid(jnp.dot(h, wgb_ref[...], preferred_element_type=F32))
    merged = (ga * jnp.dot(om_ref[...], wom_ref[...], preferred_element_type=F32)
              + gb * jnp.dot(os_ref[...], wos_ref[...], preferred_element_type=F32))
    y = jnp.dot(merged.astype(BF16), wout_ref[...], preferred_element_type=F32)
    x1 = x + ga1_ref[0] * y
    x1_ref[...] = x1
    h2 = _modulated_norm(x1, gn2_ref[...], sh2_ref[0], sc2_ref[0])
    h2_ref[...] = _pack_pairs(h2)
    h2_hi = h2.astype(BF16)
    h2_lo = (h2 - h2_hi.astype(F32)).astype(BF16)
    lg_ref[...] = (jnp.dot(h2_hi, wrh_ref[...], preferred_element_type=F32)
                   + jnp.dot(h2_hi, wrl_ref[...], preferred_element_type=F32)
                   + jnp.dot(h2_lo, wrh_ref[...], preferred_element_type=F32)
                   + br_ref[...])


def _merge_call(x2, mods, o_mla, o_sb, p, B, S):
    T = B * S
    tm = TOK_BLOCK
    nb = S // tm
    row = lambda i: (i, 0)
    per_b = lambda i: (i // nb, 0, 0)
    full = lambda i: (0, 0)
    weights = [p["g_norm1"], p["w_ga"], p["w_gb"], p["w_o_mla"], p["w_o_sb"], p["w_out"],
               p["g_norm2"], p["w_r_hi"], p["w_r_lo"], p["b_r"]]
    return pl.pallas_call(
        _merge_body,
        grid=(T // tm,),
        in_specs=[pl.BlockSpec((tm, D_MODEL), row)]
        + [pl.BlockSpec((1, 1, D_MODEL), per_b)] * 5
        + [pl.BlockSpec((tm, MLA_HEADS * V_DIM), row), pl.BlockSpec((tm, SB_HEADS * SB_DIM), row)]
        + [pl.BlockSpec(a.shape, full) for a in weights],
        out_specs=[pl.BlockSpec((tm, D_MODEL), row), pl.BlockSpec((tm, D_MODEL // 2), row),
                   pl.BlockSpec((tm, LANES), row)],
        out_shape=[jax.ShapeDtypeStruct((T, D_MODEL), F32),
                   jax.ShapeDtypeStruct((T, D_MODEL // 2), U32),
                   jax.ShapeDtypeStruct((T, LANES), F32)],
        compiler_params=_cparams(("arbitrary",)),
        name="merge_norm",
    )(x2, *mods, o_mla, o_sb, *weights)


def _route_body(lg_ref, tri_ref, e_ref, r_ref, w_ref, cnt_ref, run_ref):
    @pl.when(pl.program_id(0) == 0)
    def _():
        run_ref[...] = jnp.zeros_like(run_ref)

    v = lg_ref[...]
    tm = v.shape[0]
    lane_i = lax.broadcasted_iota(I32, (tm, LANES), 1)
    lane = lane_i.astype(F32)
    vals, idxs = [], []
    for _ in range(TOP_K):
        m = jnp.max(v, axis=-1, keepdims=True)
        idx = jnp.min(jnp.where(v == m, lane, float(LANES)), axis=-1, keepdims=True)
        vals.append(m)
        idxs.append(idx)
        v = jnp.where(lane == idx, NEG_BIG, v)
    ex = [jnp.exp(t - vals[0]) for t in vals]
    denom = ex[0] + ex[1] + ex[2] + ex[3]
    onehots = [lane == idx for idx in idxs]
    oh = jnp.zeros((tm, LANES), F32)
    for o in onehots:
        oh = oh + jnp.where(o, 1.0, 0.0)
    before = jnp.dot(tri_ref[...], oh.astype(BF16), preferred_element_type=F32) + run_ref[...]
    e_out = jnp.zeros((tm, LANES), I32)
    r_out = jnp.zeros((tm, LANES), I32)
    w_out = jnp.zeros((tm, LANES), F32)
    for kk in range(TOP_K):
        rank = jnp.sum(jnp.where(onehots[kk], before, 0.0), axis=-1, keepdims=True)
        e_out = jnp.where(lane_i == kk, idxs[kk].astype(I32), e_out)
        r_out = jnp.where(lane_i == kk, rank.astype(I32), r_out)
        w_out = jnp.where(lane_i == kk, ex[kk] / denom, w_out)
    e_ref[...] = e_out[:, :TOP_K]
    r_ref[...] = r_out[:, :TOP_K]
    w_ref[...] = w_out[:, :TOP_K]
    run_ref[...] = run_ref[...] + jnp.sum(oh, axis=0, keepdims=True)
    cnt_ref[...] = run_ref[...]


def _route_call(logits, T):
    tm = ROUTE_BLOCK
    tri = (np.arange(tm)[:, None] > np.arange(tm)[None, :]).astype(np.float32)
    tri = jnp.asarray(tri, dtype=BF16)
    row = lambda i: (i, 0)
    return pl.pallas_call(
        _route_body,
        grid=(T // tm,),
        in_specs=[pl.BlockSpec((tm, LANES), row), pl.BlockSpec((tm, tm), lambda i: (0, 0))],
        out_specs=[pl.BlockSpec((tm, TOP_K), row), pl.BlockSpec((tm, TOP_K), row),
                   pl.BlockSpec((tm, TOP_K), row), pl.BlockSpec((1, LANES), lambda i: (0, 0))],
        out_shape=[jax.ShapeDtypeStruct((T, TOP_K), I32), jax.ShapeDtypeStruct((T, TOP_K), I32),
                   jax.ShapeDtypeStruct((T, TOP_K), F32), jax.ShapeDtypeStruct((1, LANES), F32)],
        scratch_shapes=[pltpu.VMEM((1, LANES), F32)],
        compiler_params=_cparams(("arbitrary",)),
        name="route_topk",
    )(logits, tri)


def _dispatch_body(ps_ref, e_ref, r_ref, h_ref, z_hbm, xs_hbm, sem):
    del z_hbm

    def issue(t, c):
        for kk in range(TOP_K):
            a = t * TOP_K + kk
            dst = ps_ref[e_ref[a]] + r_ref[a]
            pltpu.make_async_copy(h_ref.at[pl.ds(t, 1)], xs_hbm.at[pl.ds(dst, 1)], sem).start()
        return c

    lax.fori_loop(0, ROW_BLOCK, issue, 0)
    for _ in range(TOP_K):
        pltpu.make_async_copy(h_ref, xs_hbm.at[pl.ds(0, ROW_BLOCK)], sem).wait()


def _dispatch_call(pstart, e_flat, r_flat, h2p, n_rows):
    T = h2p.shape[0]
    nblk = ROW_BLOCK * TOP_K
    zeros = jnp.zeros((n_rows, D_MODEL // 2), U32)
    smem_blk = pl.BlockSpec((nblk,), lambda i, ps: (i,), memory_space=pltpu.SMEM)
    return pl.pallas_call(
        _dispatch_body,
        grid_spec=pltpu.PrefetchScalarGridSpec(
            num_scalar_prefetch=1,
            grid=(T // ROW_BLOCK,),
            in_specs=[smem_blk, smem_blk,
                      pl.BlockSpec((ROW_BLOCK, D_MODEL // 2), lambda i, ps: (i, 0)),
                      pl.BlockSpec(memory_space=pl.ANY)],
            out_specs=pl.BlockSpec(memory_space=pl.ANY),
            scratch_shapes=[pltpu.SemaphoreType.DMA(())]),
        out_shape=jax.ShapeDtypeStruct((n_rows, D_MODEL // 2), U32),
        input_output_aliases={4: 0},
        compiler_params=_cparams(("arbitrary",)),
        name="moe_dispatch",
    )(pstart, e_flat, r_flat, h2p, zeros)


def _expert_body(te_ref, tv_ref, xs_ref, wgu_ref, bgu_ref, wd_ref, bd_ref, y_ref):
    i = pl.program_id(0)

    @pl.when(tv_ref[i] == 1)
    def _():
        lo, hi = _unpack_pairs(xs_ref[...])
        half = D_MODEL // 2
        gu = (jnp.dot(lo, wgu_ref[:half, :], preferred_element_type=F32)
              + jnp.dot(hi, wgu_ref[half:, :], preferred_element_type=F32) + bgu_ref[...])
        gate = jnp.minimum(gu[:, :D_EXPERT], SWIGLU_LIMIT)
        up = jnp.clip(gu[:, D_EXPERT:], -SWIGLU_LIMIT, SWIGLU_LIMIT)
        hid = (up + 1.0) * (gate * jax.nn.sigmoid(SWIGLU_ALPHA * gate))
        y_ref[...] = jnp.dot(hid.astype(BF16), wd_ref[...], preferred_element_type=F32) + bd_ref[...]

    @pl.when(tv_ref[i] == 0)
    def _():
        y_ref[...] = jnp.zeros_like(y_ref)


def _expert_call(tile_e, tile_v, xs, p):
    n_rows = xs.shape[0]
    tmx = EXPERT_TILE
    by_e = lambda i, te, tv: (te[i], 0, 0)
    return pl.pallas_call(
        _expert_body,
        grid_spec=pltpu.PrefetchScalarGridSpec(
            num_scalar_prefetch=2,
            grid=(n_rows // tmx,),
            in_specs=[pl.BlockSpec((tmx, D_MODEL // 2), lambda i, te, tv: (i, 0)),
                      pl.BlockSpec((None, D_MODEL, 2 * D_EXPERT), by_e),
                      pl.BlockSpec((None, 1, 2 * D_EXPERT), by_e),
                      pl.BlockSpec((None, D_EXPERT, D_MODEL), by_e),
                      pl.BlockSpec((None, 1, D_MODEL), by_e)],
            out_specs=pl.BlockSpec((tmx, D_MODEL), lambda i, te, tv: (i, 0))),
        out_shape=jax.ShapeDtypeStruct((n_rows, D_MODEL), F32),
        compiler_params=_cparams(("arbitrary",)),
        name="moe_experts",
    )(tile_e, tile_v, xs, p["w_gate_up"], p["b_gate_up"], p["w_down"], p["b_down"])


def _combine_body(ps_ref, e_ref, r_ref, x1_ref, w_ref, ga2_ref, ys_hbm, o_ref, buf, sem):
    def row_copy(src_row, kk, t):
        return pltpu.make_async_copy(ys_hbm.at[pl.ds(src_row, 1)], buf.at[kk, pl.ds(t, 1)], sem)

    def issue(t, c):
        for kk in range(TOP_K):
            a = t * TOP_K + kk
            row_copy(ps_ref[e_ref[a]] + r_ref[a], kk, t).start()
        return c

    lax.fori_loop(0, ROW_BLOCK, issue, 0)
    for kk in range(TOP_K):
        pltpu.make_async_copy(ys_hbm.at[pl.ds(0, ROW_BLOCK)], buf.at[kk], sem).wait()

    w = w_ref[...]
    acc = w[:, 0:1] * buf[0]
    for kk in range(1, TOP_K):
        acc = acc + w[:, kk:kk + 1] * buf[kk]
    o_ref[...] = x1_ref[...] + ga2_ref[0] * acc


def _combine_call(pstart, e_flat, r_flat, x1, wts, ga2, ys, B, S):
    T = B * S
    nblk = ROW_BLOCK * TOP_K
    nb = S // ROW_BLOCK
    smem_blk = pl.BlockSpec((nblk,), lambda i, ps: (i,), memory_space=pltpu.SMEM)
    return pl.pallas_call(
        _combine_body,
        grid_spec=pltpu.PrefetchScalarGridSpec(
            num_scalar_prefetch=1,
            grid=(T // ROW_BLOCK,),
            in_specs=[smem_blk, smem_blk,
                      pl.BlockSpec((ROW_BLOCK, D_MODEL), lambda i, ps: (i, 0)),
                      pl.BlockSpec((ROW_BLOCK, TOP_K), lambda i, ps: (i, 0)),
                      pl.BlockSpec((1, 1, D_MODEL), lambda i, ps: (i // nb, 0, 0)),
                      pl.BlockSpec(memory_space=pl.ANY)],
            out_specs=pl.BlockSpec((ROW_BLOCK, D_MODEL), lambda i, ps: (i, 0)),
            scratch_shapes=[pltpu.VMEM((TOP_K, ROW_BLOCK, D_MODEL), F32),
                            pltpu.SemaphoreType.DMA(())]),
        out_shape=jax.ShapeDtypeStruct((T, D_MODEL), F32),
        compiler_params=_cparams(("arbitrary",)),
        name="moe_combine",
    )(pstart, e_flat, r_flat, x1, wts, ga2, ys)


def _pad_heads(w, per_head, n_heads):
    k = w.shape[0]
    w = w.reshape(k, n_heads, per_head)
    w = jnp.pad(w, ((0, 0), (0, 0), (0, HEAD_PAD - per_head)))
    return w.reshape(k, n_heads * HEAD_PAD)


def _prepare(w_in, g_norm1, g_q_lat, w_uq, g_kv_lat, w_ukv, g_qk_q, g_qk_k, w_o_mla, w_o_sb,
             w_out, g_norm2, w_router, b_router, w_gate_up, b_gate_up, w_down, b_down):
    c0 = Q_LORA
    c1 = c0 + KV_LORA
    c2 = c1 + QK_ROPE
    sbw = SB_HEADS * SB_DIM
    c3 = c2 + 3 * sbw
    c4 = c3 + D_MODEL
    p = {}
    p["g_norm1"] = g_norm1.reshape(1, D_MODEL)
    p["g_norm2"] = g_norm2.reshape(1, D_MODEL)
    p["w_ql"] = w_in[:, :c0].astype(BF16)
    p["w_kvl"] = w_in[:, c0:c1].astype(BF16)
    p["w_kpe"] = jnp.pad(w_in[:, c1:c2], ((0, 0), (QK_NOPE, HEAD_PAD - QK_DIM))).astype(BF16)
    p["w_sq"] = w_in[:, c2:c2 + sbw].astype(BF16)
    p["w_sk"] = w_in[:, c2 + sbw:c2 + 2 * sbw].astype(BF16)
    p["w_sv"] = w_in[:, c2 + 2 * sbw:c3].astype(BF16)
    p["w_ga"] = w_in[:, c3:c4].astype(BF16)
    p["w_gb"] = w_in[:, c4:].astype(BF16)
    p["g_q_lat"] = g_q_lat.reshape(1, Q_LORA)
    p["g_kv_lat"] = g_kv_lat.reshape(1, KV_LORA)
    p["w_uq"] = _pad_heads(w_uq, QK_DIM, MLA_HEADS).astype(BF16)
    kv = w_ukv.reshape(KV_LORA, MLA_HEADS, QK_NOPE + V_DIM)
    p["w_uk"] = _pad_heads(kv[:, :, :QK_NOPE].reshape(KV_LORA, MLA_HEADS * QK_NOPE),
                           QK_NOPE, MLA_HEADS).astype(BF16)
    p["w_v"] = kv[:, :, QK_NOPE:].reshape(KV_LORA, MLA_HEADS * V_DIM).astype(BF16)
    p["g_qk_q"] = jnp.pad(g_qk_q, (0, HEAD_PAD - QK_DIM)).reshape(1, HEAD_PAD)
    p["g_qk_k"] = jnp.pad(g_qk_k, (0, HEAD_PAD - QK_DIM)).reshape(1, HEAD_PAD)
    inv_freq = 1.0 / (ROPE_THETA ** (jnp.arange(0, QK_ROPE, 2, dtype=F32) / QK_ROPE))
    p["invf"] = jnp.concatenate([jnp.zeros((QK_NOPE,), F32), inv_freq, inv_freq,
                                 jnp.zeros((HEAD_PAD - QK_DIM,), F32)]).reshape(1, HEAD_PAD)
    p["w_o_mla"] = w_o_mla.astype(BF16)
    p["w_o_sb"] = w_o_sb.astype(BF16)
    p["w_out"] = w_out.astype(BF16)
    wr = jnp.pad(w_router, ((0, 0), (0, LANES - N_EXPERTS)))
    p["w_r_hi"] = wr.astype(BF16)
    p["w_r_lo"] = (wr - p["w_r_hi"].astype(F32)).astype(BF16)
    p["b_r"] = jnp.concatenate([b_router, jnp.full((LANES - N_EXPERTS,), NEG_BIG, F32)]).reshape(1, LANES)
    p["w_gate_up"] = w_gate_up.astype(BF16)
    p["b_gate_up"] = b_gate_up.reshape(N_EXPERTS, 1, 2 * D_EXPERT)
    p["w_down"] = w_down.astype(BF16)
    p["b_down"] = b_down.reshape(N_EXPERTS, 1, D_MODEL)
    return p


def _layer(x2, c, posf, B, S, w_ada, b_ada, *layer_weights):
    T = B * S
    p = _prepare(*layer_weights)
    mod = _ada_call(c, w_ada, b_ada)
    sh1, sc1, ga1, sh2, sc2, ga2 = [m.reshape(B, 1, D_MODEL) for m in jnp.split(mod, 6, axis=-1)]

    q, k, v, sq, sk, sv = _proj_call(x2, sh1, sc1, posf, p, B, S)
    o_mla = _mla_call(q, k, v, B, S)
    o_sb = _sb_call(sq, sk, sv, B, S)
    x1, h2p, logits = _merge_call(x2, (sh1, sc1, ga1, sh2, sc2), o_mla, o_sb, p, B, S)

    e_idx, rank, wts, counts = _route_call(logits, T)
    cnt = counts[0, :N_EXPERTS].astype(I32)
    tiles_e = (cnt + EXPERT_TILE - 1) // EXPERT_TILE
    tile_end = jnp.cumsum(tiles_e)
    pstart = ((tile_end - tiles_e) * EXPERT_TILE).astype(I32)
    n_tiles = (T * TOP_K) // EXPERT_TILE + N_EXPERTS
    tile_ids = jnp.arange(n_tiles, dtype=I32)
    tile_e = jnp.minimum(jnp.sum((tile_ids[:, None] >= tile_end[None, :]).astype(I32), axis=1),
                         N_EXPERTS - 1).astype(I32)
    tile_v = (tile_ids < tile_end[-1]).astype(I32)

    e_flat = e_idx.reshape(T * TOP_K)
    r_flat = rank.reshape(T * TOP_K)
    xs = _dispatch_call(pstart, e_flat, r_flat, h2p, n_tiles * EXPERT_TILE)
    ys = _expert_call(tile_e, tile_v, xs, p)
    return _combine_call(pstart, e_flat, r_flat, x1, wts, ga2, ys, B, S)


def kernel(x, c, positions, w_ada, b_ada, g_norm1, w_in, g_q_lat, w_uq, g_kv_lat, w_ukv, g_qk_q,
           g_qk_k, w_o_mla, w_o_sb, w_out, g_norm2, w_router, b_router, w_gate_up, b_gate_up,
           w_down, b_down):
    B, S, D = x.shape
    x2 = x.reshape(B * S, D)
    posf = positions.astype(F32).reshape(B * S, 1)
    for l in range(w_ada.shape[0]):
        x2 = _layer(x2, c, posf, B, S, w_ada[l], b_ada[l], w_in[l], g_norm1[l], g_q_lat[l],
                    w_uq[l], g_kv_lat[l], w_ukv[l], g_qk_q[l], g_qk_k[l], w_o_mla[l], w_o_sb[l],
                    w_out[l], g_norm2[l], w_router[l], b_router[l], w_gate_up[l], b_gate_up[l],
                    w_down[l], b_down[l])
    return x2.reshape(B, S, D)
```

```python
import functools

import jax
import jax.numpy as jnp
import numpy as np
from jax import lax
from jax.experimental import pallas as pl
from jax.experimental.pallas import tpu as pltpu

F32 = jnp.float32
BF16 = jnp.bfloat16
I32 = jnp.int32
U32 = jnp.uint32

D_MODEL = 1024
EPS = 1e-6
CHUNK = 64
MLA_HEADS = 8
Q_LORA = 384
KV_LORA = 256
QK_NOPE = 64
QK_ROPE = 32
V_DIM = 64
QK_DIM = QK_NOPE + QK_ROPE
ROPE_THETA = 10000.0
SB_HEADS = 8
SB_DIM = 64
N_EXPERTS = 32
TOP_K = 4
D_EXPERT = D_MODEL
SWIGLU_LIMIT = 7.0
SWIGLU_ALPHA = 1.702

LANES = 128
HEAD_PAD = LANES
TOK_BLOCK = 512
ATT_BLOCK = 256
ROW_BLOCK = 256
EXPERT_TILE = 512
RUN_ALIGN = 8
RUN_BITS = tuple(2 ** b for b in reversed(range((ROW_BLOCK // RUN_ALIGN).bit_length())))
LOCAL_ROWS = ROW_BLOCK * TOP_K + N_EXPERTS * RUN_ALIGN
TABLE_BLOCKS = 32
NEG_BIG = -1e30
LOG2E = 1.4426950408889634
MLA_LOGIT_SCALE = QK_DIM ** -0.5 * LOG2E
SB_LOGIT_SCALE = SB_DIM ** -0.5 * LOG2E
VMEM_LIMIT = 48 * 1024 * 1024


def _cparams(sem):
    return pltpu.CompilerParams(dimension_semantics=sem, vmem_limit_bytes=VMEM_LIMIT)


def _ada_body(c_ref, w_ref, b_ref, o_ref):
    c = c_ref[...]
    s = c * jax.nn.sigmoid(c)
    o_ref[...] = jnp.dot(s, w_ref[...], preferred_element_type=F32,
                         precision=lax.Precision.HIGHEST) + b_ref[...]


def _ada_call(c, w_ada, b_ada):
    B = c.shape[0]
    n = w_ada.shape[1]
    bn = 512
    return pl.pallas_call(
        _ada_body,
        grid=(n // bn,),
        in_specs=[pl.BlockSpec((B, D_MODEL), lambda j: (0, 0)),
                  pl.BlockSpec((D_MODEL, bn), lambda j: (0, j)),
                  pl.BlockSpec((1, bn), lambda j: (0, j))],
        out_specs=pl.BlockSpec((B, bn), lambda j: (0, j)),
        out_shape=jax.ShapeDtypeStruct((B, n), F32),
        compiler_params=_cparams(("arbitrary",)),
        name="ada_mod",
    )(c, w_ada, b_ada.reshape(1, n))


def _rms(v, width):
    return lax.rsqrt(jnp.sum(v * v, axis=-1, keepdims=True) * (1.0 / width) + EPS)


def _modulated_norm(x, g, sh, sc):
    h = x * _rms(x, D_MODEL) * g
    return h * (1.0 + sc) + sh


def _proj_body(x_ref, sh_ref, sc_ref, pos_ref, gn_ref, wql_ref, wkvl_ref, wkpe_ref,
               wsq_ref, wsk_ref, wsv_ref, gql_ref, wuq_ref, gkvl_ref, wuk_ref, wv_ref,
               gq_ref, gk_ref, invf_ref,
               q_ref, k_ref, v_ref, sq_ref, sk_ref, sv_ref):
    x = x_ref[...]
    h = _modulated_norm(x, gn_ref[...], sh_ref[0], sc_ref[0]).astype(BF16)

    sq_ref[...] = jnp.dot(h, wsq_ref[...], preferred_element_type=F32).astype(BF16)
    sk_ref[...] = jnp.dot(h, wsk_ref[...], preferred_element_type=F32).astype(BF16)
    sv_ref[...] = jnp.dot(h, wsv_ref[...], preferred_element_type=F32).astype(BF16)

    q_lat = jnp.dot(h, wql_ref[...], preferred_element_type=F32)
    kv_lat = jnp.dot(h, wkvl_ref[...], preferred_element_type=F32)
    kpe = jnp.dot(h, wkpe_ref[...], preferred_element_type=F32)

    qn = (q_lat * _rms(q_lat, Q_LORA) * gql_ref[...]).astype(BF16)
    kvn = (kv_lat * _rms(kv_lat, KV_LORA) * gkvl_ref[...]).astype(BF16)
    q = jnp.dot(qn, wuq_ref[...], preferred_element_type=F32)
    kn = jnp.dot(kvn, wuk_ref[...], preferred_element_type=F32)
    v_ref[...] = jnp.dot(kvn, wv_ref[...], preferred_element_type=F32).astype(BF16)

    tm = x.shape[0]
    lane = lax.broadcasted_iota(I32, (tm, LANES), 1)
    ang = pos_ref[...] * invf_ref[...]
    cosf = jnp.cos(ang)
    sinf = jnp.sin(ang)
    half = QK_ROPE // 2
    s_lo = jnp.where((lane >= QK_NOPE) & (lane < QK_NOPE + half), -sinf, 0.0)
    s_hi = jnp.where((lane >= QK_NOPE + half) & (lane < QK_DIM), sinf, 0.0)

    def rope(t):
        return (t * cosf + pltpu.roll(t, LANES - half, 1) * s_lo
                + pltpu.roll(t, half, 1) * s_hi)

    gq = gq_ref[...]
    gk = gk_ref[...]
    kpe_sq = jnp.sum(kpe * kpe, axis=-1, keepdims=True)
    kr = rope(kpe * gk)
    for hh in range(MLA_HEADS):
        sl = slice(hh * HEAD_PAD, (hh + 1) * HEAD_PAD)
        qh = q[:, sl]
        qh = qh * _rms(qh, QK_DIM) * gq
        q_ref[:, sl] = (rope(qh) * MLA_LOGIT_SCALE).astype(BF16)
        kh = kn[:, sl]
        rk = lax.rsqrt((jnp.sum(kh * kh, axis=-1, keepdims=True) + kpe_sq) * (1.0 / QK_DIM) + EPS)
        k_ref[:, sl] = ((kh * gk + kr) * rk).astype(BF16)


def _proj_call(x2, sh1, sc1, posf, p, B, S):
    T = B * S
    tm = TOK_BLOCK
    nb = S // tm
    row = lambda i: (i, 0)
    per_b = lambda i: (i // nb, 0, 0)
    full = lambda i: (0, 0)

    def wspec(a):
        return pl.BlockSpec(a.shape, full)

    weights = [p["g_norm1"], p["w_ql"], p["w_kvl"], p["w_kpe"], p["w_sq"], p["w_sk"], p["w_sv"],
               p["g_q_lat"], p["w_uq"], p["g_kv_lat"], p["w_uk"], p["w_v"], p["g_qk_q"],
               p["g_qk_k"], p["invf"]]
    out_w = [MLA_HEADS * HEAD_PAD, MLA_HEADS * HEAD_PAD, MLA_HEADS * V_DIM,
             SB_HEADS * SB_DIM, SB_HEADS * SB_DIM, SB_HEADS * SB_DIM]
    return pl.pallas_call(
        _proj_body,
        grid=(T // tm,),
        in_specs=[pl.BlockSpec((tm, D_MODEL), row),
                  pl.BlockSpec((1, 1, D_MODEL), per_b),
                  pl.BlockSpec((1, 1, D_MODEL), per_b),
                  pl.BlockSpec((tm, 1), row)] + [wspec(a) for a in weights],
        out_specs=[pl.BlockSpec((tm, w), row) for w in out_w],
        out_shape=[jax.ShapeDtypeStruct((T, w), BF16) for w in out_w],
        compiler_params=_cparams(("arbitrary",)),
        name="mix_proj",
    )(x2, sh1, sc1, posf, *weights)


_NT = (((1,), (1,)), ((), ()))


def _mla_rows(qh, k_ref, v_ref, sl, i, tq):
    kend = (i + 1) * tq
    s = lax.dot_general(qh, k_ref[0:kend, sl], _NT, preferred_element_type=F32)
    row = lax.broadcasted_iota(I32, (tq, tq), 0)
    col = lax.broadcasted_iota(I32, (tq, tq), 1)
    s_diag = jnp.where((col // CHUNK) <= (row // CHUNK), s[:, kend - tq:], NEG_BIG)
    s = s_diag if i == 0 else jnp.concatenate([s[:, :kend - tq], s_diag], axis=1)
    pr = jnp.exp2(s - jnp.max(s, axis=-1, keepdims=True))
    l = jnp.sum(pr, axis=-1, keepdims=True)
    return jnp.dot(pr.astype(BF16), v_ref[0:kend, :], preferred_element_type=F32) / l


def _mla_body(q_ref, k_ref, v_ref, o_ref):
    tq = q_ref.shape[0]
    i = pl.program_id(2)
    lane = lax.broadcasted_iota(I32, (tq, LANES), 1)
    for ii in range(k_ref.shape[0] // tq):
        @pl.when(i == ii)
        def _():
            outs = []
            for hh in range(2):
                sl = slice(hh * HEAD_PAD, (hh + 1) * HEAD_PAD)
                outs.append(_mla_rows(q_ref[:, sl], k_ref, v_ref, sl, ii, tq))
            o_ref[...] = jnp.where(lane < V_DIM, outs[0], outs[1]).astype(BF16)


def _mla_call(q, k, v, B, S):
    tq = ATT_BLOCK
    q3 = q.reshape(B, S, MLA_HEADS * HEAD_PAD)
    k3 = k.reshape(B, S, MLA_HEADS * HEAD_PAD)
    v3 = v.reshape(B, S, MLA_HEADS * V_DIM)
    out = pl.pallas_call(
        _mla_body,
        grid=(B, MLA_HEADS // 2, S // tq),
        in_specs=[pl.BlockSpec((None, tq, 2 * HEAD_PAD), lambda b, hp, i: (b, i, hp)),
                  pl.BlockSpec((None, S, 2 * HEAD_PAD), lambda b, hp, i: (b, 0, hp)),
                  pl.BlockSpec((None, S, 2 * V_DIM), lambda b, hp, i: (b, 0, hp))],
        out_specs=pl.BlockSpec((None, tq, 2 * V_DIM), lambda b, hp, i: (b, i, hp)),
        out_shape=jax.ShapeDtypeStruct((B, S, MLA_HEADS * V_DIM), BF16),
        compiler_params=_cparams(("arbitrary", "arbitrary", "arbitrary")),
        name="mla_attn",
    )(q3, k3, v3)
    return out.reshape(B * S, MLA_HEADS * V_DIM)


def _sb_rows(qh, k_ref, v_ref, i, tq):
    kend = (i + 1) * tq
    z = lax.dot_general(qh, k_ref[0:kend, :], _NT, preferred_element_type=F32) * SB_LOGIT_SCALE
    sp = jnp.maximum(z, 0.0) + jnp.log2(1.0 + jnp.exp2(jnp.minimum(z, -z)))
    row = lax.broadcasted_iota(I32, (tq, tq), 0)
    col = lax.broadcasted_iota(I32, (tq, tq), 1)
    strict = col < row
    suffix_ones = jnp.where(row >= col, 1.0, 0.0).astype(BF16)
    later = jnp.zeros((tq, 1), F32)
    a_blocks = [None] * (i + 1)
    for j in range(i, -1, -1):
        cols = slice(j * tq, (j + 1) * tq)
        spj = sp[:, cols]
        if j == i:
            spj = jnp.where(strict, spj, 0.0)
        suf = jnp.dot(spj.astype(BF16), suffix_ones, preferred_element_type=F32)
        a = jnp.exp2(z[:, cols] - suf - later)
        if j == i:
            a = jnp.where(strict, a, 0.0)
        a_blocks[j] = a.astype(BF16)
        later = later + suf[:, 0:1]
    a = a_blocks[0] if i == 0 else jnp.concatenate(a_blocks, axis=1)
    return jnp.dot(a, v_ref[0:kend, :], preferred_element_type=F32)


def _sb_body(q_ref, k_ref, v_ref, o_ref):
    tq = q_ref.shape[0]
    i = pl.program_id(2)
    lane = lax.broadcasted_iota(I32, (tq, LANES), 1)
    for ii in range(k_ref.shape[0] // tq):
        @pl.when(i == ii)
        def _():
            q = q_ref[...]
            outs = []
            for hh in range(2):
                in_head = (lane >= hh * SB_DIM) & (lane < (hh + 1) * SB_DIM)
                qh = jnp.where(in_head, q, jnp.zeros_like(q))
                outs.append(_sb_rows(qh, k_ref, v_ref, ii, tq))
            o_ref[...] = jnp.where(lane < SB_DIM, outs[0], outs[1]).astype(BF16)


def _sb_call(sq, sk, sv, B, S):
    tq = ATT_BLOCK
    w = SB_HEADS * SB_DIM
    args = [a.reshape(B, S, w) for a in (sq, sk, sv)]
    out = pl.pallas_call(
        _sb_body,
        grid=(B, SB_HEADS // 2, S // tq),
        in_specs=[pl.BlockSpec((None, tq, 2 * SB_DIM), lambda b, hp, i: (b, i, hp)),
                  pl.BlockSpec((None, S, 2 * SB_DIM), lambda b, hp, i: (b, 0, hp)),
                  pl.BlockSpec((None, S, 2 * SB_DIM), lambda b, hp, i: (b, 0, hp))],
        out_specs=pl.BlockSpec((None, tq, 2 * SB_DIM), lambda b, hp, i: (b, i, hp)),
        out_shape=jax.ShapeDtypeStruct((B, S, w), BF16),
        compiler_params=_cparams(("arbitrary", "arbitrary", "arbitrary")),
        name="sb_attn",
    )(*args)
    return out.reshape(B * S, w)


def _pack_pairs(h):
    n = h.shape[1] // 2
    return _pack_halves(h[:, :n], h[:, n:])


def _pack_halves(lo, hi):
    lo = pltpu.bitcast(lo.astype(BF16).astype(F32), U32)
    hi = pltpu.bitcast(hi.astype(BF16).astype(F32), U32)
    return (lo >> 16) | (hi & jnp.uint32(0xFFFF0000))


def _unpack_pairs(w):
    lo = pltpu.bitcast(w << 16, F32).astype(BF16)
    hi = pltpu.bitcast(w & jnp.uint32(0xFFFF0000), F32).astype(BF16)
    return lo, hi


def _merge_body(x_ref, sh1_ref, sc1_ref, ga1_ref, sh2_ref, sc2_ref, om_ref, os_ref,
                gn1_ref, wga_ref, wgb_ref, wom_ref, wos_ref, wout_ref, gn2_ref,
                wrh_ref, wrl_ref, br_ref,
                x1_ref, h2_ref, lg_ref):
    x = x_ref[...]
    h = _modulated_norm(x, gn1_ref[...], sh1_ref[0], sc1_ref[0]).astype(BF16)
    ga = jax.nn.sigmoid(jnp.dot(h, wga_ref[...], preferred_element_type=F32))
    gb = jax.nn.sigmoid(jnp.dot(h, wgb_ref[...], preferred_element_type=F32))
    merged = (ga * jnp.dot(om_ref[...], wom_ref[...], preferred_element_type=F32)
              + gb * jnp.dot(os_ref[...], wos_ref[...], preferred_element_type=F32))
    y = jnp.dot(merged.astype(BF16), wout_ref[...], preferred_element_type=F32)
    x1 = x + ga1_ref[0] * y
    x1_ref[...] = x1
    h2 = _modulated_norm(x1, gn2_ref[...], sh2_ref[0], sc2_ref[0])
    h2_ref[...] = _pack_pairs(h2)
    h2_hi = h2.astype(BF16)
    h2_lo = (h2 - h2_hi.astype(F32)).astype(BF16)
    lg_ref[...] = (jnp.dot(h2_hi, wrh_ref[...], preferred_element_type=F32)
                   + jnp.dot(h2_hi, wrl_ref[...], preferred_element_type=F32)
                   + jnp.dot(h2_lo, wrh_ref[...], preferred_element_type=F32)
                   + br_ref[...])


def _merge_call(x2, mods, o_mla, o_sb, p, B, S):
    T = B * S
    tm = TOK_BLOCK
    nb = S // tm
    row = lambda i: (i, 0)
    per_b = lambda i: (i // nb, 0, 0)
    full = lambda i: (0, 0)
    weights = [p["g_norm1"], p["w_ga"], p["w_gb"], p["w_o_mla"], p["w_o_sb"], p["w_out"],
               p["g_norm2"], p["w_r_hi"], p["w_r_lo"], p["b_r"]]
    return pl.pallas_call(
        _merge_body,
        grid=(T // tm,),
        in_specs=[pl.BlockSpec((tm, D_MODEL), row)]
        + [pl.BlockSpec((1, 1, D_MODEL), per_b)] * 5
        + [pl.BlockSpec((tm, MLA_HEADS * V_DIM), row), pl.BlockSpec((tm, SB_HEADS * SB_DIM), row)]
        + [pl.BlockSpec(a.shape, full) for a in weights],
        out_specs=[pl.BlockSpec((tm, D_MODEL), row), pl.BlockSpec((tm, D_MODEL // 2), row),
                   pl.BlockSpec((tm, LANES), row)],
        out_shape=[jax.ShapeDtypeStruct((T, D_MODEL), F32),
                   jax.ShapeDtypeStruct((T, D_MODEL // 2), U32),
                   jax.ShapeDtypeStruct((T, LANES), F32)],
        compiler_params=_cparams(("arbitrary",)),
        name="merge_norm",
    )(x2, *mods, o_mla, o_sb, *weights)


def _route_body(lg_ref, tri_ref, su_ref, lpos_ref, w_ref, tab_ref, tot_ref, run_ref):
    @pl.when(pl.program_id(0) == 0)
    def _():
        run_ref[...] = jnp.zeros_like(run_ref)

    v = lg_ref[...]
    tm = v.shape[0]
    lane_i = lax.broadcasted_iota(I32, (tm, LANES), 1)
    lane = lane_i.astype(F32)
    vals, idxs = [], []
    for _ in range(TOP_K):
        m = jnp.max(v, axis=-1, keepdims=True)
        idx = jnp.min(jnp.where(v == m, lane, float(LANES)), axis=-1, keepdims=True)
        vals.append(m)
        idxs.append(idx)
        v = jnp.where(lane == idx, NEG_BIG, v)
    ex = [jnp.exp(t - vals[0]) for t in vals]
    denom = ex[0] + ex[1] + ex[2] + ex[3]
    onehots = [lane == idx for idx in idxs]
    oh = jnp.zeros((tm, LANES), F32)
    for o in onehots:
        oh = oh + jnp.where(o, 1.0, 0.0)
    units = jnp.floor((jnp.sum(oh, axis=0, keepdims=True) + (RUN_ALIGN - 1.0)) * (1.0 / RUN_ALIGN))
    lstart = RUN_ALIGN * jnp.dot(jnp.broadcast_to(units, (8, LANES)).astype(BF16), su_ref[...],
                                 preferred_element_type=F32)[0:1]
    pos = jnp.dot(tri_ref[...], oh.astype(BF16), preferred_element_type=F32) + lstart
    lp_out = jnp.zeros((tm, LANES), I32)
    w_out = jnp.zeros((tm, LANES), F32)
    for kk in range(TOP_K):
        lp = jnp.sum(jnp.where(onehots[kk], pos, 0.0), axis=-1, keepdims=True)
        lp_out = jnp.where(lane_i == kk, lp.astype(I32), lp_out)
        w_out = jnp.where(lane_i == kk, ex[kk] / denom, w_out)
    lpos_ref[...] = lp_out[:, :TOP_K]
    w_ref[...] = w_out[:, :TOP_K]
    sub = lax.broadcasted_iota(I32, (8, LANES), 0)
    tab = jnp.where(sub == 0, lstart, jnp.where(sub == 1, units, jnp.where(sub == 2, run_ref[...], 0.0)))
    tab_ref[0] = tab.astype(I32)
    run_ref[...] = run_ref[...] + RUN_ALIGN * units
    tot_ref[...] = run_ref[...]


def _route_call(logits, T):
    tm = ROW_BLOCK
    nblk = T // tm
    tri = jnp.asarray(np.arange(tm)[:, None] > np.arange(tm)[None, :], dtype=BF16)
    su = jnp.asarray(np.arange(LANES)[:, None] < np.arange(LANES)[None, :], dtype=BF16)
    row = lambda i: (i, 0)
    const = lambda i: (0, 0)
    return pl.pallas_call(
        _route_body,
        grid=(nblk,),
        in_specs=[pl.BlockSpec((tm, LANES), row), pl.BlockSpec((tm, tm), const),
                  pl.BlockSpec((LANES, LANES), const)],
        out_specs=[pl.BlockSpec((tm, TOP_K), row), pl.BlockSpec((tm, TOP_K), row),
                   pl.BlockSpec((1, 8, LANES), lambda i: (i, 0, 0)), pl.BlockSpec((1, LANES), const)],
        out_shape=[jax.ShapeDtypeStruct((T, TOP_K), I32), jax.ShapeDtypeStruct((T, TOP_K), F32),
                   jax.ShapeDtypeStruct((nblk, 8, LANES), I32), jax.ShapeDtypeStruct((1, LANES), F32)],
        scratch_shapes=[pltpu.VMEM((1, LANES), F32)],
        compiler_params=_cparams(("arbitrary",)),
        name="route_topk",
    )(logits, tri, su)


def _run_copies(fn, tabs, blk, make_copy):
    ls_ref, ds_ref, nu_ref = tabs
    base = (blk % TABLE_BLOCKS) * N_EXPERTS

    def per_expert(e, c):
        loc = ls_ref[base + e]
        dst = ds_ref[base + e]
        units = nu_ref[base + e]
        for level, bit in enumerate(RUN_BITS):
            rows = bit * RUN_ALIGN
            take = (units & bit) != 0

            @pl.when(take)
            def _():
                fn(make_copy(pl.multiple_of(loc, RUN_ALIGN), pl.multiple_of(dst, RUN_ALIGN), rows, level))

            step = jnp.where(take, rows, 0)
            loc = loc + step
            dst = dst + step
        return c

    lax.fori_loop(0, N_EXPERTS, per_expert, 0)


_TN = (((0,), (0,)), ((), ()))


def _slot_matrix(lpos, values):
    tm = lpos.shape[0]
    col = lax.broadcasted_iota(I32, (tm, LOCAL_ROWS), 1)
    out = jnp.zeros((tm, LOCAL_ROWS), F32)
    for kk in range(TOP_K):
        out = jnp.where(col == lpos[:, kk:kk + 1], values[kk], out)
    return out


def _dispatch_body(ls_ref, ds_ref, nu_ref, h_ref, lpos_ref, z_hbm, xs_hbm, loc_ref, sems):
    del z_hbm
    lo, hi = _unpack_pairs(h_ref[...])
    sel = _slot_matrix(lpos_ref[...], [1.0] * TOP_K).astype(BF16)
    x_lo = lax.dot_general(sel, lo, _TN, preferred_element_type=F32)
    x_hi = lax.dot_general(sel, hi, _TN, preferred_element_type=F32)
    loc_ref[...] = _pack_halves(x_lo, x_hi)

    def make_copy(loc, dst, rows, level):
        return pltpu.make_async_copy(loc_ref.at[pl.ds(loc, rows)], xs_hbm.at[pl.ds(dst, rows)],
                                     sems.at[level])

    tabs = (ls_ref, ds_ref, nu_ref)
    blk = pl.program_id(0)
    _run_copies(lambda cp: cp.start(), tabs, blk, make_copy)
    _run_copies(lambda cp: cp.wait(), tabs, blk, make_copy)


def _table_specs():
    blk = pl.BlockSpec((TABLE_BLOCKS * N_EXPERTS,), lambda i: (i // TABLE_BLOCKS,),
                       memory_space=pltpu.SMEM)
    return [blk, blk, blk]


def _dispatch_call(tabs, h2p, lpos, n_rows):
    T = h2p.shape[0]
    zeros = jnp.zeros((n_rows, D_MODEL // 2), U32)
    return pl.pallas_call(
        _dispatch_body,
        grid=(T // ROW_BLOCK,),
        in_specs=_table_specs() + [
            pl.BlockSpec((ROW_BLOCK, D_MODEL // 2), lambda i: (i, 0)),
            pl.BlockSpec((ROW_BLOCK, TOP_K), lambda i: (i, 0)),
            pl.BlockSpec(memory_space=pl.ANY)],
        out_specs=pl.BlockSpec(memory_space=pl.ANY),
        scratch_shapes=[pltpu.VMEM((LOCAL_ROWS, D_MODEL // 2), U32),
                        pltpu.SemaphoreType.DMA((len(RUN_BITS),))],
        out_shape=jax.ShapeDtypeStruct((n_rows, D_MODEL // 2), U32),
        input_output_aliases={5: 0},
        compiler_params=_cparams(("arbitrary",)),
        name="moe_dispatch",
    )(*tabs, h2p, lpos, zeros)


def _expert_body(te_ref, tv_ref, xs_ref, wgu_ref, bgu_ref, wd_ref, bd_ref, y_ref):
    i = pl.program_id(0)

    @pl.when(tv_ref[i] == 1)
    def _():
        lo, hi = _unpack_pairs(xs_ref[...])
        half = D_MODEL // 2
        gu = (jnp.dot(lo, wgu_ref[:half, :], preferred_element_type=F32)
              + jnp.dot(hi, wgu_ref[half:, :], preferred_element_type=F32) + bgu_ref[...])
        gate = jnp.minimum(gu[:, :D_EXPERT], SWIGLU_LIMIT)
        up = jnp.clip(gu[:, D_EXPERT:], -SWIGLU_LIMIT, SWIGLU_LIMIT)
        hid = (up + 1.0) * (gate * jax.nn.sigmoid(SWIGLU_ALPHA * gate))
        y = jnp.dot(hid.astype(BF16), wd_ref[...], preferred_element_type=F32) + bd_ref[...]
        y_ref[...] = _pack_pairs(y)

    @pl.when(tv_ref[i] == 0)
    def _():
        y_ref[...] = jnp.zeros_like(y_ref)


def _expert_call(tile_e, tile_v, xs, p):
    n_rows = xs.shape[0]
    tmx = EXPERT_TILE
    by_e = lambda i, te, tv: (te[i], 0, 0)
    return pl.pallas_call(
        _expert_body,
        grid_spec=pltpu.PrefetchScalarGridSpec(
            num_scalar_prefetch=2,
            grid=(n_rows // tmx,),
            in_specs=[pl.BlockSpec((tmx, D_MODEL // 2), lambda i, te, tv: (i, 0)),
                      pl.BlockSpec((None, D_MODEL, 2 * D_EXPERT), by_e),
                      pl.BlockSpec((None, 1, 2 * D_EXPERT), by_e),
                      pl.BlockSpec((None, D_EXPERT, D_MODEL), by_e),
                      pl.BlockSpec((None, 1, D_MODEL), by_e)],
            out_specs=pl.BlockSpec((tmx, D_MODEL // 2), lambda i, te, tv: (i, 0))),
        out_shape=jax.ShapeDtypeStruct((n_rows, D_MODEL // 2), U32),
        compiler_params=_cparams(("arbitrary",)),
        name="moe_experts",
    )(tile_e, tile_v, xs, p["w_gate_up"], p["b_gate_up"], p["w_down"], p["b_down"])


def _combine_body(ls_ref, ds_ref, nu_ref, x1_ref, lpos_ref, w_ref, ga2_ref, ys_hbm, o_ref,
                  loc_ref, sems):
    loc_ref[...] = jnp.zeros_like(loc_ref)

    def make_copy(loc, src, rows, level):
        return pltpu.make_async_copy(ys_hbm.at[pl.ds(src, rows)], loc_ref.at[pl.ds(loc, rows)],
                                     sems.at[level])

    tabs = (ls_ref, ds_ref, nu_ref)
    blk = pl.program_id(0)
    _run_copies(lambda cp: cp.start(), tabs, blk, make_copy)
    _run_copies(lambda cp: cp.wait(), tabs, blk, make_copy)

    y_lo, y_hi = _unpack_pairs(loc_ref[...])
    w = w_ref[...]
    mix = _slot_matrix(lpos_ref[...], [w[:, kk:kk + 1] for kk in range(TOP_K)])
    mix_hi = mix.astype(BF16)
    mix_lo = (mix - mix_hi.astype(F32)).astype(BF16)
    halves = [jnp.dot(mix_hi, y, preferred_element_type=F32) + jnp.dot(mix_lo, y, preferred_element_type=F32)
              for y in (y_lo, y_hi)]
    o_ref[...] = x1_ref[...] + ga2_ref[0] * jnp.concatenate(halves, axis=1)


def _combine_call(tabs, x1, lpos, wts, ga2, ys, B, S):
    T = B * S
    nb = S // ROW_BLOCK
    row = lambda i: (i, 0)
    return pl.pallas_call(
        _combine_body,
        grid=(T // ROW_BLOCK,),
        in_specs=_table_specs() + [
            pl.BlockSpec((ROW_BLOCK, D_MODEL), row),
            pl.BlockSpec((ROW_BLOCK, TOP_K), row),
            pl.BlockSpec((ROW_BLOCK, TOP_K), row),
            pl.BlockSpec((1, 1, D_MODEL), lambda i: (i // nb, 0, 0)),
            pl.BlockSpec(memory_space=pl.ANY)],
        out_specs=pl.BlockSpec((ROW_BLOCK, D_MODEL), row),
        scratch_shapes=[pltpu.VMEM((LOCAL_ROWS, D_MODEL // 2), U32),
                        pltpu.SemaphoreType.DMA((len(RUN_BITS),))],
        out_shape=jax.ShapeDtypeStruct((T, D_MODEL), F32),
        compiler_params=_cparams(("arbitrary",)),
        name="moe_combine",
    )(*tabs, x1, lpos, wts, ga2, ys)


def _pad_heads(w, per_head, n_heads):
    k = w.shape[0]
    w = w.reshape(k, n_heads, per_head)
    w = jnp.pad(w, ((0, 0), (0, 0), (0, HEAD_PAD - per_head)))
    return w.reshape(k, n_heads * HEAD_PAD)


def _prepare(w_in, g_norm1, g_q_lat, w_uq, g_kv_lat, w_ukv, g_qk_q, g_qk_k, w_o_mla, w_o_sb,
             w_out, g_norm2, w_router, b_router, w_gate_up, b_gate_up, w_down, b_down):
    c0 = Q_LORA
    c1 = c0 + KV_LORA
    c2 = c1 + QK_ROPE
    sbw = SB_HEADS * SB_DIM
    c3 = c2 + 3 * sbw
    c4 = c3 + D_MODEL
    p = {}
    p["g_norm1"] = g_norm1.reshape(1, D_MODEL)
    p["g_norm2"] = g_norm2.reshape(1, D_MODEL)
    p["w_ql"] = w_in[:, :c0].astype(BF16)
    p["w_kvl"] = w_in[:, c0:c1].astype(BF16)
    p["w_kpe"] = jnp.pad(w_in[:, c1:c2], ((0, 0), (QK_NOPE, HEAD_PAD - QK_DIM))).astype(BF16)
    p["w_sq"] = w_in[:, c2:c2 + sbw].astype(BF16)
    p["w_sk"] = w_in[:, c2 + sbw:c2 + 2 * sbw].astype(BF16)
    p["w_sv"] = w_in[:, c2 + 2 * sbw:c3].astype(BF16)
    p["w_ga"] = w_in[:, c3:c4].astype(BF16)
    p["w_gb"] = w_in[:, c4:].astype(BF16)
    p["g_q_lat"] = g_q_lat.reshape(1, Q_LORA)
    p["g_kv_lat"] = g_kv_lat.reshape(1, KV_LORA)
    p["w_uq"] = _pad_heads(w_uq, QK_DIM, MLA_HEADS).astype(BF16)
    kv = w_ukv.reshape(KV_LORA, MLA_HEADS, QK_NOPE + V_DIM)
    p["w_uk"] = _pad_heads(kv[:, :, :QK_NOPE].reshape(KV_LORA, MLA_HEADS * QK_NOPE),
                           QK_NOPE, MLA_HEADS).astype(BF16)
    p["w_v"] = kv[:, :, QK_NOPE:].reshape(KV_LORA, MLA_HEADS * V_DIM).astype(BF16)
    p["g_qk_q"] = jnp.pad(g_qk_q, (0, HEAD_PAD - QK_DIM)).reshape(1, HEAD_PAD)
    p["g_qk_k"] = jnp.pad(g_qk_k, (0, HEAD_PAD - QK_DIM)).reshape(1, HEAD_PAD)
    inv_freq = 1.0 / (ROPE_THETA ** (jnp.arange(0, QK_ROPE, 2, dtype=F32) / QK_ROPE))
    p["invf"] = jnp.concatenate([jnp.zeros((QK_NOPE,), F32), inv_freq, inv_freq,
                                 jnp.zeros((HEAD_PAD - QK_DIM,), F32)]).reshape(1, HEAD_PAD)
    p["w_o_mla"] = w_o_mla.astype(BF16)
    p["w_o_sb"] = w_o_sb.astype(BF16)
    p["w_out"] = w_out.astype(BF16)
    wr = jnp.pad(w_router, ((0, 0), (0, LANES - N_EXPERTS)))
    p["w_r_hi"] = wr.astype(BF16)
    p["w_r_lo"] = (wr - p["w_r_hi"].astype(F32)).astype(BF16)
    p["b_r"] = jnp.concatenate([b_router, jnp.full((LANES - N_EXPERTS,), NEG_BIG, F32)]).reshape(1, LANES)
    p["w_gate_up"] = w_gate_up.astype(BF16)
    p["b_gate_up"] = b_gate_up.reshape(N_EXPERTS, 1, 2 * D_EXPERT)
    p["w_down"] = w_down.astype(BF16)
    p["b_down"] = b_down.reshape(N_EXPERTS, 1, D_MODEL)
    return p


def _layer(x2, c, posf, B, S, w_ada, b_ada, *layer_weights):
    T = B * S
    p = _prepare(*layer_weights)
    mod = _ada_call(c, w_ada, b_ada)
    sh1, sc1, ga1, sh2, sc2, ga2 = [m.reshape(B, 1, D_MODEL) for m in jnp.split(mod, 6, axis=-1)]

    q, k, v, sq, sk, sv = _proj_call(x2, sh1, sc1, posf, p, B, S)
    o_mla = _mla_call(q, k, v, B, S)
    o_sb = _sb_call(sq, sk, sv, B, S)
    x1, h2p, logits = _merge_call(x2, (sh1, sc1, ga1, sh2, sc2), o_mla, o_sb, p, B, S)

    lpos, wts, tab, totals = _route_call(logits, T)
    nblk = T // ROW_BLOCK
    rows_e = totals[0, :N_EXPERTS].astype(I32)
    tiles_e = (rows_e + EXPERT_TILE - 1) // EXPERT_TILE
    tile_end = jnp.cumsum(tiles_e)
    region = ((tile_end - tiles_e) * EXPERT_TILE).astype(I32)
    n_tiles = (T * TOP_K + nblk * N_EXPERTS * (RUN_ALIGN - 1)) // EXPERT_TILE + N_EXPERTS
    tile_ids = jnp.arange(n_tiles, dtype=I32)
    tile_e = jnp.minimum(jnp.sum((tile_ids[:, None] >= tile_end[None, :]).astype(I32), axis=1),
                         N_EXPERTS - 1).astype(I32)
    tile_v = (tile_ids < tile_end[-1]).astype(I32)
    pad = (-nblk) % TABLE_BLOCKS

    def flat(t):
        return jnp.pad(t, ((0, pad), (0, 0))).reshape(-1)

    tabs = (flat(tab[:, 0, :N_EXPERTS]), flat(tab[:, 2, :N_EXPERTS] + region[None, :]),
            flat(tab[:, 1, :N_EXPERTS]))

    xs = _dispatch_call(tabs, h2p, lpos, n_tiles * EXPERT_TILE)
    ys = _expert_call(tile_e, tile_v, xs, p)
    return _combine_call(tabs, x1, lpos, wts, ga2, ys, B, S)


def kernel(x, c, positions, w_ada, b_ada, g_norm1, w_in, g_q_lat, w_uq, g_kv_lat, w_ukv, g_qk_q,
           g_qk_k, w_o_mla, w_o_sb, w_out, g_norm2, w_router, b_router, w_gate_up, b_gate_up,
           w_down, b_down):
    B, S, D = x.shape
    x2 = x.reshape(B * S, D)
    posf = positions.astype(F32).reshape(B * S, 1)
    for l in range(w_ada.shape[0]):
        x2 = _layer(x2, c, posf, B, S, w_ada[l], b_ada[l], w_in[l], g_norm1[l], g_q_lat[l],
                    w_uq[l], g_kv_lat[l], w_ukv[l], g_qk_q[l], g_qk_k[l], w_o_mla[l], w_o_sb[l],
                    w_out[l], g_norm2[l], w_router[l], b_router[l], w_gate_up[l], b_gate_up[l],
                    w_down[l], b_down[l])
    return x2.reshape(B, S, D)
```

```python
import functools

import jax
import jax.numpy as jnp
import numpy as np
from jax import lax
from jax.experimental import pallas as pl
from jax.experimental.pallas import tpu as pltpu

F32 = jnp.float32
BF16 = jnp.bfloat16
I32 = jnp.int32
U32 = jnp.uint32

D_MODEL = 1024
EPS = 1e-6
CHUNK = 64
MLA_HEADS = 8
Q_LORA = 384
KV_LORA = 256
QK_NOPE = 64
QK_ROPE = 32
V_DIM = 64
QK_DIM = QK_NOPE + QK_ROPE
ROPE_THETA = 10000.0
SB_HEADS = 8
SB_DIM = 64
N_EXPERTS = 32
TOP_K = 4
D_EXPERT = D_MODEL
SWIGLU_LIMIT = 7.0
SWIGLU_ALPHA = 1.702

LANES = 128
HEAD_PAD = LANES
TOK_BLOCK = 1024
SUB_ROWS = 256
ATT_BLOCK = 256
ROW_BLOCK = 256
EXPERT_TILE = 512
RUN_ALIGN = 8
RUN_BITS = tuple(2 ** b for b in reversed(range((ROW_BLOCK // RUN_ALIGN).bit_length())))
LOCAL_ROWS = ROW_BLOCK * TOP_K + N_EXPERTS * RUN_ALIGN
TABLE_BLOCKS = 32
NEG_BIG = -1e30
LOG2E = 1.4426950408889634
MLA_LOGIT_SCALE = QK_DIM ** -0.5 * LOG2E
SB_LOGIT_SCALE = SB_DIM ** -0.5 * LOG2E
VMEM_LIMIT = 48 * 1024 * 1024


def _cparams(sem):
    return pltpu.CompilerParams(dimension_semantics=sem, vmem_limit_bytes=VMEM_LIMIT)


def _ada_body(c_ref, w_ref, b_ref, o_ref):
    c = c_ref[...]
    s = c * jax.nn.sigmoid(c)
    o_ref[...] = jnp.dot(s, w_ref[...], preferred_element_type=F32,
                         precision=lax.Precision.HIGHEST) + b_ref[...]


def _ada_call(c, w_ada, b_ada):
    B = c.shape[0]
    n = w_ada.shape[1]
    bn = 512
    return pl.pallas_call(
        _ada_body,
        grid=(n // bn,),
        in_specs=[pl.BlockSpec((B, D_MODEL), lambda j: (0, 0)),
                  pl.BlockSpec((D_MODEL, bn), lambda j: (0, j)),
                  pl.BlockSpec((1, bn), lambda j: (0, j))],
        out_specs=pl.BlockSpec((B, bn), lambda j: (0, j)),
        out_shape=jax.ShapeDtypeStruct((B, n), F32),
        compiler_params=_cparams(("arbitrary",)),
        name="ada_mod",
    )(c, w_ada, b_ada.reshape(1, n))


def _rms(v, width):
    return lax.rsqrt(jnp.sum(v * v, axis=-1, keepdims=True) * (1.0 / width) + EPS)


def _modulated_norm(x, g, sh, sc):
    h = x * _rms(x, D_MODEL) * g
    return h * (1.0 + sc) + sh


def _proj_body(x_ref, sh_ref, sc_ref, pos_ref, gn_ref, wql_ref, wkvl_ref, wkpe_ref,
               wsq_ref, wsk_ref, wsv_ref, gql_ref, wuq_ref, gkvl_ref, wuk_ref, wv_ref,
               gq_ref, gk_ref, invf_ref,
               q_ref, k_ref, v_ref, sq_ref, sk_ref, sv_ref):
    for r in range(x_ref.shape[0] // SUB_ROWS):
        rows = slice(r * SUB_ROWS, (r + 1) * SUB_ROWS)
        x = x_ref[rows, :]
        h = _modulated_norm(x, gn_ref[...], sh_ref[0], sc_ref[0]).astype(BF16)

        sq_ref[rows, :] = jnp.dot(h, wsq_ref[...], preferred_element_type=F32).astype(BF16)
        sk_ref[rows, :] = jnp.dot(h, wsk_ref[...], preferred_element_type=F32).astype(BF16)
        sv_ref[rows, :] = jnp.dot(h, wsv_ref[...], preferred_element_type=F32).astype(BF16)

        q_lat = jnp.dot(h, wql_ref[...], preferred_element_type=F32)
        kv_lat = jnp.dot(h, wkvl_ref[...], preferred_element_type=F32)
        kpe = jnp.dot(h, wkpe_ref[...], preferred_element_type=F32)

        qn = (q_lat * _rms(q_lat, Q_LORA) * gql_ref[...]).astype(BF16)
        kvn = (kv_lat * _rms(kv_lat, KV_LORA) * gkvl_ref[...]).astype(BF16)
        q = jnp.dot(qn, wuq_ref[...], preferred_element_type=F32)
        kn = jnp.dot(kvn, wuk_ref[...], preferred_element_type=F32)
        v_ref[rows, :] = jnp.dot(kvn, wv_ref[...], preferred_element_type=F32).astype(BF16)

        lane = lax.broadcasted_iota(I32, (SUB_ROWS, LANES), 1)
        ang = pos_ref[rows, :] * invf_ref[...]
        cosf = jnp.cos(ang)
        sinf = jnp.sin(ang)
        half = QK_ROPE // 2
        s_lo = jnp.where((lane >= QK_NOPE) & (lane < QK_NOPE + half), -sinf, 0.0)
        s_hi = jnp.where((lane >= QK_NOPE + half) & (lane < QK_DIM), sinf, 0.0)

        def rope(t):
            return (t * cosf + pltpu.roll(t, LANES - half, 1) * s_lo
                    + pltpu.roll(t, half, 1) * s_hi)

        gq = gq_ref[...] * MLA_LOGIT_SCALE
        gk = gk_ref[...]
        kpe_sq = jnp.sum(kpe * kpe, axis=-1, keepdims=True)
        kr = rope(kpe * gk)
        for hh in range(MLA_HEADS):
            sl = slice(hh * HEAD_PAD, (hh + 1) * HEAD_PAD)
            qh = q[:, sl]
            q_ref[rows, sl] = rope(qh * _rms(qh, QK_DIM) * gq).astype(BF16)
            kh = kn[:, sl]
            rk = lax.rsqrt((jnp.sum(kh * kh, axis=-1, keepdims=True) + kpe_sq) * (1.0 / QK_DIM) + EPS)
            k_ref[rows, sl] = ((kh * gk + kr) * rk).astype(BF16)


def _proj_call(x2, sh1, sc1, posf, p, B, S):
    T = B * S
    tm = TOK_BLOCK
    nb = S // tm
    row = lambda i: (i, 0)
    per_b = lambda i: (i // nb, 0, 0)
    full = lambda i: (0, 0)

    def wspec(a):
        return pl.BlockSpec(a.shape, full)

    weights = [p["g_norm1"], p["w_ql"], p["w_kvl"], p["w_kpe"], p["w_sq"], p["w_sk"], p["w_sv"],
               p["g_q_lat"], p["w_uq"], p["g_kv_lat"], p["w_uk"], p["w_v"], p["g_qk_q"],
               p["g_qk_k"], p["invf"]]
    out_w = [MLA_HEADS * HEAD_PAD, MLA_HEADS * HEAD_PAD, MLA_HEADS * V_DIM,
             SB_HEADS * SB_DIM, SB_HEADS * SB_DIM, SB_HEADS * SB_DIM]
    return pl.pallas_call(
        _proj_body,
        grid=(T // tm,),
        in_specs=[pl.BlockSpec((tm, D_MODEL), row),
                  pl.BlockSpec((1, 1, D_MODEL), per_b),
                  pl.BlockSpec((1, 1, D_MODEL), per_b),
                  pl.BlockSpec((tm, 1), row)] + [wspec(a) for a in weights],
        out_specs=[pl.BlockSpec((tm, w), row) for w in out_w],
        out_shape=[jax.ShapeDtypeStruct((T, w), BF16) for w in out_w],
        compiler_params=_cparams(("arbitrary",)),
        name="mix_proj",
    )(x2, sh1, sc1, posf, *weights)


_NT = (((1,), (1,)), ((), ()))


def _mla_rows(qh, k_ref, v_ref, sl, i, tq):
    kend = (i + 1) * tq
    s = lax.dot_general(qh, k_ref[0:kend, sl], _NT, preferred_element_type=F32)
    row = lax.broadcasted_iota(I32, (tq, tq), 0)
    col = lax.broadcasted_iota(I32, (tq, tq), 1)
    s_diag = jnp.where((col // CHUNK) <= (row // CHUNK), s[:, kend - tq:], NEG_BIG)
    s = s_diag if i == 0 else jnp.concatenate([s[:, :kend - tq], s_diag], axis=1)
    pr = jnp.exp2(s - jnp.max(s, axis=-1, keepdims=True))
    l = jnp.sum(pr, axis=-1, keepdims=True)
    return jnp.dot(pr.astype(BF16), v_ref[0:kend, :], preferred_element_type=F32) / l


def _mla_body(q_ref, k_ref, v_ref, o_ref):
    tq = ATT_BLOCK
    lane = lax.broadcasted_iota(I32, (tq, LANES), 1)
    for ii in range(q_ref.shape[0] // tq):
        rows = slice(ii * tq, (ii + 1) * tq)
        outs = []
        for hh in range(2):
            sl = slice(hh * HEAD_PAD, (hh + 1) * HEAD_PAD)
            outs.append(_mla_rows(q_ref[rows, sl], k_ref, v_ref, sl, ii, tq))
        o_ref[rows, :] = jnp.where(lane < V_DIM, outs[0], outs[1]).astype(BF16)


def _mla_call(q, k, v, B, S):
    q3 = q.reshape(B, S, MLA_HEADS * HEAD_PAD)
    k3 = k.reshape(B, S, MLA_HEADS * HEAD_PAD)
    v3 = v.reshape(B, S, MLA_HEADS * V_DIM)
    pair = lambda b, hp: (b, 0, hp)
    out = pl.pallas_call(
        _mla_body,
        grid=(B, MLA_HEADS // 2),
        in_specs=[pl.BlockSpec((None, S, 2 * HEAD_PAD), pair),
                  pl.BlockSpec((None, S, 2 * HEAD_PAD), pair),
                  pl.BlockSpec((None, S, 2 * V_DIM), pair)],
        out_specs=pl.BlockSpec((None, S, 2 * V_DIM), pair),
        out_shape=jax.ShapeDtypeStruct((B, S, MLA_HEADS * V_DIM), BF16),
        compiler_params=_cparams(("arbitrary", "arbitrary")),
        name="mla_attn",
    )(q3, k3, v3)
    return out.reshape(B * S, MLA_HEADS * V_DIM)


def _sb_rows(qh, k_ref, v_ref, i, tq):
    kend = (i + 1) * tq
    z = lax.dot_general(qh, k_ref[0:kend, :], _NT, preferred_element_type=F32) * SB_LOGIT_SCALE
    sp = jnp.maximum(z, 0.0) + jnp.log2(1.0 + jnp.exp2(jnp.minimum(z, -z)))
    row = lax.broadcasted_iota(I32, (tq, tq), 0)
    col = lax.broadcasted_iota(I32, (tq, tq), 1)
    strict = col < row
    suffix_ones = jnp.where(row >= col, 1.0, 0.0).astype(BF16)
    later = jnp.zeros((tq, 1), F32)
    a_blocks = [None] * (i + 1)
    for j in range(i, -1, -1):
        cols = slice(j * tq, (j + 1) * tq)
        spj = sp[:, cols]
        if j == i:
            spj = jnp.where(strict, spj, 0.0)
        suf = jnp.dot(spj.astype(BF16), suffix_ones, preferred_element_type=F32)
        a = jnp.exp2(z[:, cols] - suf - later)
        if j == i:
            a = jnp.where(strict, a, 0.0)
        a_blocks[j] = a.astype(BF16)
        later = later + suf[:, 0:1]
    a = a_blocks[0] if i == 0 else jnp.concatenate(a_blocks, axis=1)
    return jnp.dot(a, v_ref[0:kend, :], preferred_element_type=F32)


def _sb_body(q_ref, k_ref, v_ref, o_ref):
    tq = ATT_BLOCK
    lane = lax.broadcasted_iota(I32, (tq, LANES), 1)
    for ii in range(q_ref.shape[0] // tq):
        rows = slice(ii * tq, (ii + 1) * tq)
        q = q_ref[rows, :]
        outs = []
        for hh in range(2):
            in_head = (lane >= hh * SB_DIM) & (lane < (hh + 1) * SB_DIM)
            qh = jnp.where(in_head, q, jnp.zeros_like(q))
            outs.append(_sb_rows(qh, k_ref, v_ref, ii, tq))
        o_ref[rows, :] = jnp.where(lane < SB_DIM, outs[0], outs[1]).astype(BF16)


def _sb_call(sq, sk, sv, B, S):
    w = SB_HEADS * SB_DIM
    args = [a.reshape(B, S, w) for a in (sq, sk, sv)]
    pair = pl.BlockSpec((None, S, 2 * SB_DIM), lambda b, hp: (b, 0, hp))
    out = pl.pallas_call(
        _sb_body,
        grid=(B, SB_HEADS // 2),
        in_specs=[pair, pair, pair],
        out_specs=pair,
        out_shape=jax.ShapeDtypeStruct((B, S, w), BF16),
        compiler_params=_cparams(("arbitrary", "arbitrary")),
        name="sb_attn",
    )(*args)
    return out.reshape(B * S, w)


def _pack_pairs(h):
    n = h.shape[1] // 2
    return _pack_halves(h[:, :n], h[:, n:])


def _pack_halves(lo, hi):
    lo = pltpu.bitcast(lo.astype(BF16).astype(F32), U32)
    hi = pltpu.bitcast(hi.astype(BF16).astype(F32), U32)
    return (lo >> 16) | (hi & jnp.uint32(0xFFFF0000))


def _unpack_pairs(w):
    lo = pltpu.bitcast(w << 16, F32).astype(BF16)
    hi = pltpu.bitcast(w & jnp.uint32(0xFFFF0000), F32).astype(BF16)
    return lo, hi


def _merge_body(x_ref, sh1_ref, sc1_ref, ga1_ref, sh2_ref, sc2_ref, om_ref, os_ref,
                gn1_ref, wga_ref, wgb_ref, wom_ref, wos_ref, wout_ref, gn2_ref,
                wrh_ref, wrl_ref, br_ref,
                x1_ref, h2_ref, lg_ref):
    for r in range(x_ref.shape[0] // SUB_ROWS):
        rows = slice(r * SUB_ROWS, (r + 1) * SUB_ROWS)
        x = x_ref[rows, :]
        h = _modulated_norm(x, gn1_ref[...], sh1_ref[0], sc1_ref[0]).astype(BF16)
        ga = jax.nn.sigmoid(jnp.dot(h, wga_ref[...], preferred_element_type=F32))
        gb = jax.nn.sigmoid(jnp.dot(h, wgb_ref[...], preferred_element_type=F32))
        merged = (ga * jnp.dot(om_ref[rows, :], wom_ref[...], preferred_element_type=F32)
                  + gb * jnp.dot(os_ref[rows, :], wos_ref[...], preferred_element_type=F32))
        y = jnp.dot(merged.astype(BF16), wout_ref[...], preferred_element_type=F32)
        x1 = x + ga1_ref[0] * y
        x1_ref[rows, :] = x1
        h2 = _modulated_norm(x1, gn2_ref[...], sh2_ref[0], sc2_ref[0])
        h2_ref[rows, :] = _pack_pairs(h2)
        h2_hi = h2.astype(BF16)
        h2_lo = (h2 - h2_hi.astype(F32)).astype(BF16)
        lg_ref[rows, :] = (jnp.dot(h2_hi, wrh_ref[...], preferred_element_type=F32)
                           + jnp.dot(h2_hi, wrl_ref[...], preferred_element_type=F32)
                           + jnp.dot(h2_lo, wrh_ref[...], preferred_element_type=F32)
                           + br_ref[...])


def _merge_call(x2, mods, o_mla, o_sb, p, B, S):
    T = B * S
    tm = TOK_BLOCK
    nb = S // tm
    row = lambda i: (i, 0)
    per_b = lambda i: (i // nb, 0, 0)
    full = lambda i: (0, 0)
    weights = [p["g_norm1"], p["w_ga"], p["w_gb"], p["w_o_mla"], p["w_o_sb"], p["w_out"],
               p["g_norm2"], p["w_r_hi"], p["w_r_lo"], p["b_r"]]
    return pl.pallas_call(
        _merge_body,
        grid=(T // tm,),
        in_specs=[pl.BlockSpec((tm, D_MODEL), row)]
        + [pl.BlockSpec((1, 1, D_MODEL), per_b)] * 5
        + [pl.BlockSpec((tm, MLA_HEADS * V_DIM), row), pl.BlockSpec((tm, SB_HEADS * SB_DIM), row)]
        + [pl.BlockSpec(a.shape, full) for a in weights],
        out_specs=[pl.BlockSpec((tm, D_MODEL), row), pl.BlockSpec((tm, D_MODEL // 2), row),
                   pl.BlockSpec((tm, LANES), row)],
        out_shape=[jax.ShapeDtypeStruct((T, D_MODEL), F32),
                   jax.ShapeDtypeStruct((T, D_MODEL // 2), U32),
                   jax.ShapeDtypeStruct((T, LANES), F32)],
        compiler_params=_cparams(("arbitrary",)),
        name="merge_norm",
    )(x2, *mods, o_mla, o_sb, *weights)


def _route_body(lg_ref, tri_ref, su_ref, lpos_ref, w_ref, tab_ref, tot_ref, run_ref):
    @pl.when(pl.program_id(0) == 0)
    def _():
        run_ref[...] = jnp.zeros_like(run_ref)

    v = lg_ref[...]
    tm = v.shape[0]
    lane_i = lax.broadcasted_iota(I32, (tm, LANES), 1)
    lane = lane_i.astype(F32)
    vals, idxs = [], []
    for _ in range(TOP_K):
        m = jnp.max(v, axis=-1, keepdims=True)
        idx = jnp.min(jnp.where(v == m, lane, float(LANES)), axis=-1, keepdims=True)
        vals.append(m)
        idxs.append(idx)
        v = jnp.where(lane == idx, NEG_BIG, v)
    ex = [jnp.exp(t - vals[0]) for t in vals]
    denom = ex[0] + ex[1] + ex[2] + ex[3]
    onehots = [lane == idx for idx in idxs]
    oh = jnp.zeros((tm, LANES), F32)
    for o in onehots:
        oh = oh + jnp.where(o, 1.0, 0.0)
    units = jnp.floor((jnp.sum(oh, axis=0, keepdims=True) + (RUN_ALIGN - 1.0)) * (1.0 / RUN_ALIGN))
    lstart = RUN_ALIGN * jnp.dot(jnp.broadcast_to(units, (8, LANES)).astype(BF16), su_ref[...],
                                 preferred_element_type=F32)[0:1]
    pos = jnp.dot(tri_ref[...], oh.astype(BF16), preferred_element_type=F32) + lstart
    lp_out = jnp.zeros((tm, LANES), I32)
    w_out = jnp.zeros((tm, LANES), F32)
    for kk in range(TOP_K):
        lp = jnp.sum(jnp.where(onehots[kk], pos, 0.0), axis=-1, keepdims=True)
        lp_out = jnp.where(lane_i == kk, lp.astype(I32), lp_out)
        w_out = jnp.where(lane_i == kk, ex[kk] / denom, w_out)
    lpos_ref[...] = lp_out[:, :TOP_K]
    w_ref[...] = w_out[:, :TOP_K]
    sub = lax.broadcasted_iota(I32, (8, LANES), 0)
    tab = jnp.where(sub == 0, lstart, jnp.where(sub == 1, units, jnp.where(sub == 2, run_ref[...], 0.0)))
    tab_ref[0] = tab.astype(I32)
    run_ref[...] = run_ref[...] + RUN_ALIGN * units
    tot_ref[...] = run_ref[...]


def _route_call(logits, T):
    tm = ROW_BLOCK
    nblk = T // tm
    tri = jnp.asarray(np.arange(tm)[:, None] > np.arange(tm)[None, :], dtype=BF16)
    su = jnp.asarray(np.arange(LANES)[:, None] < np.arange(LANES)[None, :], dtype=BF16)
    row = lambda i: (i, 0)
    const = lambda i: (0, 0)
    return pl.pallas_call(
        _route_body,
        grid=(nblk,),
        in_specs=[pl.BlockSpec((tm, LANES), row), pl.BlockSpec((tm, tm), const),
                  pl.BlockSpec((LANES, LANES), const)],
        out_specs=[pl.BlockSpec((tm, TOP_K), row), pl.BlockSpec((tm, TOP_K), row),
                   pl.BlockSpec((1, 8, LANES), lambda i: (i, 0, 0)), pl.BlockSpec((1, LANES), const)],
        out_shape=[jax.ShapeDtypeStruct((T, TOP_K), I32), jax.ShapeDtypeStruct((T, TOP_K), F32),
                   jax.ShapeDtypeStruct((nblk, 8, LANES), I32), jax.ShapeDtypeStruct((1, LANES), F32)],
        scratch_shapes=[pltpu.VMEM((1, LANES), F32)],
        compiler_params=_cparams(("arbitrary",)),
        name="route_topk",
    )(logits, tri, su)


def _run_copies(fn, tabs, base, make_copy):
    ls_ref, ds_ref, nu_ref = tabs

    def per_expert(e, c):
        loc = 0 if ls_ref is None else ls_ref[base + e]
        dst = ds_ref[base + e]
        units = nu_ref[base + e]
        for level, bit in enumerate(RUN_BITS):
            rows = bit * RUN_ALIGN
            take = (units & bit) != 0

            @pl.when(take)
            def _():
                src = 0 if ls_ref is None else pl.multiple_of(loc, RUN_ALIGN)
                fn(make_copy(src, pl.multiple_of(dst, RUN_ALIGN), rows, level))

            step = jnp.where(take, rows, 0)
            if ls_ref is not None:
                loc = loc + step
            dst = dst + step
        return c

    lax.fori_loop(0, N_EXPERTS, per_expert, 0)


_TN = (((0,), (0,)), ((), ()))


def _slot_matrix(lpos, values):
    tm = lpos.shape[0]
    col = lax.broadcasted_iota(I32, (tm, LOCAL_ROWS), 1)
    out = jnp.zeros((tm, LOCAL_ROWS), F32)
    for kk in range(TOP_K):
        out = jnp.where(col == lpos[:, kk:kk + 1], values[kk], out)
    return out


def _start(cp):
    cp.start()


def _wait(cp):
    cp.wait()


def _table_base(blk):
    return (blk % TABLE_BLOCKS) * N_EXPERTS


def _dispatch_body(ls_ref, ds_ref, nu_ref, pls_ref, pds_ref, pnu_ref, tds_ref, tnu_ref,
                   h_ref, lpos_ref, xs_hbm, loc_ref, zero_ref, sems):
    blk = pl.program_id(0)
    last = pl.num_programs(0) - 1
    slot = blk % 2
    lo, hi = _unpack_pairs(h_ref[...])
    sel = _slot_matrix(lpos_ref[...], [1.0] * TOP_K).astype(BF16)
    x_lo = lax.dot_general(sel, lo, _TN, preferred_element_type=F32)
    x_hi = lax.dot_general(sel, hi, _TN, preferred_element_type=F32)
    loc_ref[slot] = _pack_halves(x_lo, x_hi)

    def run_copy(s):
        def make_copy(loc, dst, rows, level):
            return pltpu.make_async_copy(loc_ref.at[s, pl.ds(loc, rows)], xs_hbm.at[pl.ds(dst, rows)],
                                         sems.at[level])
        return make_copy

    def tail_copy(loc, dst, rows, level):
        return pltpu.make_async_copy(zero_ref.at[pl.ds(loc, rows)], xs_hbm.at[pl.ds(dst, rows)],
                                     sems.at[level])

    cur = (ls_ref, ds_ref, nu_ref)

    @pl.when(blk > 0)
    def _():
        _run_copies(_wait, (pls_ref, pds_ref, pnu_ref), _table_base(blk - 1), run_copy(1 - slot))

    _run_copies(_start, cur, _table_base(blk), run_copy(slot))

    @pl.when(blk == last)
    def _():
        _run_copies(_wait, cur, _table_base(blk), run_copy(slot))
        zero_ref[...] = jnp.zeros_like(zero_ref)
        tails = (None, tds_ref, tnu_ref)
        _run_copies(_start, tails, 0, tail_copy)
        _run_copies(_wait, tails, 0, tail_copy)


def _table_specs(shift, nblk):
    def index(i):
        return (jnp.clip(i + shift, 0, nblk - 1) // TABLE_BLOCKS,)

    blk = pl.BlockSpec((TABLE_BLOCKS * N_EXPERTS,), index, memory_space=pltpu.SMEM)
    return [blk, blk, blk]


def _dispatch_call(tabs, tail_tabs, h2p, lpos, n_rows):
    T = h2p.shape[0]
    whole = pl.BlockSpec((TABLE_BLOCKS * N_EXPERTS,), lambda i: (0,), memory_space=pltpu.SMEM)
    return pl.pallas_call(
        _dispatch_body,
        grid=(T // ROW_BLOCK,),
        in_specs=_table_specs(0, T // ROW_BLOCK) + _table_specs(-1, T // ROW_BLOCK) + [whole, whole] + [
            pl.BlockSpec((ROW_BLOCK, D_MODEL // 2), lambda i: (i, 0)),
            pl.BlockSpec((ROW_BLOCK, TOP_K), lambda i: (i, 0))],
        out_specs=pl.BlockSpec(memory_space=pl.ANY),
        scratch_shapes=[pltpu.VMEM((2, LOCAL_ROWS, D_MODEL // 2), U32),
                        pltpu.VMEM((RUN_BITS[0] * RUN_ALIGN, D_MODEL // 2), U32),
                        pltpu.SemaphoreType.DMA((len(RUN_BITS),))],
        out_shape=jax.ShapeDtypeStruct((n_rows, D_MODEL // 2), U32),
        compiler_params=_cparams(("arbitrary",)),
        name="moe_dispatch",
    )(*tabs, *tabs, *tail_tabs, h2p, lpos)


def _expert_body(te_ref, tv_ref, xs_ref, wgu_ref, bgu_ref, wd_ref, bd_ref, y_ref):
    i = pl.program_id(0)

    @pl.when(tv_ref[i] == 1)
    def _():
        lo, hi = _unpack_pairs(xs_ref[...])
        half = D_MODEL // 2
        gu = (jnp.dot(lo, wgu_ref[:half, :], preferred_element_type=F32)
              + jnp.dot(hi, wgu_ref[half:, :], preferred_element_type=F32) + bgu_ref[...])
        gate = jnp.minimum(gu[:, :D_EXPERT], SWIGLU_LIMIT)
        up = jnp.clip(gu[:, D_EXPERT:], -SWIGLU_LIMIT, SWIGLU_LIMIT)
        hid = (up + 1.0) * (gate * jax.nn.sigmoid(SWIGLU_ALPHA * gate))
        y = jnp.dot(hid.astype(BF16), wd_ref[...], preferred_element_type=F32) + bd_ref[...]
        y_ref[...] = _pack_pairs(y)

    @pl.when(tv_ref[i] == 0)
    def _():
        y_ref[...] = jnp.zeros_like(y_ref)


def _expert_call(tile_e, tile_v, xs, p):
    n_rows = xs.shape[0]
    tmx = EXPERT_TILE
    by_e = lambda i, te, tv: (te[i], 0, 0)
    return pl.pallas_call(
        _expert_body,
        grid_spec=pltpu.PrefetchScalarGridSpec(
            num_scalar_prefetch=2,
            grid=(n_rows // tmx,),
            in_specs=[pl.BlockSpec((tmx, D_MODEL // 2), lambda i, te, tv: (i * tv[i], 0)),
                      pl.BlockSpec((None, D_MODEL, 2 * D_EXPERT), by_e),
                      pl.BlockSpec((None, 1, 2 * D_EXPERT), by_e),
                      pl.BlockSpec((None, D_EXPERT, D_MODEL), by_e),
                      pl.BlockSpec((None, 1, D_MODEL), by_e)],
            out_specs=pl.BlockSpec((tmx, D_MODEL // 2), lambda i, te, tv: (i, 0))),
        out_shape=jax.ShapeDtypeStruct((n_rows, D_MODEL // 2), U32),
        compiler_params=_cparams(("arbitrary",)),
        name="moe_experts",
    )(tile_e, tile_v, xs, p["w_gate_up"], p["b_gate_up"], p["w_down"], p["b_down"])


def _combine_body(ls_ref, ds_ref, nu_ref, nls_ref, nds_ref, nnu_ref, x1_ref, lpos_ref, w_ref,
                  ga2_ref, ys_hbm, o_ref, loc_ref, sems):
    blk = pl.program_id(0)
    last = pl.num_programs(0) - 1
    slot = blk % 2

    def run_copy(s):
        def make_copy(loc, src, rows, level):
            return pltpu.make_async_copy(ys_hbm.at[pl.ds(src, rows)], loc_ref.at[s, pl.ds(loc, rows)],
                                         sems.at[s, level])
        return make_copy

    def fetch(tabs, b, s):
        loc_ref[s] = jnp.zeros(loc_ref.shape[1:], loc_ref.dtype)
        _run_copies(_start, tabs, _table_base(b), run_copy(s))

    cur = (ls_ref, ds_ref, nu_ref)

    @pl.when(blk == 0)
    def _():
        fetch(cur, blk, slot)

    @pl.when(blk < last)
    def _():
        fetch((nls_ref, nds_ref, nnu_ref), blk + 1, 1 - slot)

    _run_copies(_wait, cur, _table_base(blk), run_copy(slot))

    y_lo, y_hi = _unpack_pairs(loc_ref[slot])
    w = w_ref[...]
    mix = _slot_matrix(lpos_ref[...], [w[:, kk:kk + 1] for kk in range(TOP_K)])
    mix_hi = mix.astype(BF16)
    mix_lo = (mix - mix_hi.astype(F32)).astype(BF16)
    halves = [jnp.dot(mix_hi, y, preferred_element_type=F32) + jnp.dot(mix_lo, y, preferred_element_type=F32)
              for y in (y_lo, y_hi)]
    o_ref[...] = x1_ref[...] + ga2_ref[0] * jnp.concatenate(halves, axis=1)


def _combine_call(tabs, x1, lpos, wts, ga2, ys, B, S):
    T = B * S
    nb = S // ROW_BLOCK
    row = lambda i: (i, 0)
    return pl.pallas_call(
        _combine_body,
        grid=(T // ROW_BLOCK,),
        in_specs=_table_specs(0, T // ROW_BLOCK) + _table_specs(1, T // ROW_BLOCK) + [
            pl.BlockSpec((ROW_BLOCK, D_MODEL), row),
            pl.BlockSpec((ROW_BLOCK, TOP_K), row),
            pl.BlockSpec((ROW_BLOCK, TOP_K), row),
            pl.BlockSpec((1, 1, D_MODEL), lambda i: (i // nb, 0, 0)),
            pl.BlockSpec(memory_space=pl.ANY)],
        out_specs=pl.BlockSpec((ROW_BLOCK, D_MODEL), row),
        scratch_shapes=[pltpu.VMEM((2, LOCAL_ROWS, D_MODEL // 2), U32),
                        pltpu.SemaphoreType.DMA((2, len(RUN_BITS)))],
        out_shape=jax.ShapeDtypeStruct((T, D_MODEL), F32),
        compiler_params=_cparams(("arbitrary",)),
        name="moe_combine",
    )(*tabs, *tabs, x1, lpos, wts, ga2, ys)


def _pad_heads(w, per_head, n_heads):
    k = w.shape[0]
    w = w.reshape(k, n_heads, per_head)
    w = jnp.pad(w, ((0, 0), (0, 0), (0, HEAD_PAD - per_head)))
    return w.reshape(k, n_heads * HEAD_PAD)


def _prepare(w_in, g_norm1, g_q_lat, w_uq, g_kv_lat, w_ukv, g_qk_q, g_qk_k, w_o_mla, w_o_sb,
             w_out, g_norm2, w_router, b_router, w_gate_up, b_gate_up, w_down, b_down):
    c0 = Q_LORA
    c1 = c0 + KV_LORA
    c2 = c1 + QK_ROPE
    sbw = SB_HEADS * SB_DIM
    c3 = c2 + 3 * sbw
    c4 = c3 + D_MODEL
    p = {}
    p["g_norm1"] = g_norm1.reshape(1, D_MODEL)
    p["g_norm2"] = g_norm2.reshape(1, D_MODEL)
    p["w_ql"] = w_in[:, :c0].astype(BF16)
    p["w_kvl"] = w_in[:, c0:c1].astype(BF16)
    p["w_kpe"] = jnp.pad(w_in[:, c1:c2], ((0, 0), (QK_NOPE, HEAD_PAD - QK_DIM))).astype(BF16)
    p["w_sq"] = w_in[:, c2:c2 + sbw].astype(BF16)
    p["w_sk"] = w_in[:, c2 + sbw:c2 + 2 * sbw].astype(BF16)
    p["w_sv"] = w_in[:, c2 + 2 * sbw:c3].astype(BF16)
    p["w_ga"] = w_in[:, c3:c4].astype(BF16)
    p["w_gb"] = w_in[:, c4:].astype(BF16)
    p["g_q_lat"] = g_q_lat.reshape(1, Q_LORA)
    p["g_kv_lat"] = g_kv_lat.reshape(1, KV_LORA)
    p["w_uq"] = _pad_heads(w_uq, QK_DIM, MLA_HEADS).astype(BF16)
    kv = w_ukv.reshape(KV_LORA, MLA_HEADS, QK_NOPE + V_DIM)
    p["w_uk"] = _pad_heads(kv[:, :, :QK_NOPE].reshape(KV_LORA, MLA_HEADS * QK_NOPE),
                           QK_NOPE, MLA_HEADS).astype(BF16)
    p["w_v"] = kv[:, :, QK_NOPE:].reshape(KV_LORA, MLA_HEADS * V_DIM).astype(BF16)
    p["g_qk_q"] = jnp.pad(g_qk_q, (0, HEAD_PAD - QK_DIM)).reshape(1, HEAD_PAD)
    p["g_qk_k"] = jnp.pad(g_qk_k, (0, HEAD_PAD - QK_DIM)).reshape(1, HEAD_PAD)
    inv_freq = 1.0 / (ROPE_THETA ** (jnp.arange(0, QK_ROPE, 2, dtype=F32) / QK_ROPE))
    p["invf"] = jnp.concatenate([jnp.zeros((QK_NOPE,), F32), inv_freq, inv_freq,
                                 jnp.zeros((HEAD_PAD - QK_DIM,), F32)]).reshape(1, HEAD_PAD)
    p["w_o_mla"] = w_o_mla.astype(BF16)
    p["w_o_sb"] = w_o_sb.astype(BF16)
    p["w_out"] = w_out.astype(BF16)
    wr = jnp.pad(w_router, ((0, 0), (0, LANES - N_EXPERTS)))
    p["w_r_hi"] = wr.astype(BF16)
    p["w_r_lo"] = (wr - p["w_r_hi"].astype(F32)).astype(BF16)
    p["b_r"] = jnp.concatenate([b_router, jnp.full((LANES - N_EXPERTS,), NEG_BIG, F32)]).reshape(1, LANES)
    p["w_gate_up"] = w_gate_up.astype(BF16)
    p["b_gate_up"] = b_gate_up.reshape(N_EXPERTS, 1, 2 * D_EXPERT)
    p["w_down"] = w_down.astype(BF16)
    p["b_down"] = b_down.reshape(N_EXPERTS, 1, D_MODEL)
    return p


def _layer(x2, c, posf, B, S, w_ada, b_ada, *layer_weights):
    T = B * S
    p = _prepare(*layer_weights)
    mod = _ada_call(c, w_ada, b_ada)
    sh1, sc1, ga1, sh2, sc2, ga2 = [m.reshape(B, 1, D_MODEL) for m in jnp.split(mod, 6, axis=-1)]

    q, k, v, sq, sk, sv = _proj_call(x2, sh1, sc1, posf, p, B, S)
    o_mla = _mla_call(q, k, v, B, S)
    o_sb = _sb_call(sq, sk, sv, B, S)
    x1, h2p, logits = _merge_call(x2, (sh1, sc1, ga1, sh2, sc2), o_mla, o_sb, p, B, S)

    lpos, wts, tab, totals = _route_call(logits, T)
    nblk = T // ROW_BLOCK
    rows_e = totals[0, :N_EXPERTS].astype(I32)
    tiles_e = (rows_e + EXPERT_TILE - 1) // EXPERT_TILE
    tile_end = jnp.cumsum(tiles_e)
    region = ((tile_end - tiles_e) * EXPERT_TILE).astype(I32)
    n_tiles = (T * TOP_K + nblk * N_EXPERTS * (RUN_ALIGN - 1)) // EXPERT_TILE + N_EXPERTS
    tile_ids = jnp.arange(n_tiles, dtype=I32)
    tile_e = jnp.minimum(jnp.sum((tile_ids[:, None] >= tile_end[None, :]).astype(I32), axis=1),
                         N_EXPERTS - 1).astype(I32)
    tile_v = (tile_ids < tile_end[-1]).astype(I32)
    pad = (-nblk) % TABLE_BLOCKS

    def flat(t):
        return jnp.pad(t, ((0, pad), (0, 0))).reshape(-1)

    tabs = (flat(tab[:, 0, :N_EXPERTS]), flat(tab[:, 2, :N_EXPERTS] + region[None, :]),
            flat(tab[:, 1, :N_EXPERTS]))
    table_len = TABLE_BLOCKS * N_EXPERTS
    tail_tabs = tuple(jnp.pad(t, (0, table_len - N_EXPERTS))
                      for t in (region + rows_e, (tiles_e * EXPERT_TILE - rows_e) // RUN_ALIGN))

    xs = _dispatch_call(tabs, tail_tabs, h2p, lpos, n_tiles * EXPERT_TILE)
    ys = _expert_call(tile_e, tile_v, xs, p)
    return _combine_call(tabs, x1, lpos, wts, ga2, ys, B, S)


def kernel(x, c, positions, w_ada, b_ada, g_norm1, w_in, g_q_lat, w_uq, g_kv_lat, w_ukv, g_qk_q,
           g_qk_k, w_o_mla, w_o_sb, w_out, g_norm2, w_router, b_router, w_gate_up, b_gate_up,
           w_down, b_down):
    B, S, D = x.shape
    x2 = x.reshape(B * S, D)
    posf = positions.astype(F32).reshape(B * S, 1)
    for l in range(w_ada.shape[0]):
        x2 = _layer(x2, c, posf, B, S, w_ada[l], b_ada[l], w_in[l], g_norm1[l], g_q_lat[l],
                    w_uq[l], g_kv_lat[l], w_ukv[l], g_qk_q[l], g_qk_k[l], w_o_mla[l], w_o_sb[l],
                    w_out[l], g_norm2[l], w_router[l], b_router[l], w_gate_up[l], b_gate_up[l],
                    w_down[l], b_down[l])
    return x2.reshape(B, S, D)
```

```python
import functools

import jax
import jax.numpy as jnp
import numpy as np
from jax import lax
from jax.experimental import pallas as pl
from jax.experimental.pallas import tpu as pltpu

F32 = jnp.float32
BF16 = jnp.bfloat16
I32 = jnp.int32
U32 = jnp.uint32

D_MODEL = 1024
EPS = 1e-6
CHUNK = 64
MLA_HEADS = 8
Q_LORA = 384
KV_LORA = 256
QK_NOPE = 64
QK_ROPE = 32
V_DIM = 64
QK_DIM = QK_NOPE + QK_ROPE
ROPE_THETA = 10000.0
SB_HEADS = 8
SB_DIM = 64
N_EXPERTS = 32
TOP_K = 4
D_EXPERT = D_MODEL
SWIGLU_LIMIT = 7.0
SWIGLU_ALPHA = 1.702

LANES = 128
HEAD_PAD = LANES
TOK_BLOCK = 1024
SUB_ROWS = 256
ATT_BLOCK = 256
ROW_BLOCK = 256
EXPERT_TILE = 1024
RUN_ALIGN = 8
RUN_BITS = tuple(2 ** b for b in reversed(range((ROW_BLOCK // RUN_ALIGN).bit_length())))
LOCAL_ROWS = ROW_BLOCK * TOP_K + N_EXPERTS * RUN_ALIGN
TABLE_BLOCKS = 32
FILL_CHUNK = TABLE_BLOCKS * N_EXPERTS
DUMP_OFFSETS = tuple(RUN_ALIGN * sum(RUN_BITS[:l]) for l in range(len(RUN_BITS)))
DUMP_ROWS = RUN_ALIGN * 2 * RUN_BITS[0]
TAIL_BITS = tuple(2 ** b for b in reversed(range((EXPERT_TILE // RUN_ALIGN - 1).bit_length())))
NEG_BIG = -1e30
LOG2E = 1.4426950408889634
MLA_LOGIT_SCALE = QK_DIM ** -0.5 * LOG2E
SB_LOGIT_SCALE = SB_DIM ** -0.5 * LOG2E
VMEM_LIMIT = 48 * 1024 * 1024


def _cparams(sem):
    return pltpu.CompilerParams(dimension_semantics=sem, vmem_limit_bytes=VMEM_LIMIT)


def _ada_body(c_ref, w_ref, b_ref, o_ref):
    c = c_ref[...]
    s = c * jax.nn.sigmoid(c)
    o_ref[...] = jnp.dot(s, w_ref[...], preferred_element_type=F32,
                         precision=lax.Precision.HIGHEST) + b_ref[...]


def _ada_call(c, w_ada, b_ada):
    B = c.shape[0]
    n = w_ada.shape[1]
    bn = 512
    return pl.pallas_call(
        _ada_body,
        grid=(n // bn,),
        in_specs=[pl.BlockSpec((B, D_MODEL), lambda j: (0, 0)),
                  pl.BlockSpec((D_MODEL, bn), lambda j: (0, j)),
                  pl.BlockSpec((1, bn), lambda j: (0, j))],
        out_specs=pl.BlockSpec((B, bn), lambda j: (0, j)),
        out_shape=jax.ShapeDtypeStruct((B, n), F32),
        compiler_params=_cparams(("arbitrary",)),
        name="ada_mod",
    )(c, w_ada, b_ada.reshape(1, n))


def _rms(v, width):
    return lax.rsqrt(jnp.sum(v * v, axis=-1, keepdims=True) * (1.0 / width) + EPS)


def _modulated_norm(x, g, sh, sc):
    h = x * _rms(x, D_MODEL) * g
    return h * (1.0 + sc) + sh


def _proj_body(x_ref, sh_ref, sc_ref, pos_ref, gn_ref, wql_ref, wkvl_ref, wkpe_ref,
               wsq_ref, wsk_ref, wsv_ref, gql_ref, wuq_ref, gkvl_ref, wuk_ref, wv_ref,
               gq_ref, gk_ref, invf_ref,
               q_ref, k_ref, v_ref, sq_ref, sk_ref, sv_ref):
    for r in range(x_ref.shape[0] // SUB_ROWS):
        rows = slice(r * SUB_ROWS, (r + 1) * SUB_ROWS)
        x = x_ref[rows, :]
        h = _modulated_norm(x, gn_ref[...], sh_ref[0], sc_ref[0]).astype(BF16)

        sq_ref[rows, :] = (jnp.dot(h, wsq_ref[...], preferred_element_type=F32) * SB_LOGIT_SCALE).astype(BF16)
        sk_ref[rows, :] = jnp.dot(h, wsk_ref[...], preferred_element_type=F32).astype(BF16)
        sv_ref[rows, :] = jnp.dot(h, wsv_ref[...], preferred_element_type=F32).astype(BF16)

        q_lat = jnp.dot(h, wql_ref[...], preferred_element_type=F32)
        kv_lat = jnp.dot(h, wkvl_ref[...], preferred_element_type=F32)
        kpe = jnp.dot(h, wkpe_ref[...], preferred_element_type=F32)

        qn = (q_lat * _rms(q_lat, Q_LORA) * gql_ref[...]).astype(BF16)
        kvn = (kv_lat * _rms(kv_lat, KV_LORA) * gkvl_ref[...]).astype(BF16)
        q = jnp.dot(qn, wuq_ref[...], preferred_element_type=F32)
        kn = jnp.dot(kvn, wuk_ref[...], preferred_element_type=F32)
        v_ref[rows, :] = jnp.dot(kvn, wv_ref[...], preferred_element_type=F32).astype(BF16)

        lane = lax.broadcasted_iota(I32, (SUB_ROWS, LANES), 1)
        ang = pos_ref[rows, :] * invf_ref[...]
        cosf = jnp.cos(ang)
        sinf = jnp.sin(ang)
        half = QK_ROPE // 2
        s_lo = jnp.where((lane >= QK_NOPE) & (lane < QK_NOPE + half), -sinf, 0.0)
        s_hi = jnp.where((lane >= QK_NOPE + half) & (lane < QK_DIM), sinf, 0.0)

        def rope(t):
            return (t * cosf + pltpu.roll(t, LANES - half, 1) * s_lo
                    + pltpu.roll(t, half, 1) * s_hi)

        gq = gq_ref[...] * MLA_LOGIT_SCALE
        gk = gk_ref[...]
        kpe_sq = jnp.sum(kpe * kpe, axis=-1, keepdims=True)
        kr = rope(kpe * gk)
        for hh in range(MLA_HEADS):
            sl = slice(hh * HEAD_PAD, (hh + 1) * HEAD_PAD)
            qh = q[:, sl]
            q_ref[rows, sl] = rope(qh * _rms(qh, QK_DIM) * gq).astype(BF16)
            kh = kn[:, sl]
            rk = lax.rsqrt((jnp.sum(kh * kh, axis=-1, keepdims=True) + kpe_sq) * (1.0 / QK_DIM) + EPS)
            k_ref[rows, sl] = ((kh * gk + kr) * rk).astype(BF16)


def _proj_call(x2, sh1, sc1, posf, p, B, S):
    T = B * S
    tm = TOK_BLOCK
    nb = S // tm
    row = lambda i: (i, 0)
    per_b = lambda i: (i // nb, 0, 0)
    full = lambda i: (0, 0)

    def wspec(a):
        return pl.BlockSpec(a.shape, full)

    weights = [p["g_norm1"], p["w_ql"], p["w_kvl"], p["w_kpe"], p["w_sq"], p["w_sk"], p["w_sv"],
               p["g_q_lat"], p["w_uq"], p["g_kv_lat"], p["w_uk"], p["w_v"], p["g_qk_q"],
               p["g_qk_k"], p["invf"]]
    out_w = [MLA_HEADS * HEAD_PAD, MLA_HEADS * HEAD_PAD, MLA_HEADS * V_DIM,
             SB_HEADS * SB_DIM, SB_HEADS * SB_DIM, SB_HEADS * SB_DIM]
    return pl.pallas_call(
        _proj_body,
        grid=(T // tm,),
        in_specs=[pl.BlockSpec((tm, D_MODEL), row),
                  pl.BlockSpec((1, 1, D_MODEL), per_b),
                  pl.BlockSpec((1, 1, D_MODEL), per_b),
                  pl.BlockSpec((tm, 1), row)] + [wspec(a) for a in weights],
        out_specs=[pl.BlockSpec((tm, w), row) for w in out_w],
        out_shape=[jax.ShapeDtypeStruct((T, w), BF16) for w in out_w],
        compiler_params=_cparams(("arbitrary",)),
        name="mix_proj",
    )(x2, sh1, sc1, posf, *weights)


_NT = (((1,), (1,)), ((), ()))


def _mla_rows(qh, k_ref, v_ref, sl, i, tq):
    kend = (i + 1) * tq
    s = lax.dot_general(qh, k_ref[0:kend, sl], _NT, preferred_element_type=F32)
    row = lax.broadcasted_iota(I32, (tq, tq), 0)
    col = lax.broadcasted_iota(I32, (tq, tq), 1)
    s_diag = jnp.where((col // CHUNK) <= (row // CHUNK), s[:, kend - tq:], NEG_BIG)
    s = s_diag if i == 0 else jnp.concatenate([s[:, :kend - tq], s_diag], axis=1)
    pr = jnp.exp2(s - jnp.max(s, axis=-1, keepdims=True))
    l = jnp.sum(pr, axis=-1, keepdims=True)
    return jnp.dot(pr.astype(BF16), v_ref[0:kend, :], preferred_element_type=F32) / l


def _mla_body(q_ref, k_ref, v_ref, o_ref):
    tq = ATT_BLOCK
    lane = lax.broadcasted_iota(I32, (tq, LANES), 1)
    for ii in range(q_ref.shape[0] // tq):
        rows = slice(ii * tq, (ii + 1) * tq)
        outs = []
        for hh in range(2):
            sl = slice(hh * HEAD_PAD, (hh + 1) * HEAD_PAD)
            outs.append(_mla_rows(q_ref[rows, sl], k_ref, v_ref, sl, ii, tq))
        o_ref[rows, :] = jnp.where(lane < V_DIM, outs[0], outs[1]).astype(BF16)


def _mla_call(q, k, v, B, S):
    q3 = q.reshape(B, S, MLA_HEADS * HEAD_PAD)
    k3 = k.reshape(B, S, MLA_HEADS * HEAD_PAD)
    v3 = v.reshape(B, S, MLA_HEADS * V_DIM)
    pair = lambda b, hp: (b, 0, hp)
    out = pl.pallas_call(
        _mla_body,
        grid=(B, MLA_HEADS // 2),
        in_specs=[pl.BlockSpec((None, S, 2 * HEAD_PAD), pair),
                  pl.BlockSpec((None, S, 2 * HEAD_PAD), pair),
                  pl.BlockSpec((None, S, 2 * V_DIM), pair)],
        out_specs=pl.BlockSpec((None, S, 2 * V_DIM), pair),
        out_shape=jax.ShapeDtypeStruct((B, S, MLA_HEADS * V_DIM), BF16),
        compiler_params=_cparams(("arbitrary", "arbitrary")),
        name="mla_attn",
    )(q3, k3, v3)
    return out.reshape(B * S, MLA_HEADS * V_DIM)


def _sb_rows(qh, k_ref, v_ref, i, tq):
    kend = (i + 1) * tq
    z = lax.dot_general(qh, k_ref[0:kend, :], _NT, preferred_element_type=F32)
    sp = jnp.maximum(z, 0.0) + jnp.log2(1.0 + jnp.exp2(jnp.minimum(z, -z)))
    row = lax.broadcasted_iota(I32, (tq, tq), 0)
    col = lax.broadcasted_iota(I32, (tq, tq), 1)
    strict = col < row
    suffix_ones = jnp.where(row >= col, 1.0, 0.0).astype(BF16)
    later = jnp.zeros((tq, 1), F32)
    a_blocks = [None] * (i + 1)
    for j in range(i, -1, -1):
        cols = slice(j * tq, (j + 1) * tq)
        spj = sp[:, cols]
        if j == i:
            spj = jnp.where(strict, spj, 0.0)
        suf = jnp.dot(spj.astype(BF16), suffix_ones, preferred_element_type=F32)
        a = jnp.exp2(z[:, cols] - suf - later)
        if j == i:
            a = jnp.where(strict, a, 0.0)
        a_blocks[j] = a.astype(BF16)
        later = later + suf[:, 0:1]
    a = a_blocks[0] if i == 0 else jnp.concatenate(a_blocks, axis=1)
    return jnp.dot(a, v_ref[0:kend, :], preferred_element_type=F32)


def _sb_body(q_ref, k_ref, v_ref, o_ref):
    tq = ATT_BLOCK
    lane = lax.broadcasted_iota(I32, (tq, LANES), 1)
    for ii in range(q_ref.shape[0] // tq):
        rows = slice(ii * tq, (ii + 1) * tq)
        q = q_ref[rows, :]
        outs = []
        for hh in range(2):
            in_head = (lane >= hh * SB_DIM) & (lane < (hh + 1) * SB_DIM)
            qh = jnp.where(in_head, q, jnp.zeros_like(q))
            outs.append(_sb_rows(qh, k_ref, v_ref, ii, tq))
        o_ref[rows, :] = jnp.where(lane < SB_DIM, outs[0], outs[1]).astype(BF16)


def _sb_call(sq, sk, sv, B, S):
    w = SB_HEADS * SB_DIM
    args = [a.reshape(B, S, w) for a in (sq, sk, sv)]
    pair = pl.BlockSpec((None, S, 2 * SB_DIM), lambda b, hp: (b, 0, hp))
    out = pl.pallas_call(
        _sb_body,
        grid=(B, SB_HEADS // 2),
        in_specs=[pair, pair, pair],
        out_specs=pair,
        out_shape=jax.ShapeDtypeStruct((B, S, w), BF16),
        compiler_params=_cparams(("arbitrary", "arbitrary")),
        name="sb_attn",
    )(*args)
    return out.reshape(B * S, w)


def _pack_pairs(h):
    n = h.shape[1] // 2
    return _pack_halves(h[:, :n], h[:, n:])


def _pack_halves(lo, hi):
    lo = pltpu.bitcast(lo.astype(BF16).astype(F32), U32)
    hi = pltpu.bitcast(hi.astype(BF16).astype(F32), U32)
    return (lo >> 16) | (hi & jnp.uint32(0xFFFF0000))


def _unpack_pairs(w):
    lo = pltpu.bitcast(w << 16, F32).astype(BF16)
    hi = pltpu.bitcast(w & jnp.uint32(0xFFFF0000), F32).astype(BF16)
    return lo, hi


def _merge_body(x_ref, sh1_ref, sc1_ref, ga1_ref, sh2_ref, sc2_ref, om_ref, os_ref,
                gn1_ref, wga_ref, wgb_ref, wom_ref, wos_ref, wout_ref, gn2_ref,
                wrh_ref, wrl_ref, br_ref, tri_ref, su_ref,
                x1_ref, h2_ref, lpos_ref, w_ref, tab_ref, tot_ref, run_ref):
    @pl.when(pl.program_id(0) == 0)
    def _():
        run_ref[...] = jnp.zeros_like(run_ref)

    run = run_ref[...]
    for r in range(x_ref.shape[0] // SUB_ROWS):
        rows = slice(r * SUB_ROWS, (r + 1) * SUB_ROWS)
        x = x_ref[rows, :]
        h = _modulated_norm(x, gn1_ref[...], sh1_ref[0], sc1_ref[0]).astype(BF16)
        ga = jax.nn.sigmoid(jnp.dot(h, wga_ref[...], preferred_element_type=F32))
        gb = jax.nn.sigmoid(jnp.dot(h, wgb_ref[...], preferred_element_type=F32))
        merged = (ga * jnp.dot(om_ref[rows, :], wom_ref[...], preferred_element_type=F32)
                  + gb * jnp.dot(os_ref[rows, :], wos_ref[...], preferred_element_type=F32))
        y = jnp.dot(merged.astype(BF16), wout_ref[...], preferred_element_type=F32)
        x1 = x + ga1_ref[0] * y
        x1_ref[rows, :] = x1
        h2 = _modulated_norm(x1, gn2_ref[...], sh2_ref[0], sc2_ref[0])
        h2_ref[rows, :] = _pack_pairs(h2)
        h2_hi = h2.astype(BF16)
        h2_lo = (h2 - h2_hi.astype(F32)).astype(BF16)
        logits = (jnp.dot(h2_hi, wrh_ref[...], preferred_element_type=F32)
                  + jnp.dot(h2_hi, wrl_ref[...], preferred_element_type=F32)
                  + jnp.dot(h2_lo, wrh_ref[...], preferred_element_type=F32)
                  + br_ref[...])
        lpos, wts, tab, run = _route_rows(logits, tri_ref[...], su_ref[...], run)
        lpos_ref[rows, :] = lpos
        w_ref[rows, :] = wts
        tab_ref[r] = tab
    run_ref[...] = run
    tot_ref[...] = run


def _merge_call(x2, mods, o_mla, o_sb, p, B, S):
    assert SUB_ROWS == ROW_BLOCK
    T = B * S
    tm = TOK_BLOCK
    nb = S // tm
    per_step = tm // ROW_BLOCK
    row = lambda i: (i, 0)
    per_b = lambda i: (i // nb, 0, 0)
    full = lambda i: (0, 0)
    tri = jnp.asarray(np.arange(ROW_BLOCK)[:, None] > np.arange(ROW_BLOCK)[None, :], dtype=BF16)
    su = jnp.asarray(np.arange(LANES)[:, None] < np.arange(LANES)[None, :], dtype=BF16)
    weights = [p["g_norm1"], p["w_ga"], p["w_gb"], p["w_o_mla"], p["w_o_sb"], p["w_out"],
               p["g_norm2"], p["w_r_hi"], p["w_r_lo"], p["b_r"], tri, su]
    return pl.pallas_call(
        _merge_body,
        grid=(T // tm,),
        in_specs=[pl.BlockSpec((tm, D_MODEL), row)]
        + [pl.BlockSpec((1, 1, D_MODEL), per_b)] * 5
        + [pl.BlockSpec((tm, MLA_HEADS * V_DIM), row), pl.BlockSpec((tm, SB_HEADS * SB_DIM), row)]
        + [pl.BlockSpec(a.shape, full) for a in weights],
        out_specs=[pl.BlockSpec((tm, D_MODEL), row), pl.BlockSpec((tm, D_MODEL // 2), row),
                   pl.BlockSpec((tm, TOP_K), row), pl.BlockSpec((tm, TOP_K), row),
                   pl.BlockSpec((per_step, 8, LANES), lambda i: (i, 0, 0)),
                   pl.BlockSpec((1, LANES), full)],
        out_shape=[jax.ShapeDtypeStruct((T, D_MODEL), F32),
                   jax.ShapeDtypeStruct((T, D_MODEL // 2), U32),
                   jax.ShapeDtypeStruct((T, TOP_K), I32),
                   jax.ShapeDtypeStruct((T, TOP_K), F32),
                   jax.ShapeDtypeStruct((T // ROW_BLOCK, 8, LANES), I32),
                   jax.ShapeDtypeStruct((1, LANES), F32)],
        scratch_shapes=[pltpu.VMEM((1, LANES), F32)],
        compiler_params=_cparams(("arbitrary",)),
        name="merge_route",
    )(x2, *mods, o_mla, o_sb, *weights)


def _route_rows(v, tri, su, run):
    tm = v.shape[0]
    lane_i = lax.broadcasted_iota(I32, (tm, LANES), 1)
    lane = lane_i.astype(F32)
    vals, idxs = [], []
    for _ in range(TOP_K):
        m = jnp.max(v, axis=-1, keepdims=True)
        idx = jnp.min(jnp.where(v == m, lane, float(LANES)), axis=-1, keepdims=True)
        vals.append(m)
        idxs.append(idx)
        v = jnp.where(lane == idx, NEG_BIG, v)
    ex = [jnp.exp(t - vals[0]) for t in vals]
    denom = ex[0] + ex[1] + ex[2] + ex[3]
    onehots = [lane == idx for idx in idxs]
    oh = jnp.zeros((tm, LANES), F32)
    for o in onehots:
        oh = oh + jnp.where(o, 1.0, 0.0)
    units = jnp.floor((jnp.sum(oh, axis=0, keepdims=True) + (RUN_ALIGN - 1.0)) * (1.0 / RUN_ALIGN))
    lstart = RUN_ALIGN * jnp.dot(jnp.broadcast_to(units, (8, LANES)).astype(BF16), su,
                                 preferred_element_type=F32)[0:1]
    pos = jnp.dot(tri, oh.astype(BF16), preferred_element_type=F32) + lstart
    lp_out = jnp.zeros((tm, LANES), I32)
    w_out = jnp.zeros((tm, LANES), F32)
    for kk in range(TOP_K):
        lp = jnp.sum(jnp.where(onehots[kk], pos, 0.0), axis=-1, keepdims=True)
        lp_out = jnp.where(lane_i == kk, lp.astype(I32), lp_out)
        w_out = jnp.where(lane_i == kk, ex[kk] / denom, w_out)
    sub = lax.broadcasted_iota(I32, (8, LANES), 0)
    tab = jnp.where(sub == 0, lstart, jnp.where(sub == 1, units, jnp.where(sub == 2, run, 0.0)))
    return lp_out[:, :TOP_K], w_out[:, :TOP_K], tab.astype(I32), run + RUN_ALIGN * units


def _run_copies(fn, tabs, base, make_copy, bits=RUN_BITS):
    ls_ref, ds_ref, nu_ref = tabs

    def per_expert(e, c):
        loc = 0 if ls_ref is None else ls_ref[base + e]
        dst = ds_ref[base + e]
        units = nu_ref[base + e]
        for level, bit in enumerate(bits):
            rows = bit * RUN_ALIGN
            take = (units & bit) != 0

            @pl.when(take)
            def _():
                src = 0 if ls_ref is None else pl.multiple_of(loc, RUN_ALIGN)
                fn(make_copy(src, pl.multiple_of(dst, RUN_ALIGN), rows, level))

            step = jnp.where(take, rows, 0)
            if ls_ref is not None:
                loc = loc + step
            dst = dst + step
        return c

    lax.fori_loop(0, N_EXPERTS, per_expert, 0)


_TN = (((0,), (0,)), ((), ()))


def _slot_matrix(lpos, values):
    tm = lpos.shape[0]
    col = lax.broadcasted_iota(I32, (tm, LOCAL_ROWS), 1)
    out = jnp.zeros((tm, LOCAL_ROWS), F32)
    for kk in range(TOP_K):
        out = jnp.where(col == lpos[:, kk:kk + 1], values[kk], out)
    return out


def _start(cp):
    cp.start()


def _wait(cp):
    cp.wait()


def _table_base(blk):
    return (blk % TABLE_BLOCKS) * N_EXPERTS


def _fill_copies(units, make_copy):
    off = 0
    for level, bit in enumerate(RUN_BITS):
        rows = bit * RUN_ALIGN
        take = (units & bit) != 0

        @pl.when(take)
        def _():
            make_copy(off, rows, level).start()

        off = off + jnp.where(take, rows, 0)


def _dispatch_body(ls_ref, ds_ref, nu_ref, fu_ref, tds_ref, tnu_ref,
                   h_ref, lpos_ref, xs_hbm, loc_ref, zero_ref, sem, tail_sems):
    blk = pl.program_id(0)
    last = pl.num_programs(0) - 1
    slot = blk % 2
    dump = xs_hbm.shape[0] - DUMP_ROWS
    lo, hi = _unpack_pairs(h_ref[...])
    sel = _slot_matrix(lpos_ref[...], [1.0] * TOP_K).astype(BF16)
    x_lo = lax.dot_general(sel, lo, _TN, preferred_element_type=F32)
    x_hi = lax.dot_general(sel, hi, _TN, preferred_element_type=F32)
    loc_ref[slot] = _pack_halves(x_lo, x_hi)

    def run_copy(loc, dst, rows, level):
        return pltpu.make_async_copy(loc_ref.at[slot, pl.ds(loc, rows)], xs_hbm.at[pl.ds(dst, rows)], sem)

    def fill_copy(off, rows, level):
        return pltpu.make_async_copy(loc_ref.at[slot, pl.ds(0, rows)],
                                     xs_hbm.at[pl.ds(dump + DUMP_OFFSETS[level], rows)], sem)

    def block_wait(s):
        pltpu.make_async_copy(loc_ref.at[s], xs_hbm.at[pl.ds(0, LOCAL_ROWS)], sem).wait()

    def tail_copy(loc, dst, rows, level):
        return pltpu.make_async_copy(zero_ref.at[pl.ds(loc, rows)], xs_hbm.at[pl.ds(dst, rows)],
                                     tail_sems.at[level])

    @pl.when(blk > 0)
    def _():
        block_wait(1 - slot)

    _run_copies(_start, (ls_ref, ds_ref, nu_ref), _table_base(blk), run_copy)
    _fill_copies(fu_ref[blk % FILL_CHUNK], fill_copy)

    @pl.when(blk == last)
    def _():
        block_wait(slot)
        zero_ref[...] = jnp.zeros_like(zero_ref)
        tails = (None, tds_ref, tnu_ref)
        _run_copies(_start, tails, 0, tail_copy, TAIL_BITS)
        _run_copies(_wait, tails, 0, tail_copy, TAIL_BITS)


def _table_specs(shift, nblk):
    def block(i):
        return jnp.clip(i + shift, 0, nblk - 1)

    run = pl.BlockSpec((TABLE_BLOCKS * N_EXPERTS,), lambda i: (block(i) // TABLE_BLOCKS,),
                       memory_space=pltpu.SMEM)
    fill = pl.BlockSpec((FILL_CHUNK,), lambda i: (block(i) // FILL_CHUNK,), memory_space=pltpu.SMEM)
    return [run, run, run, fill]


def _dispatch_call(tabs, tail_tabs, h2p, lpos, n_rows):
    T = h2p.shape[0]
    whole = pl.BlockSpec((TABLE_BLOCKS * N_EXPERTS,), lambda i: (0,), memory_space=pltpu.SMEM)
    return pl.pallas_call(
        _dispatch_body,
        grid=(T // ROW_BLOCK,),
        in_specs=_table_specs(0, T // ROW_BLOCK) + [whole, whole] + [
            pl.BlockSpec((ROW_BLOCK, D_MODEL // 2), lambda i: (i, 0)),
            pl.BlockSpec((ROW_BLOCK, TOP_K), lambda i: (i, 0))],
        out_specs=pl.BlockSpec(memory_space=pl.ANY),
        scratch_shapes=[pltpu.VMEM((2, LOCAL_ROWS, D_MODEL // 2), U32),
                        pltpu.VMEM((TAIL_BITS[0] * RUN_ALIGN, D_MODEL // 2), U32),
                        pltpu.SemaphoreType.DMA(()),
                        pltpu.SemaphoreType.DMA((len(TAIL_BITS),))],
        out_shape=jax.ShapeDtypeStruct((n_rows + DUMP_ROWS, D_MODEL // 2), U32),
        compiler_params=_cparams(("arbitrary",)),
        name="moe_dispatch",
    )(*tabs, *tail_tabs, h2p, lpos)


def _expert_body(te_ref, tv_ref, xs_ref, wgu_ref, bgu_ref, wd_ref, bd_ref, y_ref):
    i = pl.program_id(0)

    @pl.when(tv_ref[i] == 1)
    def _():
        lo, hi = _unpack_pairs(xs_ref[...])
        half = D_MODEL // 2
        gu = (jnp.dot(lo, wgu_ref[:half, :], preferred_element_type=F32)
              + jnp.dot(hi, wgu_ref[half:, :], preferred_element_type=F32) + bgu_ref[...])
        gate = jnp.minimum(gu[:, :D_EXPERT], SWIGLU_LIMIT)
        up = jnp.clip(gu[:, D_EXPERT:], -SWIGLU_LIMIT, SWIGLU_LIMIT)
        hid = (up + 1.0) * (gate * jax.nn.sigmoid(SWIGLU_ALPHA * gate))
        y = jnp.dot(hid.astype(BF16), wd_ref[...], preferred_element_type=F32) + bd_ref[...]
        y_ref[...] = _pack_pairs(y)

    @pl.when(tv_ref[i] == 0)
    def _():
        y_ref[...] = jnp.zeros_like(y_ref)


def _expert_call(tile_e, tile_v, xs, p):
    tmx = EXPERT_TILE
    n_rows = tile_e.shape[0] * tmx
    by_e = lambda i, te, tv: (te[i], 0, 0)
    return pl.pallas_call(
        _expert_body,
        grid_spec=pltpu.PrefetchScalarGridSpec(
            num_scalar_prefetch=2,
            grid=(n_rows // tmx,),
            in_specs=[pl.BlockSpec((tmx, D_MODEL // 2), lambda i, te, tv: (i * tv[i], 0)),
                      pl.BlockSpec((None, D_MODEL, 2 * D_EXPERT), by_e),
                      pl.BlockSpec((None, 1, 2 * D_EXPERT), by_e),
                      pl.BlockSpec((None, D_EXPERT, D_MODEL), by_e),
                      pl.BlockSpec((None, 1, D_MODEL), by_e)],
            out_specs=pl.BlockSpec((tmx, D_MODEL // 2), lambda i, te, tv: (i, 0))),
        out_shape=jax.ShapeDtypeStruct((n_rows, D_MODEL // 2), U32),
        compiler_params=_cparams(("arbitrary",)),
        name="moe_experts",
    )(tile_e, tile_v, xs, p["w_gate_up"], p["b_gate_up"], p["w_down"], p["b_down"])


def _combine_body(ls_ref, ds_ref, nu_ref, fu_ref, nls_ref, nds_ref, nnu_ref, nfu_ref,
                  x1_ref, lpos_ref, w_ref, ga2_ref, ys_hbm, o_ref, loc_ref, sems):
    blk = pl.program_id(0)
    last = pl.num_programs(0) - 1
    slot = blk % 2

    def fetch(tabs, fill_ref, b, s):
        def run_copy(loc, src, rows, level):
            return pltpu.make_async_copy(ys_hbm.at[pl.ds(src, rows)], loc_ref.at[s, pl.ds(loc, rows)],
                                         sems.at[s])

        units = fill_ref[b % FILL_CHUNK]
        used = LOCAL_ROWS - units * RUN_ALIGN

        def fill_copy(off, rows, level):
            return pltpu.make_async_copy(ys_hbm.at[pl.ds(0, rows)],
                                         loc_ref.at[s, pl.ds(pl.multiple_of(used + off, RUN_ALIGN), rows)],
                                         sems.at[s])

        _run_copies(_start, tabs, _table_base(b), run_copy)
        _fill_copies(units, fill_copy)

    @pl.when(blk == 0)
    def _():
        fetch((ls_ref, ds_ref, nu_ref), fu_ref, blk, slot)

    @pl.when(blk < last)
    def _():
        fetch((nls_ref, nds_ref, nnu_ref), nfu_ref, blk + 1, 1 - slot)

    pltpu.make_async_copy(ys_hbm.at[pl.ds(0, LOCAL_ROWS)], loc_ref.at[slot], sems.at[slot]).wait()

    y_lo, y_hi = _unpack_pairs(loc_ref[slot])
    w = w_ref[...]
    mix = _slot_matrix(lpos_ref[...], [w[:, kk:kk + 1] for kk in range(TOP_K)])
    mix_hi = mix.astype(BF16)
    mix_lo = (mix - mix_hi.astype(F32)).astype(BF16)
    halves = [jnp.dot(mix_hi, y, preferred_element_type=F32) + jnp.dot(mix_lo, y, preferred_element_type=F32)
              for y in (y_lo, y_hi)]
    o_ref[...] = x1_ref[...] + ga2_ref[0] * jnp.concatenate(halves, axis=1)


def _combine_call(tabs, x1, lpos, wts, ga2, ys, B, S):
    T = B * S
    nb = S // ROW_BLOCK
    row = lambda i: (i, 0)
    return pl.pallas_call(
        _combine_body,
        grid=(T // ROW_BLOCK,),
        in_specs=_table_specs(0, T // ROW_BLOCK) + _table_specs(1, T // ROW_BLOCK) + [
            pl.BlockSpec((ROW_BLOCK, D_MODEL), row),
            pl.BlockSpec((ROW_BLOCK, TOP_K), row),
            pl.BlockSpec((ROW_BLOCK, TOP_K), row),
            pl.BlockSpec((1, 1, D_MODEL), lambda i: (i // nb, 0, 0)),
            pl.BlockSpec(memory_space=pl.ANY)],
        out_specs=pl.BlockSpec((ROW_BLOCK, D_MODEL), row),
        scratch_shapes=[pltpu.VMEM((2, LOCAL_ROWS, D_MODEL // 2), U32),
                        pltpu.SemaphoreType.DMA((2,))],
        out_shape=jax.ShapeDtypeStruct((T, D_MODEL), F32),
        compiler_params=_cparams(("arbitrary",)),
        name="moe_combine",
    )(*tabs, *tabs, x1, lpos, wts, ga2, ys)


def _pad_heads(w, per_head, n_heads):
    k = w.shape[0]
    w = w.reshape(k, n_heads, per_head)
    w = jnp.pad(w, ((0, 0), (0, 0), (0, HEAD_PAD - per_head)))
    return w.reshape(k, n_heads * HEAD_PAD)


def _prepare(w_in, g_norm1, g_q_lat, w_uq, g_kv_lat, w_ukv, g_qk_q, g_qk_k, w_o_mla, w_o_sb,
             w_out, g_norm2, w_router, b_router, w_gate_up, b_gate_up, w_down, b_down):
    c0 = Q_LORA
    c1 = c0 + KV_LORA
    c2 = c1 + QK_ROPE
    sbw = SB_HEADS * SB_DIM
    c3 = c2 + 3 * sbw
    c4 = c3 + D_MODEL
    p = {}
    p["g_norm1"] = g_norm1.reshape(1, D_MODEL)
    p["g_norm2"] = g_norm2.reshape(1, D_MODEL)
    p["w_ql"] = w_in[:, :c0].astype(BF16)
    p["w_kvl"] = w_in[:, c0:c1].astype(BF16)
    p["w_kpe"] = jnp.pad(w_in[:, c1:c2], ((0, 0), (QK_NOPE, HEAD_PAD - QK_DIM))).astype(BF16)
    p["w_sq"] = w_in[:, c2:c2 + sbw].astype(BF16)
    p["w_sk"] = w_in[:, c2 + sbw:c2 + 2 * sbw].astype(BF16)
    p["w_sv"] = w_in[:, c2 + 2 * sbw:c3].astype(BF16)
    p["w_ga"] = w_in[:, c3:c4].astype(BF16)
    p["w_gb"] = w_in[:, c4:].astype(BF16)
    p["g_q_lat"] = g_q_lat.reshape(1, Q_LORA)
    p["g_kv_lat"] = g_kv_lat.reshape(1, KV_LORA)
    p["w_uq"] = _pad_heads(w_uq, QK_DIM, MLA_HEADS).astype(BF16)
    kv = w_ukv.reshape(KV_LORA, MLA_HEADS, QK_NOPE + V_DIM)
    p["w_uk"] = _pad_heads(kv[:, :, :QK_NOPE].reshape(KV_LORA, MLA_HEADS * QK_NOPE),
                           QK_NOPE, MLA_HEADS).astype(BF16)
    p["w_v"] = kv[:, :, QK_NOPE:].reshape(KV_LORA, MLA_HEADS * V_DIM).astype(BF16)
    p["g_qk_q"] = jnp.pad(g_qk_q, (0, HEAD_PAD - QK_DIM)).reshape(1, HEAD_PAD)
    p["g_qk_k"] = jnp.pad(g_qk_k, (0, HEAD_PAD - QK_DIM)).reshape(1, HEAD_PAD)
    inv_freq = 1.0 / (ROPE_THETA ** (jnp.arange(0, QK_ROPE, 2, dtype=F32) / QK_ROPE))
    p["invf"] = jnp.concatenate([jnp.zeros((QK_NOPE,), F32), inv_freq, inv_freq,
                                 jnp.zeros((HEAD_PAD - QK_DIM,), F32)]).reshape(1, HEAD_PAD)
    p["w_o_mla"] = w_o_mla.astype(BF16)
    p["w_o_sb"] = w_o_sb.astype(BF16)
    p["w_out"] = w_out.astype(BF16)
    wr = jnp.pad(w_router, ((0, 0), (0, LANES - N_EXPERTS)))
    p["w_r_hi"] = wr.astype(BF16)
    p["w_r_lo"] = (wr - p["w_r_hi"].astype(F32)).astype(BF16)
    p["b_r"] = jnp.concatenate([b_router, jnp.full((LANES - N_EXPERTS,), NEG_BIG, F32)]).reshape(1, LANES)
    p["w_gate_up"] = w_gate_up.astype(BF16)
    p["b_gate_up"] = b_gate_up.reshape(N_EXPERTS, 1, 2 * D_EXPERT)
    p["w_down"] = w_down.astype(BF16)
    p["b_down"] = b_down.reshape(N_EXPERTS, 1, D_MODEL)
    return p


def _layer(x2, c, posf, B, S, w_ada, b_ada, *layer_weights):
    T = B * S
    p = _prepare(*layer_weights)
    mod = _ada_call(c, w_ada, b_ada)
    sh1, sc1, ga1, sh2, sc2, ga2 = [m.reshape(B, 1, D_MODEL) for m in jnp.split(mod, 6, axis=-1)]

    q, k, v, sq, sk, sv = _proj_call(x2, sh1, sc1, posf, p, B, S)
    o_mla = _mla_call(q, k, v, B, S)
    o_sb = _sb_call(sq, sk, sv, B, S)
    x1, h2p, lpos, wts, tab, totals = _merge_call(x2, (sh1, sc1, ga1, sh2, sc2), o_mla, o_sb, p, B, S)

    nblk = T // ROW_BLOCK
    rows_e = totals[0, :N_EXPERTS].astype(I32)
    tiles_e = (rows_e + EXPERT_TILE - 1) // EXPERT_TILE
    tile_end = jnp.cumsum(tiles_e)
    region = ((tile_end - tiles_e) * EXPERT_TILE).astype(I32)
    n_tiles = (T * TOP_K + nblk * N_EXPERTS * (RUN_ALIGN - 1)) // EXPERT_TILE + N_EXPERTS
    tile_ids = jnp.arange(n_tiles, dtype=I32)
    tile_e = jnp.minimum(jnp.sum((tile_ids[:, None] >= tile_end[None, :]).astype(I32), axis=1),
                         N_EXPERTS - 1).astype(I32)
    tile_v = (tile_ids < tile_end[-1]).astype(I32)
    pad = (-nblk) % TABLE_BLOCKS

    def flat(t):
        return jnp.pad(t, ((0, pad), (0, 0))).reshape(-1)

    units = tab[:, 1, :N_EXPERTS]
    fill_units = LOCAL_ROWS // RUN_ALIGN - jnp.sum(units, axis=1)
    tabs = (flat(tab[:, 0, :N_EXPERTS]), flat(tab[:, 2, :N_EXPERTS] + region[None, :]), flat(units),
            jnp.pad(fill_units, (0, (-nblk) % FILL_CHUNK)))
    table_len = TABLE_BLOCKS * N_EXPERTS
    tail_tabs = tuple(jnp.pad(t, (0, table_len - N_EXPERTS))
                      for t in (region + rows_e, (tiles_e * EXPERT_TILE - rows_e) // RUN_ALIGN))

    xs = _dispatch_call(tabs, tail_tabs, h2p, lpos, n_tiles * EXPERT_TILE)
    ys = _expert_call(tile_e, tile_v, xs, p)
    return _combine_call(tabs, x1, lpos, wts, ga2, ys, B, S)


def kernel(x, c, positions, w_ada, b_ada, g_norm1, w_in, g_q_lat, w_uq, g_kv_lat, w_ukv, g_qk_q,
           g_qk_k, w_o_mla, w_o_sb, w_out, g_norm2, w_router, b_router, w_gate_up, b_gate_up,
           w_down, b_down):
    B, S, D = x.shape
    x2 = x.reshape(B * S, D)
    posf = positions.astype(F32).reshape(B * S, 1)
    for l in range(w_ada.shape[0]):
        x2 = _layer(x2, c, posf, B, S, w_ada[l], b_ada[l], w_in[l], g_norm1[l], g_q_lat[l],
                    w_uq[l], g_kv_lat[l], w_ukv[l], g_qk_q[l], g_qk_k[l], w_o_mla[l], w_o_sb[l],
                    w_out[l], g_norm2[l], w_router[l], b_router[l], w_gate_up[l], b_gate_up[l],
                    w_down[l], b_down[l])
    return x2.reshape(B, S, D)
```

```python
import functools

import jax
import jax.numpy as jnp
import numpy as np
from jax import lax
from jax.experimental import pallas as pl
from jax.experimental.pallas import tpu as pltpu

F32 = jnp.float32
BF16 = jnp.bfloat16
I32 = jnp.int32
U32 = jnp.uint32

D_MODEL = 1024
EPS = 1e-6
CHUNK = 64
MLA_HEADS = 8
Q_LORA = 384
KV_LORA = 256
QK_NOPE = 64
QK_ROPE = 32
V_DIM = 64
QK_DIM = QK_NOPE + QK_ROPE
ROPE_THETA = 10000.0
SB_HEADS = 8
SB_DIM = 64
N_EXPERTS = 32
TOP_K = 4
D_EXPERT = D_MODEL
SWIGLU_LIMIT = 7.0
SWIGLU_ALPHA = 1.702

LANES = 128
HEAD_PAD = LANES
TOK_BLOCK = 1024
SUB_ROWS = 256
ATT_BLOCK = 256
ROW_BLOCK = 256
EXPERT_TILE = 512
RUN_ALIGN = 8
RUN_BITS = tuple(2 ** b for b in reversed(range((ROW_BLOCK // RUN_ALIGN).bit_length())))
LOCAL_ROWS = ROW_BLOCK * TOP_K + N_EXPERTS * RUN_ALIGN
TABLE_BLOCKS = 32
FILL_CHUNK = TABLE_BLOCKS * N_EXPERTS
DUMP_OFFSETS = tuple(RUN_ALIGN * sum(RUN_BITS[:l]) for l in range(len(RUN_BITS)))
DUMP_ROWS = RUN_ALIGN * 2 * RUN_BITS[0]
TAIL_BITS = tuple(2 ** b for b in reversed(range((EXPERT_TILE // RUN_ALIGN - 1).bit_length())))
NEG_BIG = -1e30
LOG2E = 1.4426950408889634
MLA_LOGIT_SCALE = QK_DIM ** -0.5 * LOG2E
SB_LOGIT_SCALE = SB_DIM ** -0.5 * LOG2E
VMEM_LIMIT = 48 * 1024 * 1024


def _cparams(sem):
    return pltpu.CompilerParams(dimension_semantics=sem, vmem_limit_bytes=VMEM_LIMIT)


def _ada_body(c_ref, w_ref, b_ref, o_ref):
    c = c_ref[...]
    s = c * jax.nn.sigmoid(c)
    o_ref[...] = jnp.dot(s, w_ref[...], preferred_element_type=F32,
                         precision=lax.Precision.HIGHEST) + b_ref[...]


def _ada_call(c, w_ada, b_ada):
    B = c.shape[0]
    n = w_ada.shape[1]
    bn = 512
    return pl.pallas_call(
        _ada_body,
        grid=(n // bn,),
        in_specs=[pl.BlockSpec((B, D_MODEL), lambda j: (0, 0)),
                  pl.BlockSpec((D_MODEL, bn), lambda j: (0, j)),
                  pl.BlockSpec((1, bn), lambda j: (0, j))],
        out_specs=pl.BlockSpec((B, bn), lambda j: (0, j)),
        out_shape=jax.ShapeDtypeStruct((B, n), F32),
        compiler_params=_cparams(("arbitrary",)),
        name="ada_mod",
    )(c, w_ada, b_ada.reshape(1, n))


def _rms(v, width):
    return lax.rsqrt(jnp.sum(v * v, axis=-1, keepdims=True) * (1.0 / width) + EPS)


def _modulated_norm(x, g, sh, sc):
    h = x * _rms(x, D_MODEL) * g
    return h * (1.0 + sc) + sh


def _proj_body(x_ref, sh_ref, sc_ref, pos_ref, gn_ref, wql_ref, wkvl_ref, wkpe_ref,
               wsq_ref, wsk_ref, wsv_ref, gql_ref, wuq_ref, gkvl_ref, wuk_ref, wv_ref,
               gq_ref, gk_ref, invf_ref,
               q_ref, k_ref, v_ref, sq_ref, sk_ref, sv_ref):
    for r in range(x_ref.shape[0] // SUB_ROWS):
        rows = slice(r * SUB_ROWS, (r + 1) * SUB_ROWS)
        x = x_ref[rows, :]
        h = _modulated_norm(x, gn_ref[...], sh_ref[0], sc_ref[0]).astype(BF16)

        sq_ref[rows, :] = (jnp.dot(h, wsq_ref[...], preferred_element_type=F32) * SB_LOGIT_SCALE).astype(BF16)
        sk_ref[rows, :] = jnp.dot(h, wsk_ref[...], preferred_element_type=F32).astype(BF16)
        sv_ref[rows, :] = jnp.dot(h, wsv_ref[...], preferred_element_type=F32).astype(BF16)

        q_lat = jnp.dot(h, wql_ref[...], preferred_element_type=F32)
        kv_lat = jnp.dot(h, wkvl_ref[...], preferred_element_type=F32)
        kpe = jnp.dot(h, wkpe_ref[...], preferred_element_type=F32)

        qn = (q_lat * _rms(q_lat, Q_LORA) * gql_ref[...]).astype(BF16)
        kvn = (kv_lat * _rms(kv_lat, KV_LORA) * gkvl_ref[...]).astype(BF16)
        q = jnp.dot(qn, wuq_ref[...], preferred_element_type=F32)
        kn = jnp.dot(kvn, wuk_ref[...], preferred_element_type=F32)
        v_ref[rows, :] = jnp.dot(kvn, wv_ref[...], preferred_element_type=F32).astype(BF16)

        lane = lax.broadcasted_iota(I32, (SUB_ROWS, LANES), 1)
        ang = pos_ref[rows, :] * invf_ref[...]
        cosf = jnp.cos(ang)
        sinf = jnp.sin(ang)
        half = QK_ROPE // 2
        s_lo = jnp.where((lane >= QK_NOPE) & (lane < QK_NOPE + half), -sinf, 0.0)
        s_hi = jnp.where((lane >= QK_NOPE + half) & (lane < QK_DIM), sinf, 0.0)

        def rope(t):
            return (t * cosf + pltpu.roll(t, LANES - half, 1) * s_lo
                    + pltpu.roll(t, half, 1) * s_hi)

        gq = gq_ref[...] * MLA_LOGIT_SCALE
        gk = gk_ref[...]
        kpe_sq = jnp.sum(kpe * kpe, axis=-1, keepdims=True)
        kr = rope(kpe * gk)
        for hh in range(MLA_HEADS):
            sl = slice(hh * HEAD_PAD, (hh + 1) * HEAD_PAD)
            qh = q[:, sl]
            q_ref[rows, sl] = rope(qh * _rms(qh, QK_DIM) * gq).astype(BF16)
            kh = kn[:, sl]
            rk = lax.rsqrt((jnp.sum(kh * kh, axis=-1, keepdims=True) + kpe_sq) * (1.0 / QK_DIM) + EPS)
            k_ref[rows, sl] = ((kh * gk + kr) * rk).astype(BF16)


def _proj_call(x2, sh1, sc1, posf, p, B, S):
    T = B * S
    tm = TOK_BLOCK
    nb = S // tm
    row = lambda i: (i, 0)
    per_b = lambda i: (i // nb, 0, 0)
    full = lambda i: (0, 0)

    def wspec(a):
        return pl.BlockSpec(a.shape, full)

    weights = [p["g_norm1"], p["w_ql"], p["w_kvl"], p["w_kpe"], p["w_sq"], p["w_sk"], p["w_sv"],
               p["g_q_lat"], p["w_uq"], p["g_kv_lat"], p["w_uk"], p["w_v"], p["g_qk_q"],
               p["g_qk_k"], p["invf"]]
    out_w = [MLA_HEADS * HEAD_PAD, MLA_HEADS * HEAD_PAD, MLA_HEADS * V_DIM,
             SB_HEADS * SB_DIM, SB_HEADS * SB_DIM, SB_HEADS * SB_DIM]
    return pl.pallas_call(
        _proj_body,
        grid=(T // tm,),
        in_specs=[pl.BlockSpec((tm, D_MODEL), row),
                  pl.BlockSpec((1, 1, D_MODEL), per_b),
                  pl.BlockSpec((1, 1, D_MODEL), per_b),
                  pl.BlockSpec((tm, 1), row)] + [wspec(a) for a in weights],
        out_specs=[pl.BlockSpec((tm, w), row) for w in out_w],
        out_shape=[jax.ShapeDtypeStruct((T, w), BF16) for w in out_w],
        compiler_params=_cparams(("arbitrary",)),
        name="mix_proj",
    )(x2, sh1, sc1, posf, *weights)


_NT = (((1,), (1,)), ((), ()))


def _mla_rows(qh, k_ref, v_ref, sl, i, tq):
    kend = (i + 1) * tq
    s = lax.dot_general(qh, k_ref[0:kend, sl], _NT, preferred_element_type=F32)
    row = lax.broadcasted_iota(I32, (tq, tq), 0)
    col = lax.broadcasted_iota(I32, (tq, tq), 1)
    s_diag = jnp.where((col // CHUNK) <= (row // CHUNK), s[:, kend - tq:], NEG_BIG)
    s = s_diag if i == 0 else jnp.concatenate([s[:, :kend - tq], s_diag], axis=1)
    pr = jnp.exp2(s - jnp.max(s, axis=-1, keepdims=True))
    l = jnp.sum(pr, axis=-1, keepdims=True)
    return jnp.dot(pr.astype(BF16), v_ref[0:kend, :], preferred_element_type=F32) / l


def _mla_body(q_ref, k_ref, v_ref, o_ref):
    tq = ATT_BLOCK
    lane = lax.broadcasted_iota(I32, (tq, LANES), 1)
    for ii in range(q_ref.shape[0] // tq):
        rows = slice(ii * tq, (ii + 1) * tq)
        outs = []
        for hh in range(2):
            sl = slice(hh * HEAD_PAD, (hh + 1) * HEAD_PAD)
            outs.append(_mla_rows(q_ref[rows, sl], k_ref, v_ref, sl, ii, tq))
        o_ref[rows, :] = jnp.where(lane < V_DIM, outs[0], outs[1]).astype(BF16)


def _mla_call(q, k, v, B, S):
    q3 = q.reshape(B, S, MLA_HEADS * HEAD_PAD)
    k3 = k.reshape(B, S, MLA_HEADS * HEAD_PAD)
    v3 = v.reshape(B, S, MLA_HEADS * V_DIM)
    pair = lambda b, hp: (b, 0, hp)
    out = pl.pallas_call(
        _mla_body,
        grid=(B, MLA_HEADS // 2),
        in_specs=[pl.BlockSpec((None, S, 2 * HEAD_PAD), pair),
                  pl.BlockSpec((None, S, 2 * HEAD_PAD), pair),
                  pl.BlockSpec((None, S, 2 * V_DIM), pair)],
        out_specs=pl.BlockSpec((None, S, 2 * V_DIM), pair),
        out_shape=jax.ShapeDtypeStruct((B, S, MLA_HEADS * V_DIM), BF16),
        compiler_params=_cparams(("arbitrary", "arbitrary")),
        name="mla_attn",
    )(q3, k3, v3)
    return out.reshape(B * S, MLA_HEADS * V_DIM)


def _sb_rows(qh, k_ref, v_ref, i, tq):
    kend = (i + 1) * tq
    z = lax.dot_general(qh, k_ref[0:kend, :], _NT, preferred_element_type=F32)
    sp = jnp.maximum(z, 0.0) + jnp.log2(1.0 + jnp.exp2(jnp.minimum(z, -z)))
    row = lax.broadcasted_iota(I32, (tq, tq), 0)
    col = lax.broadcasted_iota(I32, (tq, tq), 1)
    strict = col < row
    suffix_ones = jnp.where(row >= col, 1.0, 0.0).astype(BF16)
    later = jnp.zeros((tq, 1), F32)
    a_blocks = [None] * (i + 1)
    for j in range(i, -1, -1):
        cols = slice(j * tq, (j + 1) * tq)
        spj = sp[:, cols]
        if j == i:
            spj = jnp.where(strict, spj, 0.0)
        suf = jnp.dot(spj.astype(BF16), suffix_ones, preferred_element_type=F32)
        a = jnp.exp2(z[:, cols] - suf - later)
        if j == i:
            a = jnp.where(strict, a, 0.0)
        a_blocks[j] = a.astype(BF16)
        later = later + suf[:, 0:1]
    a = a_blocks[0] if i == 0 else jnp.concatenate(a_blocks, axis=1)
    return jnp.dot(a, v_ref[0:kend, :], preferred_element_type=F32)


def _sb_body(q_ref, k_ref, v_ref, o_ref):
    tq = ATT_BLOCK
    lane = lax.broadcasted_iota(I32, (tq, LANES), 1)
    for ii in range(q_ref.shape[0] // tq):
        rows = slice(ii * tq, (ii + 1) * tq)
        q = q_ref[rows, :]
        outs = []
        for hh in range(2):
            in_head = (lane >= hh * SB_DIM) & (lane < (hh + 1) * SB_DIM)
            qh = jnp.where(in_head, q, jnp.zeros_like(q))
            outs.append(_sb_rows(qh, k_ref, v_ref, ii, tq))
        o_ref[rows, :] = jnp.where(lane < SB_DIM, outs[0], outs[1]).astype(BF16)


def _sb_call(sq, sk, sv, B, S):
    w = SB_HEADS * SB_DIM
    args = [a.reshape(B, S, w) for a in (sq, sk, sv)]
    pair = pl.BlockSpec((None, S, 2 * SB_DIM), lambda b, hp: (b, 0, hp))
    out = pl.pallas_call(
        _sb_body,
        grid=(B, SB_HEADS // 2),
        in_specs=[pair, pair, pair],
        out_specs=pair,
        out_shape=jax.ShapeDtypeStruct((B, S, w), BF16),
        compiler_params=_cparams(("arbitrary", "arbitrary")),
        name="sb_attn",
    )(*args)
    return out.reshape(B * S, w)


def _pack_pairs(h):
    n = h.shape[1] // 2
    return _pack_halves(h[:, :n], h[:, n:])


def _pack_halves(lo, hi):
    lo = pltpu.bitcast(lo.astype(BF16).astype(F32), U32)
    hi = pltpu.bitcast(hi.astype(BF16).astype(F32), U32)
    return (lo >> 16) | (hi & jnp.uint32(0xFFFF0000))


def _unpack_pairs(w):
    lo = pltpu.bitcast(w << 16, F32).astype(BF16)
    hi = pltpu.bitcast(w & jnp.uint32(0xFFFF0000), F32).astype(BF16)
    return lo, hi


def _merge_body(x_ref, sh1_ref, sc1_ref, ga1_ref, sh2_ref, sc2_ref, om_ref, os_ref,
                gn1_ref, wga_ref, wgb_ref, wom_ref, wos_ref, wout_ref, gn2_ref,
                wrh_ref, wrl_ref, br_ref, tri_ref, su_ref,
                x1_ref, h2_ref, lpos_ref, w_ref, tab_ref, tot_ref, run_ref):
    @pl.when(pl.program_id(0) == 0)
    def _():
        run_ref[...] = jnp.zeros_like(run_ref)

    groups = x_ref.shape[0] // SUB_ROWS
    logits = []
    for r in range(groups):
        rows = slice(r * SUB_ROWS, (r + 1) * SUB_ROWS)
        x = x_ref[rows, :]
        h = _modulated_norm(x, gn1_ref[...], sh1_ref[0], sc1_ref[0]).astype(BF16)
        ga = jax.nn.sigmoid(jnp.dot(h, wga_ref[...], preferred_element_type=F32))
        gb = jax.nn.sigmoid(jnp.dot(h, wgb_ref[...], preferred_element_type=F32))
        merged = (ga * jnp.dot(om_ref[rows, :], wom_ref[...], preferred_element_type=F32)
                  + gb * jnp.dot(os_ref[rows, :], wos_ref[...], preferred_element_type=F32))
        y = jnp.dot(merged.astype(BF16), wout_ref[...], preferred_element_type=F32)
        x1 = x + ga1_ref[0] * y
        x1_ref[rows, :] = x1
        h2 = _modulated_norm(x1, gn2_ref[...], sh2_ref[0], sc2_ref[0])
        h2_ref[rows, :] = _pack_pairs(h2)
        h2_hi = h2.astype(BF16)
        h2_lo = (h2 - h2_hi.astype(F32)).astype(BF16)
        logits.append(jnp.dot(h2_hi, wrh_ref[...], preferred_element_type=F32)
                      + jnp.dot(h2_hi, wrl_ref[...], preferred_element_type=F32)
                      + jnp.dot(h2_lo, wrh_ref[...], preferred_element_type=F32)
                      + br_ref[...])
    lpos, wts, tabs, run = _route_rows(jnp.concatenate(logits, axis=0), tri_ref[...], su_ref[...],
                                       run_ref[...])
    lpos_ref[...] = lpos
    w_ref[...] = wts
    for r in range(groups):
        tab_ref[r] = tabs[r]
    run_ref[...] = run
    tot_ref[...] = run


def _merge_call(x2, mods, o_mla, o_sb, p, B, S):
    assert SUB_ROWS == ROW_BLOCK
    T = B * S
    tm = TOK_BLOCK
    nb = S // tm
    per_step = tm // ROW_BLOCK
    row = lambda i: (i, 0)
    per_b = lambda i: (i // nb, 0, 0)
    full = lambda i: (0, 0)
    tri = jnp.asarray(np.arange(ROW_BLOCK)[:, None] > np.arange(ROW_BLOCK)[None, :], dtype=BF16)
    su = jnp.asarray(np.arange(LANES)[:, None] < np.arange(LANES)[None, :], dtype=BF16)
    weights = [p["g_norm1"], p["w_ga"], p["w_gb"], p["w_o_mla"], p["w_o_sb"], p["w_out"],
               p["g_norm2"], p["w_r_hi"], p["w_r_lo"], p["b_r"], tri, su]
    return pl.pallas_call(
        _merge_body,
        grid=(T // tm,),
        in_specs=[pl.BlockSpec((tm, D_MODEL), row)]
        + [pl.BlockSpec((1, 1, D_MODEL), per_b)] * 5
        + [pl.BlockSpec((tm, MLA_HEADS * V_DIM), row), pl.BlockSpec((tm, SB_HEADS * SB_DIM), row)]
        + [pl.BlockSpec(a.shape, full) for a in weights],
        out_specs=[pl.BlockSpec((tm, D_MODEL), row), pl.BlockSpec((tm, D_MODEL // 2), row),
                   pl.BlockSpec((tm, TOP_K), row), pl.BlockSpec((tm, TOP_K), row),
                   pl.BlockSpec((per_step, 8, LANES), lambda i: (i, 0, 0)),
                   pl.BlockSpec((1, LANES), full)],
        out_shape=[jax.ShapeDtypeStruct((T, D_MODEL), F32),
                   jax.ShapeDtypeStruct((T, D_MODEL // 2), U32),
                   jax.ShapeDtypeStruct((T, TOP_K), I32),
                   jax.ShapeDtypeStruct((T, TOP_K), F32),
                   jax.ShapeDtypeStruct((T // ROW_BLOCK, 8, LANES), I32),
                   jax.ShapeDtypeStruct((1, LANES), F32)],
        scratch_shapes=[pltpu.VMEM((1, LANES), F32)],
        compiler_params=_cparams(("arbitrary",)),
        name="merge_route",
    )(x2, *mods, o_mla, o_sb, *weights)


def _route_rows(v, tri, su, run):
    tm = v.shape[0]
    lane_i = lax.broadcasted_iota(I32, (tm, LANES), 1)
    lane = lane_i.astype(F32)
    vals, idxs = [], []
    for _ in range(TOP_K):
        m = jnp.max(v, axis=-1, keepdims=True)
        idx = jnp.min(jnp.where(v == m, lane, float(LANES)), axis=-1, keepdims=True)
        vals.append(m)
        idxs.append(idx)
        v = jnp.where(lane == idx, NEG_BIG, v)
    ex = [jnp.exp(t - vals[0]) for t in vals]
    denom = ex[0] + ex[1] + ex[2] + ex[3]
    onehots = [lane == idx for idx in idxs]
    oh = jnp.zeros((tm, LANES), F32)
    for o in onehots:
        oh = oh + jnp.where(o, 1.0, 0.0)
    sub = lax.broadcasted_iota(I32, (8, LANES), 0)
    pos, tabs = [], []
    for b in range(tm // ROW_BLOCK):
        oh_b = oh[b * ROW_BLOCK:(b + 1) * ROW_BLOCK]
        units = jnp.floor((jnp.sum(oh_b, axis=0, keepdims=True) + (RUN_ALIGN - 1.0)) * (1.0 / RUN_ALIGN))
        lstart = RUN_ALIGN * jnp.dot(jnp.broadcast_to(units, (8, LANES)).astype(BF16), su,
                                     preferred_element_type=F32)[0:1]
        pos.append(jnp.dot(tri, oh_b.astype(BF16), preferred_element_type=F32) + lstart)
        tab = jnp.where(sub == 0, lstart, jnp.where(sub == 1, units, jnp.where(sub == 2, run, 0.0)))
        tabs.append(tab.astype(I32))
        run = run + RUN_ALIGN * units
    pos = jnp.concatenate(pos, axis=0)
    lp_out = jnp.zeros((tm, LANES), I32)
    w_out = jnp.zeros((tm, LANES), F32)
    for kk in range(TOP_K):
        lp = jnp.sum(jnp.where(onehots[kk], pos, 0.0), axis=-1, keepdims=True)
        lp_out = jnp.where(lane_i == kk, lp.astype(I32), lp_out)
        w_out = jnp.where(lane_i == kk, ex[kk] / denom, w_out)
    return lp_out[:, :TOP_K], w_out[:, :TOP_K], tabs, run


def _run_copies(fn, tabs, base, make_copy, bits=RUN_BITS):
    ls_ref, ds_ref, nu_ref = tabs

    def per_expert(e, c):
        loc = 0 if ls_ref is None else ls_ref[base + e]
        dst = ds_ref[base + e]
        units = nu_ref[base + e]
        for level, bit in enumerate(bits):
            rows = bit * RUN_ALIGN
            take = (units & bit) != 0

            @pl.when(take)
            def _():
                src = 0 if ls_ref is None else pl.multiple_of(loc, RUN_ALIGN)
                fn(make_copy(src, pl.multiple_of(dst, RUN_ALIGN), rows, level))

            step = jnp.where(take, rows, 0)
            if ls_ref is not None:
                loc = loc + step
            dst = dst + step
        return c

    lax.fori_loop(0, N_EXPERTS, per_expert, 0)


_TN = (((0,), (0,)), ((), ()))


def _slot_matrix(lpos, values):
    tm = lpos.shape[0]
    col = lax.broadcasted_iota(I32, (tm, LOCAL_ROWS), 1)
    out = jnp.zeros((tm, LOCAL_ROWS), F32)
    for kk in range(TOP_K):
        out = jnp.where(col == lpos[:, kk:kk + 1], values[kk], out)
    return out


def _start(cp):
    cp.start()


def _wait(cp):
    cp.wait()


def _table_base(blk):
    return (blk % TABLE_BLOCKS) * N_EXPERTS


def _fill_copies(units, make_copy):
    off = 0
    for level, bit in enumerate(RUN_BITS):
        rows = bit * RUN_ALIGN
        take = (units & bit) != 0

        @pl.when(take)
        def _():
            make_copy(off, rows, level).start()

        off = off + jnp.where(take, rows, 0)


def _dispatch_body(ls_ref, ds_ref, nu_ref, fu_ref, tds_ref, tnu_ref,
                   h_ref, lpos_ref, xs_hbm, loc_ref, zero_ref, sem, tail_sems):
    blk = pl.program_id(0)
    last = pl.num_programs(0) - 1
    slot = blk % 2
    dump = xs_hbm.shape[0] - DUMP_ROWS
    lo, hi = _unpack_pairs(h_ref[...])
    sel = _slot_matrix(lpos_ref[...], [1.0] * TOP_K).astype(BF16)
    x_lo = lax.dot_general(sel, lo, _TN, preferred_element_type=F32)
    x_hi = lax.dot_general(sel, hi, _TN, preferred_element_type=F32)
    loc_ref[slot] = _pack_halves(x_lo, x_hi)

    def run_copy(loc, dst, rows, level):
        return pltpu.make_async_copy(loc_ref.at[slot, pl.ds(loc, rows)], xs_hbm.at[pl.ds(dst, rows)], sem)

    def fill_copy(off, rows, level):
        return pltpu.make_async_copy(loc_ref.at[slot, pl.ds(0, rows)],
                                     xs_hbm.at[pl.ds(dump + DUMP_OFFSETS[level], rows)], sem)

    def block_wait(s):
        pltpu.make_async_copy(loc_ref.at[s], xs_hbm.at[pl.ds(0, LOCAL_ROWS)], sem).wait()

    def tail_copy(loc, dst, rows, level):
        return pltpu.make_async_copy(zero_ref.at[pl.ds(loc, rows)], xs_hbm.at[pl.ds(dst, rows)],
                                     tail_sems.at[level])

    @pl.when(blk > 0)
    def _():
        block_wait(1 - slot)

    _run_copies(_start, (ls_ref, ds_ref, nu_ref), _table_base(blk), run_copy)
    _fill_copies(fu_ref[blk % FILL_CHUNK], fill_copy)

    @pl.when(blk == last)
    def _():
        block_wait(slot)
        zero_ref[...] = jnp.zeros_like(zero_ref)
        tails = (None, tds_ref, tnu_ref)
        _run_copies(_start, tails, 0, tail_copy, TAIL_BITS)
        _run_copies(_wait, tails, 0, tail_copy, TAIL_BITS)


def _table_specs(shift, nblk):
    def block(i):
        return jnp.clip(i + shift, 0, nblk - 1)

    run = pl.BlockSpec((TABLE_BLOCKS * N_EXPERTS,), lambda i: (block(i) // TABLE_BLOCKS,),
                       memory_space=pltpu.SMEM)
    fill = pl.BlockSpec((FILL_CHUNK,), lambda i: (block(i) // FILL_CHUNK,), memory_space=pltpu.SMEM)
    return [run, run, run, fill]


def _dispatch_call(tabs, tail_tabs, h2p, lpos, n_rows):
    T = h2p.shape[0]
    whole = pl.BlockSpec((TABLE_BLOCKS * N_EXPERTS,), lambda i: (0,), memory_space=pltpu.SMEM)
    return pl.pallas_call(
        _dispatch_body,
        grid=(T // ROW_BLOCK,),
        in_specs=_table_specs(0, T // ROW_BLOCK) + [whole, whole] + [
            pl.BlockSpec((ROW_BLOCK, D_MODEL // 2), lambda i: (i, 0)),
            pl.BlockSpec((ROW_BLOCK, TOP_K), lambda i: (i, 0))],
        out_specs=pl.BlockSpec(memory_space=pl.ANY),
        scratch_shapes=[pltpu.VMEM((2, LOCAL_ROWS, D_MODEL // 2), U32),
                        pltpu.VMEM((TAIL_BITS[0] * RUN_ALIGN, D_MODEL // 2), U32),
                        pltpu.SemaphoreType.DMA(()),
                        pltpu.SemaphoreType.DMA((len(TAIL_BITS),))],
        out_shape=jax.ShapeDtypeStruct((n_rows + DUMP_ROWS, D_MODEL // 2), U32),
        compiler_params=_cparams(("arbitrary",)),
        name="moe_dispatch",
    )(*tabs, *tail_tabs, h2p, lpos)


def _expert_body(te_ref, tv_ref, xs_ref, wgu_ref, bgu_ref, wd_ref, bd_ref, y_ref, wgu_bf, wd_bf):
    i = pl.program_id(0)

    @pl.when((i == 0) | (te_ref[i] != te_ref[jnp.maximum(i - 1, 0)]))
    def _():
        quarter = D_MODEL // 4
        for c in range(4):
            rows = slice(c * quarter, (c + 1) * quarter)
            wgu_bf[rows, :] = wgu_ref[rows, :].astype(BF16)
            wd_bf[rows, :] = wd_ref[rows, :].astype(BF16)

    @pl.when(tv_ref[i] == 1)
    def _():
        lo, hi = _unpack_pairs(xs_ref[...])
        half = D_MODEL // 2
        gu = (jnp.dot(lo, wgu_bf[:half, :], preferred_element_type=F32)
              + jnp.dot(hi, wgu_bf[half:, :], preferred_element_type=F32) + bgu_ref[...])
        gate = jnp.minimum(gu[:, :D_EXPERT], SWIGLU_LIMIT)
        up = jnp.clip(gu[:, D_EXPERT:], -SWIGLU_LIMIT, SWIGLU_LIMIT)
        hid = (up + 1.0) * (gate * jax.nn.sigmoid(SWIGLU_ALPHA * gate))
        y = jnp.dot(hid.astype(BF16), wd_bf[...], preferred_element_type=F32) + bd_ref[...]
        y_ref[...] = _pack_pairs(y)

    @pl.when(tv_ref[i] == 0)
    def _():
        y_ref[...] = jnp.zeros_like(y_ref)


def _expert_call(tile_e, tile_v, xs, p):
    tmx = EXPERT_TILE
    n_rows = tile_e.shape[0] * tmx
    by_e = lambda i, te, tv: (te[i], 0, 0)
    return pl.pallas_call(
        _expert_body,
        grid_spec=pltpu.PrefetchScalarGridSpec(
            num_scalar_prefetch=2,
            grid=(n_rows // tmx,),
            in_specs=[pl.BlockSpec((tmx, D_MODEL // 2), lambda i, te, tv: (i * tv[i], 0)),
                      pl.BlockSpec((None, D_MODEL, 2 * D_EXPERT), by_e),
                      pl.BlockSpec((None, 1, 2 * D_EXPERT), by_e),
                      pl.BlockSpec((None, D_EXPERT, D_MODEL), by_e),
                      pl.BlockSpec((None, 1, D_MODEL), by_e)],
            out_specs=pl.BlockSpec((tmx, D_MODEL // 2), lambda i, te, tv: (i, 0)),
            scratch_shapes=[pltpu.VMEM((D_MODEL, 2 * D_EXPERT), BF16),
                            pltpu.VMEM((D_EXPERT, D_MODEL), BF16)]),
        out_shape=jax.ShapeDtypeStruct((n_rows, D_MODEL // 2), U32),
        compiler_params=_cparams(("arbitrary",)),
        name="moe_experts",
    )(tile_e, tile_v, xs, p["w_gate_up"], p["b_gate_up"], p["w_down"], p["b_down"])


def _combine_body(ls_ref, ds_ref, nu_ref, fu_ref, nls_ref, nds_ref, nnu_ref, nfu_ref,
                  x1_ref, lpos_ref, w_ref, ga2_ref, ys_hbm, o_ref, loc_ref, sems):
    blk = pl.program_id(0)
    last = pl.num_programs(0) - 1
    slot = blk % 2

    def fetch(tabs, fill_ref, b, s):
        def run_copy(loc, src, rows, level):
            return pltpu.make_async_copy(ys_hbm.at[pl.ds(src, rows)], loc_ref.at[s, pl.ds(loc, rows)],
                                         sems.at[s])

        units = fill_ref[b % FILL_CHUNK]
        used = LOCAL_ROWS - units * RUN_ALIGN

        def fill_copy(off, rows, level):
            return pltpu.make_async_copy(ys_hbm.at[pl.ds(0, rows)],
                                         loc_ref.at[s, pl.ds(pl.multiple_of(used + off, RUN_ALIGN), rows)],
                                         sems.at[s])

        _run_copies(_start, tabs, _table_base(b), run_copy)
        _fill_copies(units, fill_copy)

    @pl.when(blk == 0)
    def _():
        fetch((ls_ref, ds_ref, nu_ref), fu_ref, blk, slot)

    @pl.when(blk < last)
    def _():
        fetch((nls_ref, nds_ref, nnu_ref), nfu_ref, blk + 1, 1 - slot)

    pltpu.make_async_copy(ys_hbm.at[pl.ds(0, LOCAL_ROWS)], loc_ref.at[slot], sems.at[slot]).wait()

    y_lo, y_hi = _unpack_pairs(loc_ref[slot])
    w = w_ref[...]
    mix = _slot_matrix(lpos_ref[...], [w[:, kk:kk + 1] for kk in range(TOP_K)]).astype(BF16)
    halves = [jnp.dot(mix, y, preferred_element_type=F32) for y in (y_lo, y_hi)]
    o_ref[...] = x1_ref[...] + ga2_ref[0] * jnp.concatenate(halves, axis=1)


def _combine_call(tabs, x1, lpos, wts, ga2, ys, B, S):
    T = B * S
    nb = S // ROW_BLOCK
    row = lambda i: (i, 0)
    return pl.pallas_call(
        _combine_body,
        grid=(T // ROW_BLOCK,),
        in_specs=_table_specs(0, T // ROW_BLOCK) + _table_specs(1, T // ROW_BLOCK) + [
            pl.BlockSpec((ROW_BLOCK, D_MODEL), row),
            pl.BlockSpec((ROW_BLOCK, TOP_K), row),
            pl.BlockSpec((ROW_BLOCK, TOP_K), row),
            pl.BlockSpec((1, 1, D_MODEL), lambda i: (i // nb, 0, 0)),
            pl.BlockSpec(memory_space=pl.ANY)],
        out_specs=pl.BlockSpec((ROW_BLOCK, D_MODEL), row),
        scratch_shapes=[pltpu.VMEM((2, LOCAL_ROWS, D_MODEL // 2), U32),
                        pltpu.SemaphoreType.DMA((2,))],
        out_shape=jax.ShapeDtypeStruct((T, D_MODEL), F32),
        compiler_params=_cparams(("arbitrary",)),
        name="moe_combine",
    )(*tabs, *tabs, x1, lpos, wts, ga2, ys)


def _pad_heads(w, per_head, n_heads):
    k = w.shape[0]
    w = w.reshape(k, n_heads, per_head)
    w = jnp.pad(w, ((0, 0), (0, 0), (0, HEAD_PAD - per_head)))
    return w.reshape(k, n_heads * HEAD_PAD)


def _prepare(w_in, g_norm1, g_q_lat, w_uq, g_kv_lat, w_ukv, g_qk_q, g_qk_k, w_o_mla, w_o_sb,
             w_out, g_norm2, w_router, b_router, w_gate_up, b_gate_up, w_down, b_down):
    c0 = Q_LORA
    c1 = c0 + KV_LORA
    c2 = c1 + QK_ROPE
    sbw = SB_HEADS * SB_DIM
    c3 = c2 + 3 * sbw
    c4 = c3 + D_MODEL
    p = {}
    p["g_norm1"] = g_norm1.reshape(1, D_MODEL)
    p["g_norm2"] = g_norm2.reshape(1, D_MODEL)
    p["w_ql"] = w_in[:, :c0].astype(BF16)
    p["w_kvl"] = w_in[:, c0:c1].astype(BF16)
    p["w_kpe"] = jnp.pad(w_in[:, c1:c2], ((0, 0), (QK_NOPE, HEAD_PAD - QK_DIM))).astype(BF16)
    p["w_sq"] = w_in[:, c2:c2 + sbw].astype(BF16)
    p["w_sk"] = w_in[:, c2 + sbw:c2 + 2 * sbw].astype(BF16)
    p["w_sv"] = w_in[:, c2 + 2 * sbw:c3].astype(BF16)
    p["w_ga"] = w_in[:, c3:c4].astype(BF16)
    p["w_gb"] = w_in[:, c4:].astype(BF16)
    p["g_q_lat"] = g_q_lat.reshape(1, Q_LORA)
    p["g_kv_lat"] = g_kv_lat.reshape(1, KV_LORA)
    p["w_uq"] = _pad_heads(w_uq, QK_DIM, MLA_HEADS).astype(BF16)
    kv = w_ukv.reshape(KV_LORA, MLA_HEADS, QK_NOPE + V_DIM)
    p["w_uk"] = _pad_heads(kv[:, :, :QK_NOPE].reshape(KV_LORA, MLA_HEADS * QK_NOPE),
                           QK_NOPE, MLA_HEADS).astype(BF16)
    p["w_v"] = kv[:, :, QK_NOPE:].reshape(KV_LORA, MLA_HEADS * V_DIM).astype(BF16)
    p["g_qk_q"] = jnp.pad(g_qk_q, (0, HEAD_PAD - QK_DIM)).reshape(1, HEAD_PAD)
    p["g_qk_k"] = jnp.pad(g_qk_k, (0, HEAD_PAD - QK_DIM)).reshape(1, HEAD_PAD)
    inv_freq = 1.0 / (ROPE_THETA ** (jnp.arange(0, QK_ROPE, 2, dtype=F32) / QK_ROPE))
    p["invf"] = jnp.concatenate([jnp.zeros((QK_NOPE,), F32), inv_freq, inv_freq,
                                 jnp.zeros((HEAD_PAD - QK_DIM,), F32)]).reshape(1, HEAD_PAD)
    p["w_o_mla"] = w_o_mla.astype(BF16)
    p["w_o_sb"] = w_o_sb.astype(BF16)
    p["w_out"] = w_out.astype(BF16)
    wr = jnp.pad(w_router, ((0, 0), (0, LANES - N_EXPERTS)))
    p["w_r_hi"] = wr.astype(BF16)
    p["w_r_lo"] = (wr - p["w_r_hi"].astype(F32)).astype(BF16)
    p["b_r"] = jnp.concatenate([b_router, jnp.full((LANES - N_EXPERTS,), NEG_BIG, F32)]).reshape(1, LANES)
    p["w_gate_up"] = w_gate_up
    p["b_gate_up"] = b_gate_up.reshape(N_EXPERTS, 1, 2 * D_EXPERT)
    p["w_down"] = w_down
    p["b_down"] = b_down.reshape(N_EXPERTS, 1, D_MODEL)
    return p


def _layer(x2, c, posf, B, S, w_ada, b_ada, *layer_weights):
    T = B * S
    p = _prepare(*layer_weights)
    mod = _ada_call(c, w_ada, b_ada)
    sh1, sc1, ga1, sh2, sc2, ga2 = [m.reshape(B, 1, D_MODEL) for m in jnp.split(mod, 6, axis=-1)]

    q, k, v, sq, sk, sv = _proj_call(x2, sh1, sc1, posf, p, B, S)
    o_mla = _mla_call(q, k, v, B, S)
    o_sb = _sb_call(sq, sk, sv, B, S)
    x1, h2p, lpos, wts, tab, totals = _merge_call(x2, (sh1, sc1, ga1, sh2, sc2), o_mla, o_sb, p, B, S)

    nblk = T // ROW_BLOCK
    rows_e = totals[0, :N_EXPERTS].astype(I32)
    tiles_e = (rows_e + EXPERT_TILE - 1) // EXPERT_TILE
    tile_end = jnp.cumsum(tiles_e)
    region = ((tile_end - tiles_e) * EXPERT_TILE).astype(I32)
    n_tiles = (T * TOP_K + nblk * N_EXPERTS * (RUN_ALIGN - 1)) // EXPERT_TILE + N_EXPERTS
    tile_ids = jnp.arange(n_tiles, dtype=I32)
    tile_e = jnp.minimum(jnp.sum((tile_ids[:, None] >= tile_end[None, :]).astype(I32), axis=1),
                         N_EXPERTS - 1).astype(I32)
    tile_v = (tile_ids < tile_end[-1]).astype(I32)
    pad = (-nblk) % TABLE_BLOCKS

    def flat(t):
        return jnp.pad(t, ((0, pad), (0, 0))).reshape(-1)

    units = tab[:, 1, :N_EXPERTS]
    fill_units = LOCAL_ROWS // RUN_ALIGN - jnp.sum(units, axis=1)
    tabs = (flat(tab[:, 0, :N_EXPERTS]), flat(tab[:, 2, :N_EXPERTS] + region[None, :]), flat(units),
            jnp.pad(fill_units, (0, (-nblk) % FILL_CHUNK)))
    table_len = TABLE_BLOCKS * N_EXPERTS
    tail_tabs = tuple(jnp.pad(t, (0, table_len - N_EXPERTS))
                      for t in (region + rows_e, (tiles_e * EXPERT_TILE - rows_e) // RUN_ALIGN))

    xs = _dispatch_call(tabs, tail_tabs, h2p, lpos, n_tiles * EXPERT_TILE)
    ys = _expert_call(tile_e, tile_v, xs, p)
    return _combine_call(tabs, x1, lpos, wts, ga2, ys, B, S)


def kernel(x, c, positions, w_ada, b_ada, g_norm1, w_in, g_q_lat, w_uq, g_kv_lat, w_ukv, g_qk_q,
           g_qk_k, w_o_mla, w_o_sb, w_out, g_norm2, w_router, b_router, w_gate_up, b_gate_up,
           w_down, b_down):
    B, S, D = x.shape
    x2 = x.reshape(B * S, D)
    posf = positions.astype(F32).reshape(B * S, 1)
    for l in range(w_ada.shape[0]):
        x2 = _layer(x2, c, posf, B, S, w_ada[l], b_ada[l], w_in[l], g_norm1[l], g_q_lat[l],
                    w_uq[l], g_kv_lat[l], w_ukv[l], g_qk_q[l], g_qk_k[l], w_o_mla[l], w_o_sb[l],
                    w_out[l], g_norm2[l], w_router[l], b_router[l], w_gate_up[l], b_gate_up[l],
                    w_down[l], b_down[l])
    return x2.reshape(B, S, D)
```

```python
import functools

import jax
import jax.numpy as jnp
import numpy as np
from jax import lax
from jax.experimental import pallas as pl
from jax.experimental.pallas import tpu as pltpu

F32 = jnp.float32
BF16 = jnp.bfloat16
I32 = jnp.int32
U32 = jnp.uint32

D_MODEL = 1024
EPS = 1e-6
CHUNK = 64
MLA_HEADS = 8
Q_LORA = 384
KV_LORA = 256
QK_NOPE = 64
QK_ROPE = 32
V_DIM = 64
QK_DIM = QK_NOPE + QK_ROPE
ROPE_THETA = 10000.0
SB_HEADS = 8
SB_DIM = 64
N_EXPERTS = 32
TOP_K = 4
D_EXPERT = D_MODEL
SWIGLU_LIMIT = 7.0
SWIGLU_ALPHA = 1.702

LANES = 128
HEAD_PAD = LANES
TOK_BLOCK = 1024
SUB_ROWS = 256
MLA_BLOCK = 512
SB_BLOCK = 256
ROW_BLOCK = 256
EXPERT_TILE = 512
RUN_ALIGN = 8
RUN_BITS = tuple(2 ** b for b in reversed(range((ROW_BLOCK // RUN_ALIGN).bit_length())))
LOCAL_ROWS = ROW_BLOCK * TOP_K + N_EXPERTS * RUN_ALIGN
SHORT_RUN = 8
TABLE_BLOCKS = 32
FILL_CHUNK = TABLE_BLOCKS * N_EXPERTS
DUMP_OFFSETS = tuple(RUN_ALIGN * sum(RUN_BITS[:l]) for l in range(len(RUN_BITS)))
DUMP_ROWS = RUN_ALIGN * 2 * RUN_BITS[0]
TAIL_BITS = tuple(2 ** b for b in reversed(range((EXPERT_TILE // RUN_ALIGN - 1).bit_length())))
NEG_BIG = -1e30
LOG2E = 1.4426950408889634
MLA_LOGIT_SCALE = QK_DIM ** -0.5 * LOG2E
SB_LOGIT_SCALE = SB_DIM ** -0.5 * LOG2E
VMEM_LIMIT = 48 * 1024 * 1024


def _cparams(sem):
    return pltpu.CompilerParams(dimension_semantics=sem, vmem_limit_bytes=VMEM_LIMIT)


def _ada_body(c_ref, w_ref, b_ref, o_ref):
    c = c_ref[...]
    s = c * jax.nn.sigmoid(c)
    o_ref[...] = jnp.dot(s, w_ref[...], preferred_element_type=F32,
                         precision=lax.Precision.HIGHEST) + b_ref[...]


def _ada_call(c, w_ada, b_ada):
    B = c.shape[0]
    n = w_ada.shape[1]
    bn = 512
    return pl.pallas_call(
        _ada_body,
        grid=(n // bn,),
        in_specs=[pl.BlockSpec((B, D_MODEL), lambda j: (0, 0)),
                  pl.BlockSpec((D_MODEL, bn), lambda j: (0, j)),
                  pl.BlockSpec((1, bn), lambda j: (0, j))],
        out_specs=pl.BlockSpec((B, bn), lambda j: (0, j)),
        out_shape=jax.ShapeDtypeStruct((B, n), F32),
        compiler_params=_cparams(("arbitrary",)),
        name="ada_mod",
    )(c, w_ada, b_ada.reshape(1, n))


def _rms(v, width):
    return lax.rsqrt(jnp.sum(v * v, axis=-1, keepdims=True) * (1.0 / width) + EPS)


def _modulated_norm(x, g, sh, sc):
    h = x * _rms(x, D_MODEL) * g
    return h * (1.0 + sc) + sh


def _rope_body(invf_ref, pos_ref, cos_ref, sin_ref):
    pos = pos_ref[...]
    for f in range(QK_ROPE // 2):
        ang = pos * invf_ref[f]
        cos_ref[f] = jnp.cos(ang)
        sin_ref[f] = jnp.sin(ang)


def _rope_table(positions):
    T = positions.size
    half = QK_ROPE // 2
    assert T % (8 * LANES) == 0
    inv_freq = 1.0 / (ROPE_THETA ** (jnp.arange(0, QK_ROPE, 2, dtype=F32) / QK_ROPE))
    pos = positions.astype(F32).reshape(T // LANES, LANES)
    rb = max(r for r in (8, 16, 32, 64) if (T // LANES) % r == 0)
    tables = pl.pallas_call(
        _rope_body,
        grid_spec=pltpu.PrefetchScalarGridSpec(
            num_scalar_prefetch=1,
            grid=(T // LANES // rb,),
            in_specs=[pl.BlockSpec((rb, LANES), lambda i, f: (i, 0))],
            out_specs=[pl.BlockSpec((half, rb, LANES), lambda i, f: (0, i, 0))] * 2),
        out_shape=[jax.ShapeDtypeStruct((half, T // LANES, LANES), F32)] * 2,
        compiler_params=_cparams(("arbitrary",)),
        name="rope_table",
    )(inv_freq, pos)
    cos, sin = [t.reshape(half, T).T for t in tables]
    gap = jnp.zeros((T, QK_NOPE - QK_ROPE), F32)
    return jnp.concatenate([sin, sin, gap, cos, cos, jnp.zeros((T, HEAD_PAD - QK_DIM), F32)], axis=1)


def _proj_body(x_ref, sh_ref, sc_ref, rope_ref, gn_ref, wql_ref, wkvl_ref, wkpe_ref,
               wsq_ref, wsk_ref, wsv_ref, gql_ref, wuq_ref, gkvl_ref, wuk_ref, wv_ref,
               gq_ref, gk_ref,
               q_ref, k_ref, v_ref, sq_ref, sk_ref, sv_ref):
    for r in range(x_ref.shape[0] // SUB_ROWS):
        rows = slice(r * SUB_ROWS, (r + 1) * SUB_ROWS)
        x = x_ref[rows, :]
        h = _modulated_norm(x, gn_ref[...], sh_ref[0], sc_ref[0]).astype(BF16)

        sq_ref[rows, :] = (jnp.dot(h, wsq_ref[...], preferred_element_type=F32) * SB_LOGIT_SCALE).astype(BF16)
        sk_ref[rows, :] = jnp.dot(h, wsk_ref[...], preferred_element_type=F32).astype(BF16)
        sv_ref[rows, :] = jnp.dot(h, wsv_ref[...], preferred_element_type=F32).astype(BF16)

        q_lat = jnp.dot(h, wql_ref[...], preferred_element_type=F32)
        kv_lat = jnp.dot(h, wkvl_ref[...], preferred_element_type=F32)
        kpe = jnp.dot(h, wkpe_ref[...], preferred_element_type=F32)

        qn = (q_lat * _rms(q_lat, Q_LORA) * gql_ref[...]).astype(BF16)
        kvn = (kv_lat * _rms(kv_lat, KV_LORA) * gkvl_ref[...]).astype(BF16)
        q = jnp.dot(qn, wuq_ref[...], preferred_element_type=F32)
        kn = jnp.dot(kvn, wuk_ref[...], preferred_element_type=F32)
        v_ref[rows, :] = jnp.dot(kvn, wv_ref[...], preferred_element_type=F32).astype(BF16)

        lane = lax.broadcasted_iota(I32, (SUB_ROWS, LANES), 1)
        tbl = rope_ref[rows, :]
        sinf = pltpu.roll(tbl, QK_NOPE, 1)
        half = QK_ROPE // 2
        cosf = jnp.where((lane >= QK_NOPE) & (lane < QK_DIM), tbl, 1.0)
        s_lo = jnp.where((lane >= QK_NOPE) & (lane < QK_NOPE + half), -sinf, 0.0)
        s_hi = jnp.where((lane >= QK_NOPE + half) & (lane < QK_DIM), sinf, 0.0)

        def rope(t):
            return (t * cosf + pltpu.roll(t, LANES - half, 1) * s_lo
                    + pltpu.roll(t, half, 1) * s_hi)

        gq = gq_ref[...] * MLA_LOGIT_SCALE
        gk = gk_ref[...]
        kpe_sq = jnp.sum(kpe * kpe, axis=-1, keepdims=True)
        kr = rope(kpe * gk)
        for hh in range(MLA_HEADS):
            sl = slice(hh * HEAD_PAD, (hh + 1) * HEAD_PAD)
            qh = q[:, sl]
            q_ref[rows, sl] = rope(qh * _rms(qh, QK_DIM) * gq).astype(BF16)
            kh = kn[:, sl]
            rk = lax.rsqrt((jnp.sum(kh * kh, axis=-1, keepdims=True) + kpe_sq) * (1.0 / QK_DIM) + EPS)
            k_ref[rows, sl] = ((kh * gk + kr) * rk).astype(BF16)


def _proj_call(x2, sh1, sc1, rope, p, B, S):
    T = B * S
    tm = TOK_BLOCK
    nb = S // tm
    row = lambda i: (i, 0)
    per_b = lambda i: (i // nb, 0, 0)
    full = lambda i: (0, 0)

    def wspec(a):
        return pl.BlockSpec(a.shape, full)

    weights = [p["g_norm1"], p["w_ql"], p["w_kvl"], p["w_kpe"], p["w_sq"], p["w_sk"], p["w_sv"],
               p["g_q_lat"], p["w_uq"], p["g_kv_lat"], p["w_uk"], p["w_v"], p["g_qk_q"],
               p["g_qk_k"]]
    out_w = [MLA_HEADS * HEAD_PAD, MLA_HEADS * HEAD_PAD, MLA_HEADS * V_DIM,
             SB_HEADS * SB_DIM, SB_HEADS * SB_DIM, SB_HEADS * SB_DIM]
    return pl.pallas_call(
        _proj_body,
        grid=(T // tm,),
        in_specs=[pl.BlockSpec((tm, D_MODEL), row),
                  pl.BlockSpec((1, 1, D_MODEL), per_b),
                  pl.BlockSpec((1, 1, D_MODEL), per_b),
                  pl.BlockSpec((tm, LANES), row)] + [wspec(a) for a in weights],
        out_specs=[pl.BlockSpec((tm, w), row) for w in out_w],
        out_shape=[jax.ShapeDtypeStruct((T, w), BF16) for w in out_w],
        compiler_params=_cparams(("arbitrary",)),
        name="mix_proj",
    )(x2, sh1, sc1, rope, *weights)


_NT = (((1,), (1,)), ((), ()))


def _mla_rows(qh, k_ref, v_ref, sl, i, tq):
    kend = (i + 1) * tq
    s = lax.dot_general(qh, k_ref[0:kend, sl], _NT, preferred_element_type=F32)
    row = lax.broadcasted_iota(I32, (tq, tq), 0)
    col = lax.broadcasted_iota(I32, (tq, tq), 1)
    s_diag = jnp.where((col // CHUNK) <= (row // CHUNK), s[:, kend - tq:], NEG_BIG)
    s = s_diag if i == 0 else jnp.concatenate([s[:, :kend - tq], s_diag], axis=1)
    pr = jnp.exp2(s - jnp.max(s, axis=-1, keepdims=True))
    l = jnp.sum(pr, axis=-1, keepdims=True)
    return jnp.dot(pr.astype(BF16), v_ref[0:kend, :], preferred_element_type=F32) / l


def _mla_body(q_ref, k_ref, v_ref, o_ref):
    tq = MLA_BLOCK
    lane = lax.broadcasted_iota(I32, (tq, LANES), 1)
    for ii in range(q_ref.shape[0] // tq):
        rows = slice(ii * tq, (ii + 1) * tq)
        outs = []
        for hh in range(2):
            sl = slice(hh * HEAD_PAD, (hh + 1) * HEAD_PAD)
            outs.append(_mla_rows(q_ref[rows, sl], k_ref, v_ref, sl, ii, tq))
        o_ref[rows, :] = jnp.where(lane < V_DIM, outs[0], outs[1]).astype(BF16)


def _mla_call(q, k, v, B, S):
    q3 = q.reshape(B, S, MLA_HEADS * HEAD_PAD)
    k3 = k.reshape(B, S, MLA_HEADS * HEAD_PAD)
    v3 = v.reshape(B, S, MLA_HEADS * V_DIM)
    pair = lambda b, hp: (b, 0, hp)
    out = pl.pallas_call(
        _mla_body,
        grid=(B, MLA_HEADS // 2),
        in_specs=[pl.BlockSpec((None, S, 2 * HEAD_PAD), pair),
                  pl.BlockSpec((None, S, 2 * HEAD_PAD), pair),
                  pl.BlockSpec((None, S, 2 * V_DIM), pair)],
        out_specs=pl.BlockSpec((None, S, 2 * V_DIM), pair),
        out_shape=jax.ShapeDtypeStruct((B, S, MLA_HEADS * V_DIM), BF16),
        compiler_params=_cparams(("arbitrary", "arbitrary")),
        name="mla_attn",
    )(q3, k3, v3)
    return out.reshape(B * S, MLA_HEADS * V_DIM)


def _sb_rows(qh, k_ref, v_ref, i, tq):
    kend = (i + 1) * tq
    z = lax.dot_general(qh, k_ref[0:kend, :], _NT, preferred_element_type=F32)
    sp = jnp.maximum(z, 0.0) + jnp.log2(1.0 + jnp.exp2(jnp.minimum(z, -z)))
    row = lax.broadcasted_iota(I32, (tq, tq), 0)
    col = lax.broadcasted_iota(I32, (tq, tq), 1)
    strict = col < row
    suffix_ones = jnp.where(row >= col, 1.0, 0.0).astype(BF16)
    later = jnp.zeros((tq, 1), F32)
    a_blocks = [None] * (i + 1)
    for j in range(i, -1, -1):
        cols = slice(j * tq, (j + 1) * tq)
        spj = sp[:, cols]
        if j == i:
            spj = jnp.where(strict, spj, 0.0)
        suf = jnp.dot(spj.astype(BF16), suffix_ones, preferred_element_type=F32)
        a = jnp.exp2(z[:, cols] - suf - later)
        if j == i:
            a = jnp.where(strict, a, 0.0)
        a_blocks[j] = a.astype(BF16)
        later = later + suf[:, 0:1]
    a = a_blocks[0] if i == 0 else jnp.concatenate(a_blocks, axis=1)
    return jnp.dot(a, v_ref[0:kend, :], preferred_element_type=F32)


def _sb_body(q_ref, k_ref, v_ref, o_ref):
    tq = SB_BLOCK
    lane = lax.broadcasted_iota(I32, (tq, LANES), 1)
    for ii in range(q_ref.shape[0] // tq):
        rows = slice(ii * tq, (ii + 1) * tq)
        q = q_ref[rows, :]
        outs = []
        for hh in range(2):
            in_head = (lane >= hh * SB_DIM) & (lane < (hh + 1) * SB_DIM)
            qh = jnp.where(in_head, q, jnp.zeros_like(q))
            outs.append(_sb_rows(qh, k_ref, v_ref, ii, tq))
        o_ref[rows, :] = jnp.where(lane < SB_DIM, outs[0], outs[1]).astype(BF16)


def _sb_call(sq, sk, sv, B, S):
    w = SB_HEADS * SB_DIM
    args = [a.reshape(B, S, w) for a in (sq, sk, sv)]
    pair = pl.BlockSpec((None, S, 2 * SB_DIM), lambda b, hp: (b, 0, hp))
    out = pl.pallas_call(
        _sb_body,
        grid=(B, SB_HEADS // 2),
        in_specs=[pair, pair, pair],
        out_specs=pair,
        out_shape=jax.ShapeDtypeStruct((B, S, w), BF16),
        compiler_params=_cparams(("arbitrary", "arbitrary")),
        name="sb_attn",
    )(*args)
    return out.reshape(B * S, w)


def _pack_pairs(h):
    n = h.shape[1] // 2
    return _pack_halves(h[:, :n], h[:, n:])


def _pack_halves(lo, hi):
    lo = pltpu.bitcast(lo.astype(BF16).astype(F32), U32)
    hi = pltpu.bitcast(hi.astype(BF16).astype(F32), U32)
    return (lo >> 16) | (hi & jnp.uint32(0xFFFF0000))


def _unpack_pairs(w):
    lo = pltpu.bitcast(w << 16, F32).astype(BF16)
    hi = pltpu.bitcast(w & jnp.uint32(0xFFFF0000), F32).astype(BF16)
    return lo, hi


def _merge_body(x_ref, sh1_ref, sc1_ref, ga1_ref, sh2_ref, sc2_ref, om_ref, os_ref,
                gn1_ref, wga_ref, wgb_ref, wom_ref, wos_ref, wout_ref, gn2_ref,
                wrh_ref, wrl_ref, br_ref, tri_ref, su_ref,
                x1_ref, h2_ref, lpos_ref, w_ref, tab_ref, tot_ref, run_ref):
    @pl.when(pl.program_id(0) == 0)
    def _():
        run_ref[...] = jnp.zeros_like(run_ref)

    groups = x_ref.shape[0] // SUB_ROWS
    logits = []
    for r in range(groups):
        rows = slice(r * SUB_ROWS, (r + 1) * SUB_ROWS)
        x = x_ref[rows, :]
        h = _modulated_norm(x, gn1_ref[...], sh1_ref[0], sc1_ref[0]).astype(BF16)
        ga = jax.nn.sigmoid(jnp.dot(h, wga_ref[...], preferred_element_type=F32))
        gb = jax.nn.sigmoid(jnp.dot(h, wgb_ref[...], preferred_element_type=F32))
        merged = (ga * jnp.dot(om_ref[rows, :], wom_ref[...], preferred_element_type=F32)
                  + gb * jnp.dot(os_ref[rows, :], wos_ref[...], preferred_element_type=F32))
        y = jnp.dot(merged.astype(BF16), wout_ref[...], preferred_element_type=F32)
        x1 = x + ga1_ref[0] * y
        x1_ref[rows, :] = x1
        h2 = _modulated_norm(x1, gn2_ref[...], sh2_ref[0], sc2_ref[0])
        h2_ref[rows, :] = _pack_pairs(h2)
        h2_hi = h2.astype(BF16)
        h2_lo = (h2 - h2_hi.astype(F32)).astype(BF16)
        logits.append(jnp.dot(h2_hi, wrh_ref[...], preferred_element_type=F32)
                      + jnp.dot(h2_hi, wrl_ref[...], preferred_element_type=F32)
                      + jnp.dot(h2_lo, wrh_ref[...], preferred_element_type=F32)
                      + br_ref[...])
    lpos, wts, tabs, run = _route_rows(jnp.concatenate(logits, axis=0), tri_ref[...], su_ref[...],
                                       run_ref[...])
    lpos_ref[...] = lpos
    w_ref[...] = wts
    for r in range(groups):
        tab_ref[r] = tabs[r]
    run_ref[...] = run
    tot_ref[...] = run


def _merge_call(x2, mods, o_mla, o_sb, p, B, S):
    assert SUB_ROWS == ROW_BLOCK
    T = B * S
    tm = TOK_BLOCK
    nb = S // tm
    per_step = tm // ROW_BLOCK
    row = lambda i: (i, 0)
    per_b = lambda i: (i // nb, 0, 0)
    full = lambda i: (0, 0)
    tri = jnp.asarray(np.arange(ROW_BLOCK)[:, None] > np.arange(ROW_BLOCK)[None, :], dtype=BF16)
    su = jnp.asarray(np.arange(LANES)[:, None] < np.arange(LANES)[None, :], dtype=BF16)
    weights = [p["g_norm1"], p["w_ga"], p["w_gb"], p["w_o_mla"], p["w_o_sb"], p["w_out"],
               p["g_norm2"], p["w_r_hi"], p["w_r_lo"], p["b_r"], tri, su]
    return pl.pallas_call(
        _merge_body,
        grid=(T // tm,),
        in_specs=[pl.BlockSpec((tm, D_MODEL), row)]
        + [pl.BlockSpec((1, 1, D_MODEL), per_b)] * 5
        + [pl.BlockSpec((tm, MLA_HEADS * V_DIM), row), pl.BlockSpec((tm, SB_HEADS * SB_DIM), row)]
        + [pl.BlockSpec(a.shape, full) for a in weights],
        out_specs=[pl.BlockSpec((tm, D_MODEL), row), pl.BlockSpec((tm, D_MODEL // 2), row),
                   pl.BlockSpec((tm, TOP_K), row), pl.BlockSpec((tm, TOP_K), row),
                   pl.BlockSpec((per_step, 8, LANES), lambda i: (i, 0, 0)),
                   pl.BlockSpec((1, LANES), full)],
        out_shape=[jax.ShapeDtypeStruct((T, D_MODEL), F32),
                   jax.ShapeDtypeStruct((T, D_MODEL // 2), U32),
                   jax.ShapeDtypeStruct((T, TOP_K), I32),
                   jax.ShapeDtypeStruct((T, TOP_K), F32),
                   jax.ShapeDtypeStruct((T // ROW_BLOCK, 8, LANES), I32),
                   jax.ShapeDtypeStruct((1, LANES), F32)],
        scratch_shapes=[pltpu.VMEM((1, LANES), F32)],
        compiler_params=_cparams(("arbitrary",)),
        name="merge_route",
    )(x2, *mods, o_mla, o_sb, *weights)


def _route_rows(v, tri, su, run):
    tm = v.shape[0]
    lane_i = lax.broadcasted_iota(I32, (tm, LANES), 1)
    lane = lane_i.astype(F32)
    vals, idxs = [], []
    for _ in range(TOP_K):
        m = jnp.max(v, axis=-1, keepdims=True)
        idx = jnp.min(jnp.where(v == m, lane, float(LANES)), axis=-1, keepdims=True)
        vals.append(m)
        idxs.append(idx)
        v = jnp.where(lane == idx, NEG_BIG, v)
    ex = [jnp.exp(t - vals[0]) for t in vals]
    denom = ex[0] + ex[1] + ex[2] + ex[3]
    onehots = [lane == idx for idx in idxs]
    oh = jnp.zeros((tm, LANES), F32)
    for o in onehots:
        oh = oh + jnp.where(o, 1.0, 0.0)
    sub = lax.broadcasted_iota(I32, (8, LANES), 0)
    pos, tabs = [], []
    for b in range(tm // ROW_BLOCK):
        oh_b = oh[b * ROW_BLOCK:(b + 1) * ROW_BLOCK]
        units = jnp.floor((jnp.sum(oh_b, axis=0, keepdims=True) + (RUN_ALIGN - 1.0)) * (1.0 / RUN_ALIGN))
        lstart = RUN_ALIGN * jnp.dot(jnp.broadcast_to(units, (8, LANES)).astype(BF16), su,
                                     preferred_element_type=F32)[0:1]
        pos.append(jnp.dot(tri, oh_b.astype(BF16), preferred_element_type=F32) + lstart)
        tab = jnp.where(sub == 0, lstart, jnp.where(sub == 1, units, jnp.where(sub == 2, run, 0.0)))
        tabs.append(tab.astype(I32))
        run = run + RUN_ALIGN * units
    pos = jnp.concatenate(pos, axis=0)
    lp_out = jnp.zeros((tm, LANES), I32)
    w_out = jnp.zeros((tm, LANES), F32)
    for kk in range(TOP_K):
        lp = jnp.sum(jnp.where(onehots[kk], pos, 0.0), axis=-1, keepdims=True)
        lp_out = jnp.where(lane_i == kk, lp.astype(I32), lp_out)
        w_out = jnp.where(lane_i == kk, ex[kk] / denom, w_out)
    return lp_out[:, :TOP_K], w_out[:, :TOP_K], tabs, run


def _run_copies(fn, tabs, base, make_copy, bits=RUN_BITS):
    ls_ref, ds_ref, nu_ref = tabs

    def pieces(levels, units, loc, dst):
        for level, bit in levels:
            rows = bit * RUN_ALIGN
            take = (units & bit) != 0

            @pl.when(take)
            def _():
                src = 0 if ls_ref is None else pl.multiple_of(loc, RUN_ALIGN)
                fn(make_copy(src, pl.multiple_of(dst, RUN_ALIGN), rows, level))

            step = jnp.where(take, rows, 0)
            if ls_ref is not None:
                loc = loc + step
            dst = dst + step
        return loc, dst

    levels = list(enumerate(bits))
    small = [lb for lb in levels if lb[1] < SHORT_RUN]
    large = [lb for lb in levels if lb[1] >= SHORT_RUN]

    def per_expert(e, c):
        loc = 0 if ls_ref is None else ls_ref[base + e]
        dst = ds_ref[base + e]
        units = nu_ref[base + e]
        loc, dst = pieces(small, units, loc, dst)

        @pl.when(units >= SHORT_RUN)
        def _():
            pieces(large, units, loc, dst)

        return c

    lax.fori_loop(0, N_EXPERTS, per_expert, 0)


_TN = (((0,), (0,)), ((), ()))


def _slot_matrix(lpos, values):
    tm = lpos.shape[0]
    col = lax.broadcasted_iota(I32, (tm, LOCAL_ROWS), 1)
    out = jnp.zeros((tm, LOCAL_ROWS), F32)
    for kk in range(TOP_K):
        out = jnp.where(col == lpos[:, kk:kk + 1], values[kk], out)
    return out


def _start(cp):
    cp.start()


def _wait(cp):
    cp.wait()


def _table_base(blk):
    return (blk % TABLE_BLOCKS) * N_EXPERTS


def _fill_copies(units, make_copy):
    off = 0
    for level, bit in enumerate(RUN_BITS):
        rows = bit * RUN_ALIGN
        take = (units & bit) != 0

        @pl.when(take)
        def _():
            make_copy(off, rows, level).start()

        off = off + jnp.where(take, rows, 0)


def _dispatch_body(ls_ref, ds_ref, nu_ref, fu_ref, tds_ref, tnu_ref,
                   h_ref, lpos_ref, xs_hbm, loc_ref, zero_ref, sem, tail_sems):
    blk = pl.program_id(0)
    last = pl.num_programs(0) - 1
    slot = blk % 2
    dump = xs_hbm.shape[0] - DUMP_ROWS
    lo, hi = _unpack_pairs(h_ref[...])
    sel = _slot_matrix(lpos_ref[...], [1.0] * TOP_K).astype(BF16)
    x_lo = lax.dot_general(sel, lo, _TN, preferred_element_type=F32)
    x_hi = lax.dot_general(sel, hi, _TN, preferred_element_type=F32)
    loc_ref[slot] = _pack_halves(x_lo, x_hi)

    def run_copy(loc, dst, rows, level):
        return pltpu.make_async_copy(loc_ref.at[slot, pl.ds(loc, rows)], xs_hbm.at[pl.ds(dst, rows)], sem)

    def fill_copy(off, rows, level):
        return pltpu.make_async_copy(loc_ref.at[slot, pl.ds(0, rows)],
                                     xs_hbm.at[pl.ds(dump + DUMP_OFFSETS[level], rows)], sem)

    def block_wait(s):
        pltpu.make_async_copy(loc_ref.at[s], xs_hbm.at[pl.ds(0, LOCAL_ROWS)], sem).wait()

    def tail_copy(loc, dst, rows, level):
        return pltpu.make_async_copy(zero_ref.at[pl.ds(loc, rows)], xs_hbm.at[pl.ds(dst, rows)],
                                     tail_sems.at[level])

    @pl.when(blk > 0)
    def _():
        block_wait(1 - slot)

    _run_copies(_start, (ls_ref, ds_ref, nu_ref), _table_base(blk), run_copy)
    _fill_copies(fu_ref[blk % FILL_CHUNK], fill_copy)

    @pl.when(blk == last)
    def _():
        block_wait(slot)
        zero_ref[...] = jnp.zeros_like(zero_ref)
        tails = (None, tds_ref, tnu_ref)
        _run_copies(_start, tails, 0, tail_copy, TAIL_BITS)
        _run_copies(_wait, tails, 0, tail_copy, TAIL_BITS)


def _table_specs(shift, nblk):
    def block(i):
        return jnp.clip(i + shift, 0, nblk - 1)

    run = pl.BlockSpec((TABLE_BLOCKS * N_EXPERTS,), lambda i: (block(i) // TABLE_BLOCKS,),
                       memory_space=pltpu.SMEM)
    fill = pl.BlockSpec((FILL_CHUNK,), lambda i: (block(i) // FILL_CHUNK,), memory_space=pltpu.SMEM)
    return [run, run, run, fill]


def _dispatch_call(tabs, tail_tabs, h2p, lpos, n_rows):
    T = h2p.shape[0]
    whole = pl.BlockSpec((TABLE_BLOCKS * N_EXPERTS,), lambda i: (0,), memory_space=pltpu.SMEM)
    return pl.pallas_call(
        _dispatch_body,
        grid=(T // ROW_BLOCK,),
        in_specs=_table_specs(0, T // ROW_BLOCK) + [whole, whole] + [
            pl.BlockSpec((ROW_BLOCK, D_MODEL // 2), lambda i: (i, 0)),
            pl.BlockSpec((ROW_BLOCK, TOP_K), lambda i: (i, 0))],
        out_specs=pl.BlockSpec(memory_space=pl.ANY),
        scratch_shapes=[pltpu.VMEM((2, LOCAL_ROWS, D_MODEL // 2), U32),
                        pltpu.VMEM((TAIL_BITS[0] * RUN_ALIGN, D_MODEL // 2), U32),
                        pltpu.SemaphoreType.DMA(()),
                        pltpu.SemaphoreType.DMA((len(TAIL_BITS),))],
        out_shape=jax.ShapeDtypeStruct((n_rows + DUMP_ROWS, D_MODEL // 2), U32),
        compiler_params=_cparams(("arbitrary",)),
        name="moe_dispatch",
    )(*tabs, *tail_tabs, h2p, lpos)


def _expert_body(te_ref, tv_ref, xs_ref, wgu_ref, bgu_ref, wd_ref, bd_ref, y_ref, wgu_bf, wd_bf):
    i = pl.program_id(0)

    @pl.when((i == 0) | (te_ref[i] != te_ref[jnp.maximum(i - 1, 0)]))
    def _():
        quarter = D_MODEL // 4
        for c in range(4):
            rows = slice(c * quarter, (c + 1) * quarter)
            wgu_bf[rows, :] = wgu_ref[rows, :].astype(BF16)
            wd_bf[rows, :] = wd_ref[rows, :].astype(BF16)

    @pl.when(tv_ref[i] == 1)
    def _():
        x = jnp.concatenate(_unpack_pairs(xs_ref[...]), axis=1)
        gu = jnp.dot(x, wgu_bf[...], preferred_element_type=F32) + bgu_ref[...]
        gate = jnp.minimum(gu[:, :D_EXPERT], SWIGLU_LIMIT)
        up = jnp.clip(gu[:, D_EXPERT:], -SWIGLU_LIMIT, SWIGLU_LIMIT)
        hid = (up + 1.0) * (gate * jax.nn.sigmoid(SWIGLU_ALPHA * gate))
        y = jnp.dot(hid.astype(BF16), wd_bf[...], preferred_element_type=F32) + bd_ref[...]
        y_ref[...] = _pack_pairs(y)

    @pl.when(tv_ref[i] == 0)
    def _():
        y_ref[...] = jnp.zeros_like(y_ref)


def _expert_call(tile_e, tile_v, xs, p):
    tmx = EXPERT_TILE
    n_rows = tile_e.shape[0] * tmx
    by_e = lambda i, te, tv: (te[i], 0, 0)
    return pl.pallas_call(
        _expert_body,
        grid_spec=pltpu.PrefetchScalarGridSpec(
            num_scalar_prefetch=2,
            grid=(n_rows // tmx,),
            in_specs=[pl.BlockSpec((tmx, D_MODEL // 2), lambda i, te, tv: (i * tv[i], 0)),
                      pl.BlockSpec((None, D_MODEL, 2 * D_EXPERT), by_e),
                      pl.BlockSpec((None, 1, 2 * D_EXPERT), by_e),
                      pl.BlockSpec((None, D_EXPERT, D_MODEL), by_e),
                      pl.BlockSpec((None, 1, D_MODEL), by_e)],
            out_specs=pl.BlockSpec((tmx, D_MODEL // 2), lambda i, te, tv: (i, 0)),
            scratch_shapes=[pltpu.VMEM((D_MODEL, 2 * D_EXPERT), BF16),
                            pltpu.VMEM((D_EXPERT, D_MODEL), BF16)]),
        out_shape=jax.ShapeDtypeStruct((n_rows, D_MODEL // 2), U32),
        compiler_params=_cparams(("arbitrary",)),
        name="moe_experts",
    )(tile_e, tile_v, xs, p["w_gate_up"], p["b_gate_up"], p["w_down"], p["b_down"])


def _combine_body(ls_ref, ds_ref, nu_ref, fu_ref, nls_ref, nds_ref, nnu_ref, nfu_ref,
                  x1_ref, lpos_ref, w_ref, ga2_ref, ys_hbm, o_ref, loc_ref, sems):
    blk = pl.program_id(0)
    last = pl.num_programs(0) - 1
    slot = blk % 2

    def fetch(tabs, fill_ref, b, s):
        def run_copy(loc, src, rows, level):
            return pltpu.make_async_copy(ys_hbm.at[pl.ds(src, rows)], loc_ref.at[s, pl.ds(loc, rows)],
                                         sems.at[s])

        units = fill_ref[b % FILL_CHUNK]
        used = LOCAL_ROWS - units * RUN_ALIGN

        def fill_copy(off, rows, level):
            return pltpu.make_async_copy(ys_hbm.at[pl.ds(0, rows)],
                                         loc_ref.at[s, pl.ds(pl.multiple_of(used + off, RUN_ALIGN), rows)],
                                         sems.at[s])

        _run_copies(_start, tabs, _table_base(b), run_copy)
        _fill_copies(units, fill_copy)

    @pl.when(blk == 0)
    def _():
        fetch((ls_ref, ds_ref, nu_ref), fu_ref, blk, slot)

    @pl.when(blk < last)
    def _():
        fetch((nls_ref, nds_ref, nnu_ref), nfu_ref, blk + 1, 1 - slot)

    pltpu.make_async_copy(ys_hbm.at[pl.ds(0, LOCAL_ROWS)], loc_ref.at[slot], sems.at[slot]).wait()

    y_lo, y_hi = _unpack_pairs(loc_ref[slot])
    w = w_ref[...]
    mix = _slot_matrix(lpos_ref[...], [w[:, kk:kk + 1] for kk in range(TOP_K)]).astype(BF16)
    halves = [jnp.dot(mix, y, preferred_element_type=F32) for y in (y_lo, y_hi)]
    o_ref[...] = x1_ref[...] + ga2_ref[0] * jnp.concatenate(halves, axis=1)


def _combine_call(tabs, x1, lpos, wts, ga2, ys, B, S):
    T = B * S
    nb = S // ROW_BLOCK
    row = lambda i: (i, 0)
    return pl.pallas_call(
        _combine_body,
        grid=(T // ROW_BLOCK,),
        in_specs=_table_specs(0, T // ROW_BLOCK) + _table_specs(1, T // ROW_BLOCK) + [
            pl.BlockSpec((ROW_BLOCK, D_MODEL), row),
            pl.BlockSpec((ROW_BLOCK, TOP_K), row),
            pl.BlockSpec((ROW_BLOCK, TOP_K), row),
            pl.BlockSpec((1, 1, D_MODEL), lambda i: (i // nb, 0, 0)),
            pl.BlockSpec(memory_space=pl.ANY)],
        out_specs=pl.BlockSpec((ROW_BLOCK, D_MODEL), row),
        scratch_shapes=[pltpu.VMEM((2, LOCAL_ROWS, D_MODEL // 2), U32),
                        pltpu.SemaphoreType.DMA((2,))],
        out_shape=jax.ShapeDtypeStruct((T, D_MODEL), F32),
        compiler_params=_cparams(("arbitrary",)),
        name="moe_combine",
    )(*tabs, *tabs, x1, lpos, wts, ga2, ys)


def _pad_heads(w, per_head, n_heads):
    k = w.shape[0]
    w = w.reshape(k, n_heads, per_head)
    w = jnp.pad(w, ((0, 0), (0, 0), (0, HEAD_PAD - per_head)))
    return w.reshape(k, n_heads * HEAD_PAD)


def _prepare(w_in, g_norm1, g_q_lat, w_uq, g_kv_lat, w_ukv, g_qk_q, g_qk_k, w_o_mla, w_o_sb,
             w_out, g_norm2, w_router, b_router, w_gate_up, b_gate_up, w_down, b_down):
    c0 = Q_LORA
    c1 = c0 + KV_LORA
    c2 = c1 + QK_ROPE
    sbw = SB_HEADS * SB_DIM
    c3 = c2 + 3 * sbw
    c4 = c3 + D_MODEL
    p = {}
    p["g_norm1"] = g_norm1.reshape(1, D_MODEL)
    p["g_norm2"] = g_norm2.reshape(1, D_MODEL)
    p["w_ql"] = w_in[:, :c0].astype(BF16)
    p["w_kvl"] = w_in[:, c0:c1].astype(BF16)
    p["w_kpe"] = jnp.pad(w_in[:, c1:c2], ((0, 0), (QK_NOPE, HEAD_PAD - QK_DIM))).astype(BF16)
    p["w_sq"] = w_in[:, c2:c2 + sbw].astype(BF16)
    p["w_sk"] = w_in[:, c2 + sbw:c2 + 2 * sbw].astype(BF16)
    p["w_sv"] = w_in[:, c2 + 2 * sbw:c3].astype(BF16)
    p["w_ga"] = w_in[:, c3:c4].astype(BF16)
    p["w_gb"] = w_in[:, c4:].astype(BF16)
    p["g_q_lat"] = g_q_lat.reshape(1, Q_LORA)
    p["g_kv_lat"] = g_kv_lat.reshape(1, KV_LORA)
    p["w_uq"] = _pad_heads(w_uq, QK_DIM, MLA_HEADS).astype(BF16)
    kv = w_ukv.reshape(KV_LORA, MLA_HEADS, QK_NOPE + V_DIM)
    p["w_uk"] = _pad_heads(kv[:, :, :QK_NOPE].reshape(KV_LORA, MLA_HEADS * QK_NOPE),
                           QK_NOPE, MLA_HEADS).astype(BF16)
    p["w_v"] = kv[:, :, QK_NOPE:].reshape(KV_LORA, MLA_HEADS * V_DIM).astype(BF16)
    p["g_qk_q"] = jnp.pad(g_qk_q, (0, HEAD_PAD - QK_DIM)).reshape(1, HEAD_PAD)
    p["g_qk_k"] = jnp.pad(g_qk_k, (0, HEAD_PAD - QK_DIM)).reshape(1, HEAD_PAD)
    p["w_o_mla"] = w_o_mla.astype(BF16)
    p["w_o_sb"] = w_o_sb.astype(BF16)
    p["w_out"] = w_out.astype(BF16)
    wr = jnp.pad(w_router, ((0, 0), (0, LANES - N_EXPERTS)))
    p["w_r_hi"] = wr.astype(BF16)
    p["w_r_lo"] = (wr - p["w_r_hi"].astype(F32)).astype(BF16)
    p["b_r"] = jnp.concatenate([b_router, jnp.full((LANES - N_EXPERTS,), NEG_BIG, F32)]).reshape(1, LANES)
    p["w_gate_up"] = w_gate_up
    p["b_gate_up"] = b_gate_up.reshape(N_EXPERTS, 1, 2 * D_EXPERT)
    p["w_down"] = w_down
    p["b_down"] = b_down.reshape(N_EXPERTS, 1, D_MODEL)
    return p


def _layer(x2, c, rope, B, S, w_ada, b_ada, *layer_weights):
    T = B * S
    p = _prepare(*layer_weights)
    mod = _ada_call(c, w_ada, b_ada)
    sh1, sc1, ga1, sh2, sc2, ga2 = [m.reshape(B, 1, D_MODEL) for m in jnp.split(mod, 6, axis=-1)]

    q, k, v, sq, sk, sv = _proj_call(x2, sh1, sc1, rope, p, B, S)
    o_mla = _mla_call(q, k, v, B, S)
    o_sb = _sb_call(sq, sk, sv, B, S)
    x1, h2p, lpos, wts, tab, totals = _merge_call(x2, (sh1, sc1, ga1, sh2, sc2), o_mla, o_sb, p, B, S)

    nblk = T // ROW_BLOCK
    rows_e = totals[0, :N_EXPERTS].astype(I32)
    tiles_e = (rows_e + EXPERT_TILE - 1) // EXPERT_TILE
    tile_end = jnp.cumsum(tiles_e)
    region = ((tile_end - tiles_e) * EXPERT_TILE).astype(I32)
    n_tiles = (T * TOP_K + nblk * N_EXPERTS * (RUN_ALIGN - 1)) // EXPERT_TILE + N_EXPERTS
    tile_ids = jnp.arange(n_tiles, dtype=I32)
    tile_e = jnp.minimum(jnp.sum((tile_ids[:, None] >= tile_end[None, :]).astype(I32), axis=1),
                         N_EXPERTS - 1).astype(I32)
    tile_v = (tile_ids < tile_end[-1]).astype(I32)
    pad = (-nblk) % TABLE_BLOCKS

    def flat(t):
        return jnp.pad(t, ((0, pad), (0, 0))).reshape(-1)

    units = tab[:, 1, :N_EXPERTS]
    fill_units = LOCAL_ROWS // RUN_ALIGN - jnp.sum(units, axis=1)
    tabs = (flat(tab[:, 0, :N_EXPERTS]), flat(tab[:, 2, :N_EXPERTS] + region[None, :]), flat(units),
            jnp.pad(fill_units, (0, (-nblk) % FILL_CHUNK)))
    table_len = TABLE_BLOCKS * N_EXPERTS
    tail_tabs = tuple(jnp.pad(t, (0, table_len - N_EXPERTS))
                      for t in (region + rows_e, (tiles_e * EXPERT_TILE - rows_e) // RUN_ALIGN))

    xs = _dispatch_call(tabs, tail_tabs, h2p, lpos, n_tiles * EXPERT_TILE)
    ys = _expert_call(tile_e, tile_v, xs, p)
    return _combine_call(tabs, x1, lpos, wts, ga2, ys, B, S)


def kernel(x, c, positions, w_ada, b_ada, g_norm1, w_in, g_q_lat, w_uq, g_kv_lat, w_ukv, g_qk_q,
           g_qk_k, w_o_mla, w_o_sb, w_out, g_norm2, w_router, b_router, w_gate_up, b_gate_up,
           w_down, b_down):
    B, S, D = x.shape
    x2 = x.reshape(B * S, D)
    rope = _rope_table(positions)
    for l in range(w_ada.shape[0]):
        x2 = _layer(x2, c, rope, B, S, w_ada[l], b_ada[l], w_in[l], g_norm1[l], g_q_lat[l],
                    w_uq[l], g_kv_lat[l], w_ukv[l], g_qk_q[l], g_qk_k[l], w_o_mla[l], w_o_sb[l],
                    w_out[l], g_norm2[l], w_router[l], b_router[l], w_gate_up[l], b_gate_up[l],
                    w_down[l], b_down[l])
    return x2.reshape(B, S, D)
```

```python
import functools

import jax
import jax.numpy as jnp
import numpy as np
from jax import lax
from jax.experimental import pallas as pl
from jax.experimental.pallas import tpu as pltpu

F32 = jnp.float32
BF16 = jnp.bfloat16
I32 = jnp.int32
U32 = jnp.uint32

D_MODEL = 1024
EPS = 1e-6
CHUNK = 64
MLA_HEADS = 8
Q_LORA = 384
KV_LORA = 256
QK_NOPE = 64
QK_ROPE = 32
V_DIM = 64
QK_DIM = QK_NOPE + QK_ROPE
ROPE_THETA = 10000.0
SB_HEADS = 8
SB_DIM = 64
N_EXPERTS = 32
TOP_K = 4
D_EXPERT = D_MODEL
SWIGLU_LIMIT = 7.0
SWIGLU_ALPHA = 1.702

LANES = 128
HEAD_PAD = LANES
TOK_BLOCK = 1024
SUB_ROWS = 256
MLA_BLOCK = 512
SB_BLOCK = 256
ROW_BLOCK = 256
EXPERT_TILE = 512
RUN_ALIGN = 8
RUN_BITS = tuple(2 ** b for b in reversed(range((ROW_BLOCK // RUN_ALIGN).bit_length())))
LOCAL_ROWS = ROW_BLOCK * TOP_K + N_EXPERTS * RUN_ALIGN
SHORT_RUN = 8
TABLE_BLOCKS = 32
FILL_CHUNK = TABLE_BLOCKS * N_EXPERTS
DUMP_OFFSETS = tuple(RUN_ALIGN * sum(RUN_BITS[:l]) for l in range(len(RUN_BITS)))
DUMP_ROWS = RUN_ALIGN * 2 * RUN_BITS[0]
TAIL_BITS = tuple(2 ** b for b in reversed(range((EXPERT_TILE // RUN_ALIGN - 1).bit_length())))
NEG_BIG = -1e30
LOG2E = 1.4426950408889634
MLA_LOGIT_SCALE = QK_DIM ** -0.5 * LOG2E
SB_LOGIT_SCALE = SB_DIM ** -0.5 * LOG2E
VMEM_LIMIT = 48 * 1024 * 1024


def _cparams(sem):
    return pltpu.CompilerParams(dimension_semantics=sem, vmem_limit_bytes=VMEM_LIMIT)


def _ada_body(c_ref, w_ref, b_ref, o_ref):
    c = c_ref[...]
    s = c * jax.nn.sigmoid(c)
    o_ref[...] = jnp.dot(s, w_ref[...], preferred_element_type=F32,
                         precision=lax.Precision.HIGHEST) + b_ref[...]


def _ada_call(c, w_ada, b_ada):
    B = c.shape[0]
    n = w_ada.shape[1]
    bn = 512
    return pl.pallas_call(
        _ada_body,
        grid=(n // bn,),
        in_specs=[pl.BlockSpec((B, D_MODEL), lambda j: (0, 0)),
                  pl.BlockSpec((D_MODEL, bn), lambda j: (0, j)),
                  pl.BlockSpec((1, bn), lambda j: (0, j))],
        out_specs=pl.BlockSpec((B, bn), lambda j: (0, j)),
        out_shape=jax.ShapeDtypeStruct((B, n), F32),
        compiler_params=_cparams(("arbitrary",)),
        name="ada_mod",
    )(c, w_ada, b_ada.reshape(1, n))


def _rms(v, width):
    return lax.rsqrt(jnp.sum(v * v, axis=-1, keepdims=True) * (1.0 / width) + EPS)


def _modulated_norm(x, g, sh, sc):
    h = x * _rms(x, D_MODEL) * g
    return h * (1.0 + sc) + sh


def _rope_body(invf_ref, pos_ref, tab_ref):
    pos = pos_ref[...]
    half = QK_ROPE // 2
    tab_ref[...] = jnp.zeros_like(tab_ref)
    for f in range(half):
        ang = pos * invf_ref[f]
        cosf = jnp.cos(ang)
        sinf = jnp.sin(ang)
        tab_ref[f] = sinf
        tab_ref[half + f] = sinf
        tab_ref[QK_NOPE + f] = cosf
        tab_ref[QK_NOPE + half + f] = cosf


def _rope_table(positions):
    T = positions.size
    assert T % (8 * LANES) == 0
    inv_freq = 1.0 / (ROPE_THETA ** (jnp.arange(0, QK_ROPE, 2, dtype=F32) / QK_ROPE))
    pos = positions.astype(F32).reshape(T // LANES, LANES)
    rb = max(r for r in (8, 16, 32, 64) if (T // LANES) % r == 0)
    table = pl.pallas_call(
        _rope_body,
        grid_spec=pltpu.PrefetchScalarGridSpec(
            num_scalar_prefetch=1,
            grid=(T // LANES // rb,),
            in_specs=[pl.BlockSpec((rb, LANES), lambda i, f: (i, 0))],
            out_specs=pl.BlockSpec((HEAD_PAD, rb, LANES), lambda i, f: (0, i, 0))),
        out_shape=jax.ShapeDtypeStruct((HEAD_PAD, T // LANES, LANES), F32),
        compiler_params=_cparams(("arbitrary",)),
        name="rope_table",
    )(inv_freq, pos)
    return table.reshape(HEAD_PAD, T).T


def _proj_body(x_ref, sh_ref, sc_ref, rope_ref, gn_ref, wql_ref, wkvl_ref, wkpe_ref,
               wsq_ref, wsk_ref, wsv_ref, gql_ref, wuq_ref, gkvl_ref, wuk_ref, wv_ref,
               gq_ref, gk_ref,
               q_ref, k_ref, v_ref, sq_ref, sk_ref, sv_ref):
    for r in range(x_ref.shape[0] // SUB_ROWS):
        rows = slice(r * SUB_ROWS, (r + 1) * SUB_ROWS)
        x = x_ref[rows, :]
        h = _modulated_norm(x, gn_ref[...], sh_ref[0], sc_ref[0]).astype(BF16)

        sq_ref[rows, :] = (jnp.dot(h, wsq_ref[...], preferred_element_type=F32) * SB_LOGIT_SCALE).astype(BF16)
        sk_ref[rows, :] = jnp.dot(h, wsk_ref[...], preferred_element_type=F32).astype(BF16)
        sv_ref[rows, :] = jnp.dot(h, wsv_ref[...], preferred_element_type=F32).astype(BF16)

        q_lat = jnp.dot(h, wql_ref[...], preferred_element_type=F32)
        kv_lat = jnp.dot(h, wkvl_ref[...], preferred_element_type=F32)
        kpe = jnp.dot(h, wkpe_ref[...], preferred_element_type=F32)

        qn = (q_lat * _rms(q_lat, Q_LORA) * gql_ref[...]).astype(BF16)
        kvn = (kv_lat * _rms(kv_lat, KV_LORA) * gkvl_ref[...]).astype(BF16)
        q = jnp.dot(qn, wuq_ref[...], preferred_element_type=F32)
        kn = jnp.dot(kvn, wuk_ref[...], preferred_element_type=F32)
        v_ref[rows, :] = jnp.dot(kvn, wv_ref[...], preferred_element_type=F32).astype(BF16)

        lane = lax.broadcasted_iota(I32, (SUB_ROWS, LANES), 1)
        tbl = rope_ref[rows, :]
        sinf = pltpu.roll(tbl, QK_NOPE, 1)
        half = QK_ROPE // 2
        cosf = jnp.where((lane >= QK_NOPE) & (lane < QK_DIM), tbl, 1.0)
        s_lo = jnp.where((lane >= QK_NOPE) & (lane < QK_NOPE + half), -sinf, 0.0)
        s_hi = jnp.where((lane >= QK_NOPE + half) & (lane < QK_DIM), sinf, 0.0)

        def rope(t):
            return (t * cosf + pltpu.roll(t, LANES - half, 1) * s_lo
                    + pltpu.roll(t, half, 1) * s_hi)

        gq = gq_ref[...] * MLA_LOGIT_SCALE
        gk = gk_ref[...]
        kpe_sq = jnp.sum(kpe * kpe, axis=-1, keepdims=True)
        kr = rope(kpe * gk)
        for hh in range(MLA_HEADS):
            sl = slice(hh * HEAD_PAD, (hh + 1) * HEAD_PAD)
            qh = q[:, sl]
            q_ref[rows, sl] = rope(qh * _rms(qh, QK_DIM) * gq).astype(BF16)
            kh = kn[:, sl]
            rk = lax.rsqrt((jnp.sum(kh * kh, axis=-1, keepdims=True) + kpe_sq) * (1.0 / QK_DIM) + EPS)
            k_ref[rows, sl] = ((kh * gk + kr) * rk).astype(BF16)


def _proj_call(x2, sh1, sc1, rope, p, B, S):
    T = B * S
    tm = TOK_BLOCK
    nb = S // tm
    row = lambda i: (i, 0)
    per_b = lambda i: (i // nb, 0, 0)
    full = lambda i: (0, 0)

    def wspec(a):
        return pl.BlockSpec(a.shape, full)

    weights = [p["g_norm1"], p["w_ql"], p["w_kvl"], p["w_kpe"], p["w_sq"], p["w_sk"], p["w_sv"],
               p["g_q_lat"], p["w_uq"], p["g_kv_lat"], p["w_uk"], p["w_v"], p["g_qk_q"],
               p["g_qk_k"]]
    out_w = [MLA_HEADS * HEAD_PAD, MLA_HEADS * HEAD_PAD, MLA_HEADS * V_DIM,
             SB_HEADS * SB_DIM, SB_HEADS * SB_DIM, SB_HEADS * SB_DIM]
    return pl.pallas_call(
        _proj_body,
        grid=(T // tm,),
        in_specs=[pl.BlockSpec((tm, D_MODEL), row),
                  pl.BlockSpec((1, 1, D_MODEL), per_b),
                  pl.BlockSpec((1, 1, D_MODEL), per_b),
                  pl.BlockSpec((tm, LANES), row)] + [wspec(a) for a in weights],
        out_specs=[pl.BlockSpec((tm, w), row) for w in out_w],
        out_shape=[jax.ShapeDtypeStruct((T, w), BF16) for w in out_w],
        compiler_params=_cparams(("arbitrary",)),
        name="mix_proj",
    )(x2, sh1, sc1, rope, *weights)


_NT = (((1,), (1,)), ((), ()))


def _mla_rows(qh, k_ref, v_ref, sl, i, tq):
    kend = (i + 1) * tq
    s = lax.dot_general(qh, k_ref[0:kend, sl], _NT, preferred_element_type=F32)
    row = lax.broadcasted_iota(I32, (tq, tq), 0)
    col = lax.broadcasted_iota(I32, (tq, tq), 1)
    s_diag = jnp.where((col // CHUNK) <= (row // CHUNK), s[:, kend - tq:], NEG_BIG)
    s = s_diag if i == 0 else jnp.concatenate([s[:, :kend - tq], s_diag], axis=1)
    pr = jnp.exp2(s - jnp.max(s, axis=-1, keepdims=True))
    l = jnp.sum(pr, axis=-1, keepdims=True)
    return jnp.dot(pr.astype(BF16), v_ref[0:kend, :], preferred_element_type=F32) / l


def _mla_body(q_ref, k_ref, v_ref, o_ref):
    tq = MLA_BLOCK
    lane = lax.broadcasted_iota(I32, (tq, LANES), 1)
    for ii in range(q_ref.shape[0] // tq):
        rows = slice(ii * tq, (ii + 1) * tq)
        outs = []
        for hh in range(2):
            sl = slice(hh * HEAD_PAD, (hh + 1) * HEAD_PAD)
            outs.append(_mla_rows(q_ref[rows, sl], k_ref, v_ref, sl, ii, tq))
        o_ref[rows, :] = jnp.where(lane < V_DIM, outs[0], outs[1]).astype(BF16)


def _mla_call(q, k, v, B, S):
    q3 = q.reshape(B, S, MLA_HEADS * HEAD_PAD)
    k3 = k.reshape(B, S, MLA_HEADS * HEAD_PAD)
    v3 = v.reshape(B, S, MLA_HEADS * V_DIM)
    pair = lambda b, hp: (b, 0, hp)
    out = pl.pallas_call(
        _mla_body,
        grid=(B, MLA_HEADS // 2),
        in_specs=[pl.BlockSpec((None, S, 2 * HEAD_PAD), pair),
                  pl.BlockSpec((None, S, 2 * HEAD_PAD), pair),
                  pl.BlockSpec((None, S, 2 * V_DIM), pair)],
        out_specs=pl.BlockSpec((None, S, 2 * V_DIM), pair),
        out_shape=jax.ShapeDtypeStruct((B, S, MLA_HEADS * V_DIM), BF16),
        compiler_params=_cparams(("arbitrary", "arbitrary")),
        name="mla_attn",
    )(q3, k3, v3)
    return out.reshape(B * S, MLA_HEADS * V_DIM)


def _sb_rows(qh, k_ref, v_ref, i, tq):
    kend = (i + 1) * tq
    z = lax.dot_general(qh, k_ref[0:kend, :], _NT, preferred_element_type=F32)
    sp = jnp.maximum(z, 0.0) + jnp.log2(1.0 + jnp.exp2(jnp.minimum(z, -z)))
    row = lax.broadcasted_iota(I32, (tq, tq), 0)
    col = lax.broadcasted_iota(I32, (tq, tq), 1)
    strict = col < row
    suffix_ones = jnp.where(row >= col, 1.0, 0.0).astype(BF16)
    later = jnp.zeros((tq, 1), F32)
    a_blocks = [None] * (i + 1)
    for j in range(i, -1, -1):
        cols = slice(j * tq, (j + 1) * tq)
        spj = sp[:, cols]
        if j == i:
            spj = jnp.where(strict, spj, 0.0)
        suf = jnp.dot(spj.astype(BF16), suffix_ones, preferred_element_type=F32)
        a = jnp.exp2(z[:, cols] - suf - later)
        if j == i:
            a = jnp.where(strict, a, 0.0)
        a_blocks[j] = a.astype(BF16)
        later = later + suf[:, 0:1]
    a = a_blocks[0] if i == 0 else jnp.concatenate(a_blocks, axis=1)
    return jnp.dot(a, v_ref[0:kend, :], preferred_element_type=F32)


def _sb_body(q_ref, k_ref, v_ref, o_ref):
    tq = SB_BLOCK
    lane = lax.broadcasted_iota(I32, (tq, LANES), 1)
    for ii in range(q_ref.shape[0] // tq):
        rows = slice(ii * tq, (ii + 1) * tq)
        q = q_ref[rows, :]
        outs = []
        for hh in range(2):
            in_head = (lane >= hh * SB_DIM) & (lane < (hh + 1) * SB_DIM)
            qh = jnp.where(in_head, q, jnp.zeros_like(q))
            outs.append(_sb_rows(qh, k_ref, v_ref, ii, tq))
        o_ref[rows, :] = jnp.where(lane < SB_DIM, outs[0], outs[1]).astype(BF16)


def _sb_call(sq, sk, sv, B, S):
    w = SB_HEADS * SB_DIM
    args = [a.reshape(B, S, w) for a in (sq, sk, sv)]
    pair = pl.BlockSpec((None, S, 2 * SB_DIM), lambda b, hp: (b, 0, hp))
    out = pl.pallas_call(
        _sb_body,
        grid=(B, SB_HEADS // 2),
        in_specs=[pair, pair, pair],
        out_specs=pair,
        out_shape=jax.ShapeDtypeStruct((B, S, w), BF16),
        compiler_params=_cparams(("arbitrary", "arbitrary")),
        name="sb_attn",
    )(*args)
    return out.reshape(B * S, w)


def _pack_pairs(h):
    n = h.shape[1] // 2
    return _pack_halves(h[:, :n], h[:, n:])


def _pack_halves(lo, hi):
    lo = pltpu.bitcast(lo.astype(BF16).astype(F32), U32)
    hi = pltpu.bitcast(hi.astype(BF16).astype(F32), U32)
    return (lo >> 16) | (hi & jnp.uint32(0xFFFF0000))


def _unpack_pairs(w):
    lo = pltpu.bitcast(w << 16, F32).astype(BF16)
    hi = pltpu.bitcast(w & jnp.uint32(0xFFFF0000), F32).astype(BF16)
    return lo, hi


def _merge_body(x_ref, sh1_ref, sc1_ref, ga1_ref, sh2_ref, sc2_ref, om_ref, os_ref,
                gn1_ref, wga_ref, wgb_ref, wom_ref, wos_ref, wout_ref, gn2_ref,
                wrh_ref, wrl_ref, br_ref, tri_ref, su_ref,
                x1_ref, h2_ref, lpos_ref, w_ref, tab_ref, tot_ref, run_ref):
    @pl.when(pl.program_id(0) == 0)
    def _():
        run_ref[...] = jnp.zeros_like(run_ref)

    groups = x_ref.shape[0] // SUB_ROWS
    logits = []
    for r in range(groups):
        rows = slice(r * SUB_ROWS, (r + 1) * SUB_ROWS)
        x = x_ref[rows, :]
        h = _modulated_norm(x, gn1_ref[...], sh1_ref[0], sc1_ref[0]).astype(BF16)
        ga = jax.nn.sigmoid(jnp.dot(h, wga_ref[...], preferred_element_type=F32))
        gb = jax.nn.sigmoid(jnp.dot(h, wgb_ref[...], preferred_element_type=F32))
        merged = (ga * jnp.dot(om_ref[rows, :], wom_ref[...], preferred_element_type=F32)
                  + gb * jnp.dot(os_ref[rows, :], wos_ref[...], preferred_element_type=F32))
        y = jnp.dot(merged.astype(BF16), wout_ref[...], preferred_element_type=F32)
        x1 = x + ga1_ref[0] * y
        x1_ref[rows, :] = x1
        h2 = _modulated_norm(x1, gn2_ref[...], sh2_ref[0], sc2_ref[0])
        h2_ref[rows, :] = _pack_pairs(h2)
        h2_hi = h2.astype(BF16)
        h2_lo = (h2 - h2_hi.astype(F32)).astype(BF16)
        logits.append(jnp.dot(h2_hi, wrh_ref[...], preferred_element_type=F32)
                      + jnp.dot(h2_hi, wrl_ref[...], preferred_element_type=F32)
                      + jnp.dot(h2_lo, wrh_ref[...], preferred_element_type=F32)
                      + br_ref[...])
    lpos, wts, tabs, run = _route_rows(jnp.concatenate(logits, axis=0), tri_ref[...], su_ref[...],
                                       run_ref[...])
    lpos_ref[...] = lpos
    w_ref[...] = wts
    for r in range(groups):
        tab_ref[r] = tabs[r]
    run_ref[...] = run
    tot_ref[...] = run


def _merge_call(x2, mods, o_mla, o_sb, p, B, S):
    assert SUB_ROWS == ROW_BLOCK
    T = B * S
    tm = TOK_BLOCK
    nb = S // tm
    per_step = tm // ROW_BLOCK
    row = lambda i: (i, 0)
    per_b = lambda i: (i // nb, 0, 0)
    full = lambda i: (0, 0)
    tri = jnp.asarray(np.arange(ROW_BLOCK)[:, None] > np.arange(ROW_BLOCK)[None, :], dtype=BF16)
    su = jnp.asarray(np.arange(LANES)[:, None] < np.arange(LANES)[None, :], dtype=BF16)
    weights = [p["g_norm1"], p["w_ga"], p["w_gb"], p["w_o_mla"], p["w_o_sb"], p["w_out"],
               p["g_norm2"], p["w_r_hi"], p["w_r_lo"], p["b_r"], tri, su]
    return pl.pallas_call(
        _merge_body,
        grid=(T // tm,),
        in_specs=[pl.BlockSpec((tm, D_MODEL), row)]
        + [pl.BlockSpec((1, 1, D_MODEL), per_b)] * 5
        + [pl.BlockSpec((tm, MLA_HEADS * V_DIM), row), pl.BlockSpec((tm, SB_HEADS * SB_DIM), row)]
        + [pl.BlockSpec(a.shape, full) for a in weights],
        out_specs=[pl.BlockSpec((tm, D_MODEL), row), pl.BlockSpec((tm, D_MODEL // 2), row),
                   pl.BlockSpec((tm, TOP_K), row), pl.BlockSpec((tm, TOP_K), row),
                   pl.BlockSpec((per_step, 8, LANES), lambda i: (i, 0, 0)),
                   pl.BlockSpec((1, LANES), full)],
        out_shape=[jax.ShapeDtypeStruct((T, D_MODEL), F32),
                   jax.ShapeDtypeStruct((T, D_MODEL // 2), U32),
                   jax.ShapeDtypeStruct((T, TOP_K), I32),
                   jax.ShapeDtypeStruct((T, TOP_K), F32),
                   jax.ShapeDtypeStruct((T // ROW_BLOCK, 8, LANES), I32),
                   jax.ShapeDtypeStruct((1, LANES), F32)],
        scratch_shapes=[pltpu.VMEM((1, LANES), F32)],
        compiler_params=_cparams(("arbitrary",)),
        name="merge_route",
    )(x2, *mods, o_mla, o_sb, *weights)


def _route_rows(v, tri, su, run):
    tm = v.shape[0]
    lane_i = lax.broadcasted_iota(I32, (tm, LANES), 1)
    lane = lane_i.astype(F32)
    vals, idxs = [], []
    for _ in range(TOP_K):
        m = jnp.max(v, axis=-1, keepdims=True)
        idx = jnp.min(jnp.where(v == m, lane, float(LANES)), axis=-1, keepdims=True)
        vals.append(m)
        idxs.append(idx)
        v = jnp.where(lane == idx, NEG_BIG, v)
    ex = [jnp.exp(t - vals[0]) for t in vals]
    denom = ex[0] + ex[1] + ex[2] + ex[3]
    onehots = [lane == idx for idx in idxs]
    oh = jnp.zeros((tm, LANES), F32)
    for o in onehots:
        oh = oh + jnp.where(o, 1.0, 0.0)
    sub = lax.broadcasted_iota(I32, (8, LANES), 0)
    pos, tabs = [], []
    for b in range(tm // ROW_BLOCK):
        oh_b = oh[b * ROW_BLOCK:(b + 1) * ROW_BLOCK]
        units = jnp.floor((jnp.sum(oh_b, axis=0, keepdims=True) + (RUN_ALIGN - 1.0)) * (1.0 / RUN_ALIGN))
        lstart = RUN_ALIGN * jnp.dot(jnp.broadcast_to(units, (8, LANES)).astype(BF16), su,
                                     preferred_element_type=F32)[0:1]
        pos.append(jnp.dot(tri, oh_b.astype(BF16), preferred_element_type=F32) + lstart)
        tab = jnp.where(sub == 0, lstart, jnp.where(sub == 1, units, jnp.where(sub == 2, run, 0.0)))
        tabs.append(tab.astype(I32))
        run = run + RUN_ALIGN * units
    pos = jnp.concatenate(pos, axis=0)
    lp_out = jnp.zeros((tm, LANES), I32)
    w_out = jnp.zeros((tm, LANES), F32)
    for kk in range(TOP_K):
        lp = jnp.sum(jnp.where(onehots[kk], pos, 0.0), axis=-1, keepdims=True)
        lp_out = jnp.where(lane_i == kk, lp.astype(I32), lp_out)
        w_out = jnp.where(lane_i == kk, ex[kk] / denom, w_out)
    return lp_out[:, :TOP_K], w_out[:, :TOP_K], tabs, run


def _run_copies(fn, tabs, base, make_copy, bits=RUN_BITS):
    ls_ref, ds_ref, nu_ref = tabs

    def pieces(levels, units, loc, dst):
        for level, bit in levels:
            rows = bit * RUN_ALIGN
            take = (units & bit) != 0

            @pl.when(take)
            def _():
                src = 0 if ls_ref is None else pl.multiple_of(loc, RUN_ALIGN)
                fn(make_copy(src, pl.multiple_of(dst, RUN_ALIGN), rows, level))

            step = jnp.where(take, rows, 0)
            if ls_ref is not None:
                loc = loc + step
            dst = dst + step
        return loc, dst

    levels = list(enumerate(bits))
    small = [lb for lb in levels if lb[1] < SHORT_RUN]
    large = [lb for lb in levels if lb[1] >= SHORT_RUN]

    def per_expert(e, c):
        loc = 0 if ls_ref is None else ls_ref[base + e]
        dst = ds_ref[base + e]
        units = nu_ref[base + e]
        loc, dst = pieces(small, units, loc, dst)

        @pl.when(units >= SHORT_RUN)
        def _():
            pieces(large, units, loc, dst)

        return c

    lax.fori_loop(0, N_EXPERTS, per_expert, 0)


_TN = (((0,), (0,)), ((), ()))


def _slot_matrix(lpos, values):
    tm = lpos.shape[0]
    col = lax.broadcasted_iota(I32, (tm, LOCAL_ROWS), 1)
    out = jnp.zeros((tm, LOCAL_ROWS), F32)
    for kk in range(TOP_K):
        out = jnp.where(col == lpos[:, kk:kk + 1], values[kk], out)
    return out


def _start(cp):
    cp.start()


def _wait(cp):
    cp.wait()


def _table_base(blk):
    return (blk % TABLE_BLOCKS) * N_EXPERTS


def _fill_copies(units, make_copy):
    off = 0
    for level, bit in enumerate(RUN_BITS):
        rows = bit * RUN_ALIGN
        take = (units & bit) != 0

        @pl.when(take)
        def _():
            make_copy(off, rows, level).start()

        off = off + jnp.where(take, rows, 0)


def _dispatch_body(ls_ref, ds_ref, nu_ref, fu_ref, tds_ref, tnu_ref,
                   h_ref, lpos_ref, xs_hbm, loc_ref, zero_ref, sem, tail_sems):
    blk = pl.program_id(0)
    last = pl.num_programs(0) - 1
    slot = blk % 2
    dump = xs_hbm.shape[0] - DUMP_ROWS
    lo, hi = _unpack_pairs(h_ref[...])
    sel = _slot_matrix(lpos_ref[...], [1.0] * TOP_K).astype(BF16)
    x_lo = lax.dot_general(sel, lo, _TN, preferred_element_type=F32)
    x_hi = lax.dot_general(sel, hi, _TN, preferred_element_type=F32)
    loc_ref[slot] = _pack_halves(x_lo, x_hi)

    def run_copy(loc, dst, rows, level):
        return pltpu.make_async_copy(loc_ref.at[slot, pl.ds(loc, rows)], xs_hbm.at[pl.ds(dst, rows)], sem)

    def fill_copy(off, rows, level):
        return pltpu.make_async_copy(loc_ref.at[slot, pl.ds(0, rows)],
                                     xs_hbm.at[pl.ds(dump + DUMP_OFFSETS[level], rows)], sem)

    def block_wait(s):
        pltpu.make_async_copy(loc_ref.at[s], xs_hbm.at[pl.ds(0, LOCAL_ROWS)], sem).wait()

    def tail_copy(loc, dst, rows, level):
        return pltpu.make_async_copy(zero_ref.at[pl.ds(loc, rows)], xs_hbm.at[pl.ds(dst, rows)],
                                     tail_sems.at[level])

    @pl.when(blk > 0)
    def _():
        block_wait(1 - slot)

    _run_copies(_start, (ls_ref, ds_ref, nu_ref), _table_base(blk), run_copy)
    _fill_copies(fu_ref[blk % FILL_CHUNK], fill_copy)

    @pl.when(blk == last)
    def _():
        block_wait(slot)
        zero_ref[...] = jnp.zeros_like(zero_ref)
        tails = (None, tds_ref, tnu_ref)
        _run_copies(_start, tails, 0, tail_copy, TAIL_BITS)
        _run_copies(_wait, tails, 0, tail_copy, TAIL_BITS)


def _table_specs(shift, nblk):
    def block(i):
        return jnp.clip(i + shift, 0, nblk - 1)

    run = pl.BlockSpec((TABLE_BLOCKS * N_EXPERTS,), lambda i: (block(i) // TABLE_BLOCKS,),
                       memory_space=pltpu.SMEM)
    fill = pl.BlockSpec((FILL_CHUNK,), lambda i: (block(i) // FILL_CHUNK,), memory_space=pltpu.SMEM)
    return [run, run, run, fill]


def _dispatch_call(tabs, tail_tabs, h2p, lpos, n_rows):
    T = h2p.shape[0]
    whole = pl.BlockSpec((TABLE_BLOCKS * N_EXPERTS,), lambda i: (0,), memory_space=pltpu.SMEM)
    return pl.pallas_call(
        _dispatch_body,
        grid=(T // ROW_BLOCK,),
        in_specs=_table_specs(0, T // ROW_BLOCK) + [whole, whole] + [
            pl.BlockSpec((ROW_BLOCK, D_MODEL // 2), lambda i: (i, 0)),
            pl.BlockSpec((ROW_BLOCK, TOP_K), lambda i: (i, 0))],
        out_specs=pl.BlockSpec(memory_space=pl.ANY),
        scratch_shapes=[pltpu.VMEM((2, LOCAL_ROWS, D_MODEL // 2), U32),
                        pltpu.VMEM((TAIL_BITS[0] * RUN_ALIGN, D_MODEL // 2), U32),
                        pltpu.SemaphoreType.DMA(()),
                        pltpu.SemaphoreType.DMA((len(TAIL_BITS),))],
        out_shape=jax.ShapeDtypeStruct((n_rows + DUMP_ROWS, D_MODEL // 2), U32),
        compiler_params=_cparams(("arbitrary",)),
        name="moe_dispatch",
    )(*tabs, *tail_tabs, h2p, lpos)


def _expert_body(te_ref, tv_ref, xs_ref, wgu_ref, bgu_ref, wd_ref, bd_ref, y_ref, wgu_bf, wd_bf):
    i = pl.program_id(0)

    @pl.when((i == 0) | (te_ref[i] != te_ref[jnp.maximum(i - 1, 0)]))
    def _():
        quarter = D_MODEL // 4
        for c in range(4):
            rows = slice(c * quarter, (c + 1) * quarter)
            wgu_bf[rows, :] = wgu_ref[rows, :].astype(BF16)
            wd_bf[rows, :] = wd_ref[rows, :].astype(BF16)

    @pl.when(tv_ref[i] == 1)
    def _():
        x = jnp.concatenate(_unpack_pairs(xs_ref[...]), axis=1)
        gu = jnp.dot(x, wgu_bf[...], preferred_element_type=F32) + bgu_ref[...]
        gate = jnp.minimum(gu[:, :D_EXPERT], SWIGLU_LIMIT)
        up = jnp.clip(gu[:, D_EXPERT:], -SWIGLU_LIMIT, SWIGLU_LIMIT)
        hid = (up + 1.0) * (gate * jax.nn.sigmoid(SWIGLU_ALPHA * gate))
        y = jnp.dot(hid.astype(BF16), wd_bf[...], preferred_element_type=F32) + bd_ref[...]
        y_ref[...] = _pack_pairs(y)

    @pl.when(tv_ref[i] == 0)
    def _():
        y_ref[...] = jnp.zeros_like(y_ref)


def _expert_call(tile_e, tile_v, xs, p):
    tmx = EXPERT_TILE
    n_rows = tile_e.shape[0] * tmx
    by_e = lambda i, te, tv: (te[i], 0, 0)
    return pl.pallas_call(
        _expert_body,
        grid_spec=pltpu.PrefetchScalarGridSpec(
            num_scalar_prefetch=2,
            grid=(n_rows // tmx,),
            in_specs=[pl.BlockSpec((tmx, D_MODEL // 2), lambda i, te, tv: (i * tv[i], 0)),
                      pl.BlockSpec((None, D_MODEL, 2 * D_EXPERT), by_e),
                      pl.BlockSpec((None, 1, 2 * D_EXPERT), by_e),
                      pl.BlockSpec((None, D_EXPERT, D_MODEL), by_e),
                      pl.BlockSpec((None, 1, D_MODEL), by_e)],
            out_specs=pl.BlockSpec((tmx, D_MODEL // 2), lambda i, te, tv: (i, 0)),
            scratch_shapes=[pltpu.VMEM((D_MODEL, 2 * D_EXPERT), BF16),
                            pltpu.VMEM((D_EXPERT, D_MODEL), BF16)]),
        out_shape=jax.ShapeDtypeStruct((n_rows, D_MODEL // 2), U32),
        compiler_params=_cparams(("arbitrary",)),
        name="moe_experts",
    )(tile_e, tile_v, xs, p["w_gate_up"], p["b_gate_up"], p["w_down"], p["b_down"])


def _combine_body(ls_ref, ds_ref, nu_ref, fu_ref, nls_ref, nds_ref, nnu_ref, nfu_ref,
                  x1_ref, lpos_ref, w_ref, ga2_ref, ys_hbm, o_ref, loc_ref, sems):
    blk = pl.program_id(0)
    last = pl.num_programs(0) - 1
    slot = blk % 2

    def fetch(tabs, fill_ref, b, s):
        def run_copy(loc, src, rows, level):
            return pltpu.make_async_copy(ys_hbm.at[pl.ds(src, rows)], loc_ref.at[s, pl.ds(loc, rows)],
                                         sems.at[s])

        units = fill_ref[b % FILL_CHUNK]
        used = LOCAL_ROWS - units * RUN_ALIGN

        def fill_copy(off, rows, level):
            return pltpu.make_async_copy(ys_hbm.at[pl.ds(0, rows)],
                                         loc_ref.at[s, pl.ds(pl.multiple_of(used + off, RUN_ALIGN), rows)],
                                         sems.at[s])

        _run_copies(_start, tabs, _table_base(b), run_copy)
        _fill_copies(units, fill_copy)

    @pl.when(blk == 0)
    def _():
        fetch((ls_ref, ds_ref, nu_ref), fu_ref, blk, slot)

    @pl.when(blk < last)
    def _():
        fetch((nls_ref, nds_ref, nnu_ref), nfu_ref, blk + 1, 1 - slot)

    pltpu.make_async_copy(ys_hbm.at[pl.ds(0, LOCAL_ROWS)], loc_ref.at[slot], sems.at[slot]).wait()

    y_lo, y_hi = _unpack_pairs(loc_ref[slot])
    w = w_ref[...]
    mix = _slot_matrix(lpos_ref[...], [w[:, kk:kk + 1] for kk in range(TOP_K)]).astype(BF16)
    halves = [jnp.dot(mix, y, preferred_element_type=F32) for y in (y_lo, y_hi)]
    o_ref[...] = x1_ref[...] + ga2_ref[0] * jnp.concatenate(halves, axis=1)


def _combine_call(tabs, x1, lpos, wts, ga2, ys, B, S):
    T = B * S
    nb = S // ROW_BLOCK
    row = lambda i: (i, 0)
    return pl.pallas_call(
        _combine_body,
        grid=(T // ROW_BLOCK,),
        in_specs=_table_specs(0, T // ROW_BLOCK) + _table_specs(1, T // ROW_BLOCK) + [
            pl.BlockSpec((ROW_BLOCK, D_MODEL), row),
            pl.BlockSpec((ROW_BLOCK, TOP_K), row),
            pl.BlockSpec((ROW_BLOCK, TOP_K), row),
            pl.BlockSpec((1, 1, D_MODEL), lambda i: (i // nb, 0, 0)),
            pl.BlockSpec(memory_space=pl.ANY)],
        out_specs=pl.BlockSpec((ROW_BLOCK, D_MODEL), row),
        scratch_shapes=[pltpu.VMEM((2, LOCAL_ROWS, D_MODEL // 2), U32),
                        pltpu.SemaphoreType.DMA((2,))],
        out_shape=jax.ShapeDtypeStruct((T, D_MODEL), F32),
        compiler_params=_cparams(("arbitrary",)),
        name="moe_combine",
    )(*tabs, *tabs, x1, lpos, wts, ga2, ys)


def _pad_heads(w, per_head, n_heads):
    k = w.shape[0]
    w = w.reshape(k, n_heads, per_head)
    w = jnp.pad(w, ((0, 0), (0, 0), (0, HEAD_PAD - per_head)))
    return w.reshape(k, n_heads * HEAD_PAD)


def _prepare(w_in, g_norm1, g_q_lat, w_uq, g_kv_lat, w_ukv, g_qk_q, g_qk_k, w_o_mla, w_o_sb,
             w_out, g_norm2, w_router, b_router, w_gate_up, b_gate_up, w_down, b_down):
    c0 = Q_LORA
    c1 = c0 + KV_LORA
    c2 = c1 + QK_ROPE
    sbw = SB_HEADS * SB_DIM
    c3 = c2 + 3 * sbw
    c4 = c3 + D_MODEL
    p = {}
    p["g_norm1"] = g_norm1.reshape(1, D_MODEL)
    p["g_norm2"] = g_norm2.reshape(1, D_MODEL)
    p["w_ql"] = w_in[:, :c0].astype(BF16)
    p["w_kvl"] = w_in[:, c0:c1].astype(BF16)
    p["w_kpe"] = jnp.pad(w_in[:, c1:c2], ((0, 0), (QK_NOPE, HEAD_PAD - QK_DIM))).astype(BF16)
    p["w_sq"] = w_in[:, c2:c2 + sbw].astype(BF16)
    p["w_sk"] = w_in[:, c2 + sbw:c2 + 2 * sbw].astype(BF16)
    p["w_sv"] = w_in[:, c2 + 2 * sbw:c3].astype(BF16)
    p["w_ga"] = w_in[:, c3:c4].astype(BF16)
    p["w_gb"] = w_in[:, c4:].astype(BF16)
    p["g_q_lat"] = g_q_lat.reshape(1, Q_LORA)
    p["g_kv_lat"] = g_kv_lat.reshape(1, KV_LORA)
    p["w_uq"] = _pad_heads(w_uq, QK_DIM, MLA_HEADS).astype(BF16)
    kv = w_ukv.reshape(KV_LORA, MLA_HEADS, QK_NOPE + V_DIM)
    p["w_uk"] = _pad_heads(kv[:, :, :QK_NOPE].reshape(KV_LORA, MLA_HEADS * QK_NOPE),
                           QK_NOPE, MLA_HEADS).astype(BF16)
    p["w_v"] = kv[:, :, QK_NOPE:].reshape(KV_LORA, MLA_HEADS * V_DIM).astype(BF16)
    p["g_qk_q"] = jnp.pad(g_qk_q, (0, HEAD_PAD - QK_DIM)).reshape(1, HEAD_PAD)
    p["g_qk_k"] = jnp.pad(g_qk_k, (0, HEAD_PAD - QK_DIM)).reshape(1, HEAD_PAD)
    p["w_o_mla"] = w_o_mla.astype(BF16)
    p["w_o_sb"] = w_o_sb.astype(BF16)
    p["w_out"] = w_out.astype(BF16)
    wr = jnp.pad(w_router, ((0, 0), (0, LANES - N_EXPERTS)))
    p["w_r_hi"] = wr.astype(BF16)
    p["w_r_lo"] = (wr - p["w_r_hi"].astype(F32)).astype(BF16)
    p["b_r"] = jnp.concatenate([b_router, jnp.full((LANES - N_EXPERTS,), NEG_BIG, F32)]).reshape(1, LANES)
    p["w_gate_up"] = w_gate_up
    p["b_gate_up"] = b_gate_up.reshape(N_EXPERTS, 1, 2 * D_EXPERT)
    p["w_down"] = w_down
    p["b_down"] = b_down.reshape(N_EXPERTS, 1, D_MODEL)
    return p


def _layer(x2, c, rope, B, S, w_ada, b_ada, *layer_weights):
    T = B * S
    p = _prepare(*layer_weights)
    mod = _ada_call(c, w_ada, b_ada)
    sh1, sc1, ga1, sh2, sc2, ga2 = [m.reshape(B, 1, D_MODEL) for m in jnp.split(mod, 6, axis=-1)]

    q, k, v, sq, sk, sv = _proj_call(x2, sh1, sc1, rope, p, B, S)
    o_mla = _mla_call(q, k, v, B, S)
    o_sb = _sb_call(sq, sk, sv, B, S)
    x1, h2p, lpos, wts, tab, totals = _merge_call(x2, (sh1, sc1, ga1, sh2, sc2), o_mla, o_sb, p, B, S)

    nblk = T // ROW_BLOCK
    rows_e = totals[0, :N_EXPERTS].astype(I32)
    tiles_e = (rows_e + EXPERT_TILE - 1) // EXPERT_TILE
    tile_end = jnp.cumsum(tiles_e)
    region = ((tile_end - tiles_e) * EXPERT_TILE).astype(I32)
    n_tiles = (T * TOP_K + nblk * N_EXPERTS * (RUN_ALIGN - 1)) // EXPERT_TILE + N_EXPERTS
    tile_ids = jnp.arange(n_tiles, dtype=I32)
    tile_e = jnp.minimum(jnp.sum((tile_ids[:, None] >= tile_end[None, :]).astype(I32), axis=1),
                         N_EXPERTS - 1).astype(I32)
    tile_v = (tile_ids < tile_end[-1]).astype(I32)
    pad = (-nblk) % TABLE_BLOCKS

    def flat(t):
        return jnp.pad(t, ((0, pad), (0, 0))).reshape(-1)

    units = tab[:, 1, :N_EXPERTS]
    fill_units = LOCAL_ROWS // RUN_ALIGN - jnp.sum(units, axis=1)
    tabs = (flat(tab[:, 0, :N_EXPERTS]), flat(tab[:, 2, :N_EXPERTS] + region[None, :]), flat(units),
            jnp.pad(fill_units, (0, (-nblk) % FILL_CHUNK)))
    table_len = TABLE_BLOCKS * N_EXPERTS
    tail_tabs = tuple(jnp.pad(t, (0, table_len - N_EXPERTS))
                      for t in (region + rows_e, (tiles_e * EXPERT_TILE - rows_e) // RUN_ALIGN))

    xs = _dispatch_call(tabs, tail_tabs, h2p, lpos, n_tiles * EXPERT_TILE)
    ys = _expert_call(tile_e, tile_v, xs, p)
    return _combine_call(tabs, x1, lpos, wts, ga2, ys, B, S)


def kernel(x, c, positions, w_ada, b_ada, g_norm1, w_in, g_q_lat, w_uq, g_kv_lat, w_ukv, g_qk_q,
           g_qk_k, w_o_mla, w_o_sb, w_out, g_norm2, w_router, b_router, w_gate_up, b_gate_up,
           w_down, b_down):
    B, S, D = x.shape
    x2 = x.reshape(B * S, D)
    rope = _rope_table(positions)
    for l in range(w_ada.shape[0]):
        x2 = _layer(x2, c, rope, B, S, w_ada[l], b_ada[l], w_in[l], g_norm1[l], g_q_lat[l],
                    w_uq[l], g_kv_lat[l], w_ukv[l], g_qk_q[l], g_qk_k[l], w_o_mla[l], w_o_sb[l],
                    w_out[l], g_norm2[l], w_router[l], b_router[l], w_gate_up[l], b_gate_up[l],
                    w_down[l], b_down[l])
    return x2.reshape(B, S, D)
```

```python
import functools

import jax
import jax.numpy as jnp
import numpy as np
from jax import lax
from jax.experimental import pallas as pl
from jax.experimental.pallas import tpu as pltpu

F32 = jnp.float32
BF16 = jnp.bfloat16
I32 = jnp.int32
U32 = jnp.uint32

D_MODEL = 1024
EPS = 1e-6
CHUNK = 64
MLA_HEADS = 8
Q_LORA = 384
KV_LORA = 256
QK_NOPE = 64
QK_ROPE = 32
V_DIM = 64
QK_DIM = QK_NOPE + QK_ROPE
ROPE_THETA = 10000.0
SB_HEADS = 8
SB_DIM = 64
N_EXPERTS = 32
TOP_K = 4
D_EXPERT = D_MODEL
SWIGLU_LIMIT = 7.0
SWIGLU_ALPHA = 1.702

LANES = 128
HEAD_PAD = LANES
TOK_BLOCK = 1024
SUB_ROWS = 256
MLA_BLOCK = 512
SB_BLOCK = 256
ROW_BLOCK = 256
EXPERT_TILE = 512
RUN_ALIGN = 8
RUN_BITS = tuple(2 ** b for b in reversed(range((ROW_BLOCK // RUN_ALIGN).bit_length())))
LOCAL_ROWS = ROW_BLOCK * TOP_K + N_EXPERTS * RUN_ALIGN
SHORT_RUN = 8
TABLE_BLOCKS = 32
FILL_CHUNK = TABLE_BLOCKS * N_EXPERTS
DUMP_OFFSETS = tuple(RUN_ALIGN * sum(RUN_BITS[:l]) for l in range(len(RUN_BITS)))
DUMP_ROWS = RUN_ALIGN * 2 * RUN_BITS[0]
TAIL_BITS = tuple(2 ** b for b in reversed(range((EXPERT_TILE // RUN_ALIGN - 1).bit_length())))
NEG_BIG = -1e30
LOG2E = 1.4426950408889634
MLA_LOGIT_SCALE = QK_DIM ** -0.5 * LOG2E
SB_LOGIT_SCALE = SB_DIM ** -0.5 * LOG2E
VMEM_LIMIT = 48 * 1024 * 1024


def _cparams(sem):
    return pltpu.CompilerParams(dimension_semantics=sem, vmem_limit_bytes=VMEM_LIMIT)


def _ada_body(c_ref, w_ref, b_ref, o_ref):
    c = c_ref[...]
    s = c * jax.nn.sigmoid(c)
    o_ref[...] = jnp.dot(s, w_ref[...], preferred_element_type=F32,
                         precision=lax.Precision.HIGHEST) + b_ref[...]


def _ada_call(c, w_ada, b_ada):
    B = c.shape[0]
    n = w_ada.shape[1]
    bn = 512
    return pl.pallas_call(
        _ada_body,
        grid=(n // bn,),
        in_specs=[pl.BlockSpec((B, D_MODEL), lambda j: (0, 0)),
                  pl.BlockSpec((D_MODEL, bn), lambda j: (0, j)),
                  pl.BlockSpec((1, bn), lambda j: (0, j))],
        out_specs=pl.BlockSpec((B, bn), lambda j: (0, j)),
        out_shape=jax.ShapeDtypeStruct((B, n), F32),
        compiler_params=_cparams(("arbitrary",)),
        name="ada_mod",
    )(c, w_ada, b_ada.reshape(1, n))


def _rms(v, width):
    return lax.rsqrt(jnp.sum(v * v, axis=-1, keepdims=True) * (1.0 / width) + EPS)


def _modulated_norm(x, g, sh, sc):
    h = x * _rms(x, D_MODEL) * g
    return h * (1.0 + sc) + sh


def _rope_body(invf_ref, pos_ref, tab_ref):
    pos = pos_ref[...]
    half = QK_ROPE // 2
    tab_ref[...] = jnp.zeros_like(tab_ref)
    for f in range(half):
        ang = pos * invf_ref[f]
        cosf = jnp.cos(ang)
        sinf = jnp.sin(ang)
        tab_ref[f] = sinf
        tab_ref[half + f] = sinf
        tab_ref[QK_NOPE + f] = cosf
        tab_ref[QK_NOPE + half + f] = cosf


def _rope_table(positions):
    T = positions.size
    assert T % (8 * LANES) == 0
    inv_freq = 1.0 / (ROPE_THETA ** (jnp.arange(0, QK_ROPE, 2, dtype=F32) / QK_ROPE))
    pos = positions.astype(F32).reshape(T // LANES, LANES)
    rb = max(r for r in (8, 16, 32, 64) if (T // LANES) % r == 0)
    table = pl.pallas_call(
        _rope_body,
        grid_spec=pltpu.PrefetchScalarGridSpec(
            num_scalar_prefetch=1,
            grid=(T // LANES // rb,),
            in_specs=[pl.BlockSpec((rb, LANES), lambda i, f: (i, 0))],
            out_specs=pl.BlockSpec((HEAD_PAD, rb, LANES), lambda i, f: (0, i, 0))),
        out_shape=jax.ShapeDtypeStruct((HEAD_PAD, T // LANES, LANES), F32),
        compiler_params=_cparams(("arbitrary",)),
        name="rope_table",
    )(inv_freq, pos)
    return table.reshape(HEAD_PAD, T).T


def _proj_body(x_ref, sh_ref, sc_ref, rope_ref, gn_ref, wql_ref, wkvl_ref, wkpe_ref,
               wsq_ref, wsk_ref, wsv_ref, gql_ref, wuq_ref, gkvl_ref, wuk_ref, wv_ref,
               gq_ref, gk_ref,
               q_ref, k_ref, v_ref, sq_ref, sk_ref, sv_ref):
    for r in range(x_ref.shape[0] // SUB_ROWS):
        rows = slice(r * SUB_ROWS, (r + 1) * SUB_ROWS)
        x = x_ref[rows, :]
        h = _modulated_norm(x, gn_ref[...], sh_ref[0], sc_ref[0]).astype(BF16)

        sq_ref[rows, :] = (jnp.dot(h, wsq_ref[...], preferred_element_type=F32) * SB_LOGIT_SCALE).astype(BF16)
        sk_ref[rows, :] = jnp.dot(h, wsk_ref[...], preferred_element_type=F32).astype(BF16)
        sv_ref[rows, :] = jnp.dot(h, wsv_ref[...], preferred_element_type=F32).astype(BF16)

        q_lat = jnp.dot(h, wql_ref[...], preferred_element_type=F32)
        kv_lat = jnp.dot(h, wkvl_ref[...], preferred_element_type=F32)
        kpe = jnp.dot(h, wkpe_ref[...], preferred_element_type=F32)

        qn = (q_lat * _rms(q_lat, Q_LORA) * gql_ref[...]).astype(BF16)
        kvn = (kv_lat * _rms(kv_lat, KV_LORA) * gkvl_ref[...]).astype(BF16)
        q = jnp.dot(qn, wuq_ref[...], preferred_element_type=F32)
        kn = jnp.dot(kvn, wuk_ref[...], preferred_element_type=F32)
        v_ref[rows, :] = jnp.dot(kvn, wv_ref[...], preferred_element_type=F32).astype(BF16)

        lane = lax.broadcasted_iota(I32, (SUB_ROWS, LANES), 1)
        tbl = rope_ref[rows, :]
        sinf = pltpu.roll(tbl, QK_NOPE, 1)
        half = QK_ROPE // 2
        cosf = jnp.where((lane >= QK_NOPE) & (lane < QK_DIM), tbl, 1.0)
        s_lo = jnp.where((lane >= QK_NOPE) & (lane < QK_NOPE + half), -sinf, 0.0)
        s_hi = jnp.where((lane >= QK_NOPE + half) & (lane < QK_DIM), sinf, 0.0)

        def rope(t):
            return (t * cosf + pltpu.roll(t, LANES - half, 1) * s_lo
                    + pltpu.roll(t, half, 1) * s_hi)

        gq = gq_ref[...] * MLA_LOGIT_SCALE
        gk = gk_ref[...]
        kpe_sq = jnp.sum(kpe * kpe, axis=-1, keepdims=True)
        kr = rope(kpe * gk)
        for hh in range(MLA_HEADS):
            sl = slice(hh * HEAD_PAD, (hh + 1) * HEAD_PAD)
            qh = q[:, sl]
            q_ref[rows, sl] = rope(qh * _rms(qh, QK_DIM) * gq).astype(BF16)
            kh = kn[:, sl]
            rk = lax.rsqrt((jnp.sum(kh * kh, axis=-1, keepdims=True) + kpe_sq) * (1.0 / QK_DIM) + EPS)
            k_ref[rows, sl] = ((kh * gk + kr) * rk).astype(BF16)


def _proj_call(x2, sh1, sc1, rope, p, B, S):
    T = B * S
    tm = TOK_BLOCK
    nb = S // tm
    row = lambda i: (i, 0)
    per_b = lambda i: (i // nb, 0, 0)
    full = lambda i: (0, 0)

    def wspec(a):
        return pl.BlockSpec(a.shape, full)

    weights = [p["g_norm1"], p["w_ql"], p["w_kvl"], p["w_kpe"], p["w_sq"], p["w_sk"], p["w_sv"],
               p["g_q_lat"], p["w_uq"], p["g_kv_lat"], p["w_uk"], p["w_v"], p["g_qk_q"],
               p["g_qk_k"]]
    out_w = [MLA_HEADS * HEAD_PAD, MLA_HEADS * HEAD_PAD, MLA_HEADS * V_DIM,
             SB_HEADS * SB_DIM, SB_HEADS * SB_DIM, SB_HEADS * SB_DIM]
    return pl.pallas_call(
        _proj_body,
        grid=(T // tm,),
        in_specs=[pl.BlockSpec((tm, D_MODEL), row),
                  pl.BlockSpec((1, 1, D_MODEL), per_b),
                  pl.BlockSpec((1, 1, D_MODEL), per_b),
                  pl.BlockSpec((tm, LANES), row)] + [wspec(a) for a in weights],
        out_specs=[pl.BlockSpec((tm, w), row) for w in out_w],
        out_shape=[jax.ShapeDtypeStruct((T, w), BF16) for w in out_w],
        compiler_params=_cparams(("arbitrary",)),
        name="mix_proj",
    )(x2, sh1, sc1, rope, *weights)


_NT = (((1,), (1,)), ((), ()))


def _mla_rows(qh, k_ref, v_ref, sl, i, tq):
    kend = (i + 1) * tq
    s = lax.dot_general(qh, k_ref[0:kend, sl], _NT, preferred_element_type=F32)
    row = lax.broadcasted_iota(I32, (tq, tq), 0)
    col = lax.broadcasted_iota(I32, (tq, tq), 1)
    s_diag = jnp.where((col // CHUNK) <= (row // CHUNK), s[:, kend - tq:], NEG_BIG)
    s = s_diag if i == 0 else jnp.concatenate([s[:, :kend - tq], s_diag], axis=1)
    pr = jnp.exp2(s - jnp.max(s, axis=-1, keepdims=True))
    l = jnp.sum(pr, axis=-1, keepdims=True)
    return jnp.dot(pr.astype(BF16), v_ref[0:kend, :], preferred_element_type=F32) / l


def _mla_block(q_ref, k_ref, v_ref, o_ref, ii):
    tq = MLA_BLOCK
    lane = lax.broadcasted_iota(I32, (tq, LANES), 1)
    rows = slice(ii * tq, (ii + 1) * tq)
    outs = []
    for hh in range(2):
        sl = slice(hh * HEAD_PAD, (hh + 1) * HEAD_PAD)
        outs.append(_mla_rows(q_ref[rows, sl], k_ref, v_ref, sl, ii, tq))
    o_ref[rows, :] = jnp.where(lane < V_DIM, outs[0], outs[1]).astype(BF16)


def _sb_rows(qh, k_ref, v_ref, i, tq):
    kend = (i + 1) * tq
    z = lax.dot_general(qh, k_ref[0:kend, :], _NT, preferred_element_type=F32)
    sp = jnp.maximum(z, 0.0) + jnp.log2(1.0 + jnp.exp2(jnp.minimum(z, -z)))
    row = lax.broadcasted_iota(I32, (tq, tq), 0)
    col = lax.broadcasted_iota(I32, (tq, tq), 1)
    strict = col < row
    suffix_ones = jnp.where(row >= col, 1.0, 0.0).astype(BF16)
    later = jnp.zeros((tq, 1), F32)
    a_blocks = [None] * (i + 1)
    for j in range(i, -1, -1):
        cols = slice(j * tq, (j + 1) * tq)
        spj = sp[:, cols]
        if j == i:
            spj = jnp.where(strict, spj, 0.0)
        suf = jnp.dot(spj.astype(BF16), suffix_ones, preferred_element_type=F32)
        a = jnp.exp2(z[:, cols] - suf - later)
        if j == i:
            a = jnp.where(strict, a, 0.0)
        a_blocks[j] = a.astype(BF16)
        later = later + suf[:, 0:1]
    a = a_blocks[0] if i == 0 else jnp.concatenate(a_blocks, axis=1)
    return jnp.dot(a, v_ref[0:kend, :], preferred_element_type=F32)


def _sb_block(q_ref, k_ref, v_ref, o_ref, ii):
    tq = SB_BLOCK
    lane = lax.broadcasted_iota(I32, (tq, LANES), 1)
    rows = slice(ii * tq, (ii + 1) * tq)
    q = q_ref[rows, :]
    outs = []
    for hh in range(2):
        in_head = (lane >= hh * SB_DIM) & (lane < (hh + 1) * SB_DIM)
        qh = jnp.where(in_head, q, jnp.zeros_like(q))
        outs.append(_sb_rows(qh, k_ref, v_ref, ii, tq))
    o_ref[rows, :] = jnp.where(lane < SB_DIM, outs[0], outs[1]).astype(BF16)


def _attn_body(q_ref, k_ref, v_ref, sq_ref, sk_ref, sv_ref, om_ref, os_ref):
    S = q_ref.shape[0]
    per_mla = MLA_BLOCK // SB_BLOCK
    for ii in range(S // MLA_BLOCK):
        _mla_block(q_ref, k_ref, v_ref, om_ref, ii)
        for jj in range(per_mla):
            _sb_block(sq_ref, sk_ref, sv_ref, os_ref, ii * per_mla + jj)


def _attn_call(q, k, v, sq, sk, sv, B, S):
    assert MLA_HEADS == SB_HEADS and V_DIM == SB_DIM and MLA_BLOCK % SB_BLOCK == 0
    wide = MLA_HEADS * HEAD_PAD
    narrow = SB_HEADS * SB_DIM
    args = [a.reshape(B, S, wide) for a in (q, k)] + [a.reshape(B, S, narrow) for a in (v, sq, sk, sv)]
    pair = lambda b, hp: (b, 0, hp)
    wide_spec = pl.BlockSpec((None, S, 2 * HEAD_PAD), pair)
    narrow_spec = pl.BlockSpec((None, S, 2 * SB_DIM), pair)
    o_mla, o_sb = pl.pallas_call(
        _attn_body,
        grid=(B, MLA_HEADS // 2),
        in_specs=[wide_spec, wide_spec] + [narrow_spec] * 4,
        out_specs=[narrow_spec, narrow_spec],
        out_shape=[jax.ShapeDtypeStruct((B, S, narrow), BF16)] * 2,
        compiler_params=_cparams(("arbitrary", "arbitrary")),
        name="attn_pair",
    )(*args)
    return o_mla.reshape(B * S, narrow), o_sb.reshape(B * S, narrow)


def _pack_pairs(h):
    n = h.shape[1] // 2
    return _pack_halves(h[:, :n], h[:, n:])


def _pack_halves(lo, hi):
    lo = pltpu.bitcast(lo.astype(BF16).astype(F32), U32)
    hi = pltpu.bitcast(hi.astype(BF16).astype(F32), U32)
    return (lo >> 16) | (hi & jnp.uint32(0xFFFF0000))


def _unpack_pairs(w):
    lo = pltpu.bitcast(w << 16, F32).astype(BF16)
    hi = pltpu.bitcast(w & jnp.uint32(0xFFFF0000), F32).astype(BF16)
    return lo, hi


def _merge_body(x_ref, sh1_ref, sc1_ref, ga1_ref, sh2_ref, sc2_ref, om_ref, os_ref,
                gn1_ref, wga_ref, wgb_ref, wom_ref, wos_ref, wout_ref, gn2_ref,
                wrh_ref, wrl_ref, br_ref, tri_ref, su_ref,
                x1_ref, h2_ref, lpos_ref, w_ref, tab_ref, tot_ref, run_ref):
    @pl.when(pl.program_id(0) == 0)
    def _():
        run_ref[...] = jnp.zeros_like(run_ref)

    groups = x_ref.shape[0] // SUB_ROWS
    logits = []
    for r in range(groups):
        rows = slice(r * SUB_ROWS, (r + 1) * SUB_ROWS)
        x = x_ref[rows, :]
        h = _modulated_norm(x, gn1_ref[...], sh1_ref[0], sc1_ref[0]).astype(BF16)
        ga = jax.nn.sigmoid(jnp.dot(h, wga_ref[...], preferred_element_type=F32))
        gb = jax.nn.sigmoid(jnp.dot(h, wgb_ref[...], preferred_element_type=F32))
        merged = (ga * jnp.dot(om_ref[rows, :], wom_ref[...], preferred_element_type=F32)
                  + gb * jnp.dot(os_ref[rows, :], wos_ref[...], preferred_element_type=F32))
        y = jnp.dot(merged.astype(BF16), wout_ref[...], preferred_element_type=F32)
        x1 = x + ga1_ref[0] * y
        x1_ref[rows, :] = x1
        h2 = _modulated_norm(x1, gn2_ref[...], sh2_ref[0], sc2_ref[0])
        h2_ref[rows, :] = _pack_pairs(h2)
        h2_hi = h2.astype(BF16)
        h2_lo = (h2 - h2_hi.astype(F32)).astype(BF16)
        logits.append(jnp.dot(h2_hi, wrh_ref[...], preferred_element_type=F32)
                      + jnp.dot(h2_hi, wrl_ref[...], preferred_element_type=F32)
                      + jnp.dot(h2_lo, wrh_ref[...], preferred_element_type=F32)
                      + br_ref[...])
    lpos, wts, tabs, run = _route_rows(jnp.concatenate(logits, axis=0), tri_ref[...], su_ref[...],
                                       run_ref[...])
    lpos_ref[...] = lpos
    w_ref[...] = wts
    for r in range(groups):
        tab_ref[r] = tabs[r]
    run_ref[...] = run
    tot_ref[...] = run


def _merge_call(x2, mods, o_mla, o_sb, p, B, S):
    assert SUB_ROWS == ROW_BLOCK
    T = B * S
    tm = TOK_BLOCK
    nb = S // tm
    per_step = tm // ROW_BLOCK
    row = lambda i: (i, 0)
    per_b = lambda i: (i // nb, 0, 0)
    full = lambda i: (0, 0)
    tri = jnp.asarray(np.arange(ROW_BLOCK)[:, None] > np.arange(ROW_BLOCK)[None, :], dtype=BF16)
    su = jnp.asarray(np.arange(LANES)[:, None] < np.arange(LANES)[None, :], dtype=BF16)
    weights = [p["g_norm1"], p["w_ga"], p["w_gb"], p["w_o_mla"], p["w_o_sb"], p["w_out"],
               p["g_norm2"], p["w_r_hi"], p["w_r_lo"], p["b_r"], tri, su]
    return pl.pallas_call(
        _merge_body,
        grid=(T // tm,),
        in_specs=[pl.BlockSpec((tm, D_MODEL), row)]
        + [pl.BlockSpec((1, 1, D_MODEL), per_b)] * 5
        + [pl.BlockSpec((tm, MLA_HEADS * V_DIM), row), pl.BlockSpec((tm, SB_HEADS * SB_DIM), row)]
        + [pl.BlockSpec(a.shape, full) for a in weights],
        out_specs=[pl.BlockSpec((tm, D_MODEL), row), pl.BlockSpec((tm, D_MODEL // 2), row),
                   pl.BlockSpec((tm, TOP_K), row), pl.BlockSpec((tm, TOP_K), row),
                   pl.BlockSpec((per_step, 8, LANES), lambda i: (i, 0, 0)),
                   pl.BlockSpec((1, LANES), full)],
        out_shape=[jax.ShapeDtypeStruct((T, D_MODEL), F32),
                   jax.ShapeDtypeStruct((T, D_MODEL // 2), U32),
                   jax.ShapeDtypeStruct((T, TOP_K), I32),
                   jax.ShapeDtypeStruct((T, TOP_K), F32),
                   jax.ShapeDtypeStruct((T // ROW_BLOCK, 8, LANES), I32),
                   jax.ShapeDtypeStruct((1, LANES), F32)],
        scratch_shapes=[pltpu.VMEM((1, LANES), F32)],
        compiler_params=_cparams(("arbitrary",)),
        name="merge_route",
    )(x2, *mods, o_mla, o_sb, *weights)


def _route_rows(v, tri, su, run):
    tm = v.shape[0]
    lane_i = lax.broadcasted_iota(I32, (tm, LANES), 1)
    lane = lane_i.astype(F32)
    vals, idxs = [], []
    for _ in range(TOP_K):
        m = jnp.max(v, axis=-1, keepdims=True)
        idx = jnp.min(jnp.where(v == m, lane, float(LANES)), axis=-1, keepdims=True)
        vals.append(m)
        idxs.append(idx)
        v = jnp.where(lane == idx, NEG_BIG, v)
    ex = [jnp.exp(t - vals[0]) for t in vals]
    denom = ex[0] + ex[1] + ex[2] + ex[3]
    onehots = [lane == idx for idx in idxs]
    oh = jnp.zeros((tm, LANES), F32)
    for o in onehots:
        oh = oh + jnp.where(o, 1.0, 0.0)
    sub = lax.broadcasted_iota(I32, (8, LANES), 0)
    pos, tabs = [], []
    for b in range(tm // ROW_BLOCK):
        oh_b = oh[b * ROW_BLOCK:(b + 1) * ROW_BLOCK]
        units = jnp.floor((jnp.sum(oh_b, axis=0, keepdims=True) + (RUN_ALIGN - 1.0)) * (1.0 / RUN_ALIGN))
        lstart = RUN_ALIGN * jnp.dot(jnp.broadcast_to(units, (8, LANES)).astype(BF16), su,
                                     preferred_element_type=F32)[0:1]
        pos.append(jnp.dot(tri, oh_b.astype(BF16), preferred_element_type=F32) + lstart)
        tab = jnp.where(sub == 0, lstart, jnp.where(sub == 1, units, jnp.where(sub == 2, run, 0.0)))
        tabs.append(tab.astype(I32))
        run = run + RUN_ALIGN * units
    pos = jnp.concatenate(pos, axis=0)
    lp_out = jnp.zeros((tm, LANES), I32)
    w_out = jnp.zeros((tm, LANES), F32)
    for kk in range(TOP_K):
        lp = jnp.sum(jnp.where(onehots[kk], pos, 0.0), axis=-1, keepdims=True)
        lp_out = jnp.where(lane_i == kk, lp.astype(I32), lp_out)
        w_out = jnp.where(lane_i == kk, ex[kk] / denom, w_out)
    return lp_out[:, :TOP_K], w_out[:, :TOP_K], tabs, run


def _run_copies(fn, tabs, base, make_copy, bits=RUN_BITS):
    ls_ref, ds_ref, nu_ref = tabs

    def pieces(levels, units, loc, dst):
        for level, bit in levels:
            rows = bit * RUN_ALIGN
            take = (units & bit) != 0

            @pl.when(take)
            def _():
                src = 0 if ls_ref is None else pl.multiple_of(loc, RUN_ALIGN)
                fn(make_copy(src, pl.multiple_of(dst, RUN_ALIGN), rows, level))

            step = jnp.where(take, rows, 0)
            if ls_ref is not None:
                loc = loc + step
            dst = dst + step
        return loc, dst

    levels = list(enumerate(bits))
    small = [lb for lb in levels if lb[1] < SHORT_RUN]
    large = [lb for lb in levels if lb[1] >= SHORT_RUN]

    def per_expert(e, c):
        loc = 0 if ls_ref is None else ls_ref[base + e]
        dst = ds_ref[base + e]
        units = nu_ref[base + e]
        loc, dst = pieces(small, units, loc, dst)

        @pl.when(units >= SHORT_RUN)
        def _():
            pieces(large, units, loc, dst)

        return c

    lax.fori_loop(0, N_EXPERTS, per_expert, 0)


_TN = (((0,), (0,)), ((), ()))


def _slot_matrix(lpos, values):
    tm = lpos.shape[0]
    col = lax.broadcasted_iota(I32, (tm, LOCAL_ROWS), 1)
    out = jnp.zeros((tm, LOCAL_ROWS), F32)
    for kk in range(TOP_K):
        out = jnp.where(col == lpos[:, kk:kk + 1], values[kk], out)
    return out


def _start(cp):
    cp.start()


def _wait(cp):
    cp.wait()


def _table_base(blk):
    return (blk % TABLE_BLOCKS) * N_EXPERTS


def _fill_copies(units, make_copy):
    off = 0
    for level, bit in enumerate(RUN_BITS):
        rows = bit * RUN_ALIGN
        take = (units & bit) != 0

        @pl.when(take)
        def _():
            make_copy(off, rows, level).start()

        off = off + jnp.where(take, rows, 0)


def _dispatch_body(ls_ref, ds_ref, nu_ref, fu_ref, tds_ref, tnu_ref,
                   h_ref, lpos_ref, xs_hbm, loc_ref, zero_ref, sem, tail_sems):
    blk = pl.program_id(0)
    last = pl.num_programs(0) - 1
    slot = blk % 2
    dump = xs_hbm.shape[0] - DUMP_ROWS
    lo, hi = _unpack_pairs(h_ref[...])
    sel = _slot_matrix(lpos_ref[...], [1.0] * TOP_K).astype(BF16)
    x_lo = lax.dot_general(sel, lo, _TN, preferred_element_type=F32)
    x_hi = lax.dot_general(sel, hi, _TN, preferred_element_type=F32)
    loc_ref[slot] = _pack_halves(x_lo, x_hi)

    def run_copy(loc, dst, rows, level):
        return pltpu.make_async_copy(loc_ref.at[slot, pl.ds(loc, rows)], xs_hbm.at[pl.ds(dst, rows)], sem)

    def fill_copy(off, rows, level):
        return pltpu.make_async_copy(loc_ref.at[slot, pl.ds(0, rows)],
                                     xs_hbm.at[pl.ds(dump + DUMP_OFFSETS[level], rows)], sem)

    def block_wait(s):
        pltpu.make_async_copy(loc_ref.at[s], xs_hbm.at[pl.ds(0, LOCAL_ROWS)], sem).wait()

    def tail_copy(loc, dst, rows, level):
        return pltpu.make_async_copy(zero_ref.at[pl.ds(loc, rows)], xs_hbm.at[pl.ds(dst, rows)],
                                     tail_sems.at[level])

    @pl.when(blk > 0)
    def _():
        block_wait(1 - slot)

    _run_copies(_start, (ls_ref, ds_ref, nu_ref), _table_base(blk), run_copy)
    _fill_copies(fu_ref[blk % FILL_CHUNK], fill_copy)

    @pl.when(blk == last)
    def _():
        block_wait(slot)
        zero_ref[...] = jnp.zeros_like(zero_ref)
        tails = (None, tds_ref, tnu_ref)
        _run_copies(_start, tails, 0, tail_copy, TAIL_BITS)
        _run_copies(_wait, tails, 0, tail_copy, TAIL_BITS)


def _table_specs(shift, nblk):
    def block(i):
        return jnp.clip(i + shift, 0, nblk - 1)

    run = pl.BlockSpec((TABLE_BLOCKS * N_EXPERTS,), lambda i: (block(i) // TABLE_BLOCKS,),
                       memory_space=pltpu.SMEM)
    fill = pl.BlockSpec((FILL_CHUNK,), lambda i: (block(i) // FILL_CHUNK,), memory_space=pltpu.SMEM)
    return [run, run, run, fill]


def _dispatch_call(tabs, tail_tabs, h2p, lpos, n_rows):
    T = h2p.shape[0]
    whole = pl.BlockSpec((TABLE_BLOCKS * N_EXPERTS,), lambda i: (0,), memory_space=pltpu.SMEM)
    return pl.pallas_call(
        _dispatch_body,
        grid=(T // ROW_BLOCK,),
        in_specs=_table_specs(0, T // ROW_BLOCK) + [whole, whole] + [
            pl.BlockSpec((ROW_BLOCK, D_MODEL // 2), lambda i: (i, 0)),
            pl.BlockSpec((ROW_BLOCK, TOP_K), lambda i: (i, 0))],
        out_specs=pl.BlockSpec(memory_space=pl.ANY),
        scratch_shapes=[pltpu.VMEM((2, LOCAL_ROWS, D_MODEL // 2), U32),
                        pltpu.VMEM((TAIL_BITS[0] * RUN_ALIGN, D_MODEL // 2), U32),
                        pltpu.SemaphoreType.DMA(()),
                        pltpu.SemaphoreType.DMA((len(TAIL_BITS),))],
        out_shape=jax.ShapeDtypeStruct((n_rows + DUMP_ROWS, D_MODEL // 2), U32),
        compiler_params=_cparams(("arbitrary",)),
        name="moe_dispatch",
    )(*tabs, *tail_tabs, h2p, lpos)


def _expert_body(te_ref, tv_ref, xs_ref, wgu_ref, bgu_ref, wd_ref, bd_ref, y_ref, wgu_bf, wd_bf):
    i = pl.program_id(0)

    @pl.when((i == 0) | (te_ref[i] != te_ref[jnp.maximum(i - 1, 0)]))
    def _():
        quarter = D_MODEL // 4
        for c in range(4):
            rows = slice(c * quarter, (c + 1) * quarter)
            wgu_bf[rows, :] = wgu_ref[rows, :].astype(BF16)
            wd_bf[rows, :] = wd_ref[rows, :].astype(BF16)

    @pl.when(tv_ref[i] == 1)
    def _():
        x = jnp.concatenate(_unpack_pairs(xs_ref[...]), axis=1)
        gu = jnp.dot(x, wgu_bf[...], preferred_element_type=F32) + bgu_ref[...]
        gate = jnp.minimum(gu[:, :D_EXPERT], SWIGLU_LIMIT)
        up = jnp.clip(gu[:, D_EXPERT:], -SWIGLU_LIMIT, SWIGLU_LIMIT)
        hid = (up + 1.0) * (gate * jax.nn.sigmoid(SWIGLU_ALPHA * gate))
        y = jnp.dot(hid.astype(BF16), wd_bf[...], preferred_element_type=F32) + bd_ref[...]
        y_ref[...] = _pack_pairs(y)

    @pl.when(tv_ref[i] == 0)
    def _():
        y_ref[...] = jnp.zeros_like(y_ref)


def _expert_call(tile_e, tile_v, xs, p):
    tmx = EXPERT_TILE
    n_rows = tile_e.shape[0] * tmx
    by_e = lambda i, te, tv: (te[i], 0, 0)
    return pl.pallas_call(
        _expert_body,
        grid_spec=pltpu.PrefetchScalarGridSpec(
            num_scalar_prefetch=2,
            grid=(n_rows // tmx,),
            in_specs=[pl.BlockSpec((tmx, D_MODEL // 2), lambda i, te, tv: (i * tv[i], 0)),
                      pl.BlockSpec((None, D_MODEL, 2 * D_EXPERT), by_e),
                      pl.BlockSpec((None, 1, 2 * D_EXPERT), by_e),
                      pl.BlockSpec((None, D_EXPERT, D_MODEL), by_e),
                      pl.BlockSpec((None, 1, D_MODEL), by_e)],
            out_specs=pl.BlockSpec((tmx, D_MODEL // 2), lambda i, te, tv: (i, 0)),
            scratch_shapes=[pltpu.VMEM((D_MODEL, 2 * D_EXPERT), BF16),
                            pltpu.VMEM((D_EXPERT, D_MODEL), BF16)]),
        out_shape=jax.ShapeDtypeStruct((n_rows, D_MODEL // 2), U32),
        compiler_params=_cparams(("arbitrary",)),
        name="moe_experts",
    )(tile_e, tile_v, xs, p["w_gate_up"], p["b_gate_up"], p["w_down"], p["b_down"])


def _combine_body(ls_ref, ds_ref, nu_ref, fu_ref, nls_ref, nds_ref, nnu_ref, nfu_ref,
                  x1_ref, lpos_ref, w_ref, ga2_ref, ys_hbm, o_ref, loc_ref, sems):
    blk = pl.program_id(0)
    last = pl.num_programs(0) - 1
    slot = blk % 2

    def fetch(tabs, fill_ref, b, s):
        def run_copy(loc, src, rows, level):
            return pltpu.make_async_copy(ys_hbm.at[pl.ds(src, rows)], loc_ref.at[s, pl.ds(loc, rows)],
                                         sems.at[s])

        units = fill_ref[b % FILL_CHUNK]
        used = LOCAL_ROWS - units * RUN_ALIGN

        def fill_copy(off, rows, level):
            return pltpu.make_async_copy(ys_hbm.at[pl.ds(0, rows)],
                                         loc_ref.at[s, pl.ds(pl.multiple_of(used + off, RUN_ALIGN), rows)],
                                         sems.at[s])

        _run_copies(_start, tabs, _table_base(b), run_copy)
        _fill_copies(units, fill_copy)

    @pl.when(blk == 0)
    def _():
        fetch((ls_ref, ds_ref, nu_ref), fu_ref, blk, slot)

    @pl.when(blk < last)
    def _():
        fetch((nls_ref, nds_ref, nnu_ref), nfu_ref, blk + 1, 1 - slot)

    pltpu.make_async_copy(ys_hbm.at[pl.ds(0, LOCAL_ROWS)], loc_ref.at[slot], sems.at[slot]).wait()

    y_lo, y_hi = _unpack_pairs(loc_ref[slot])
    w = w_ref[...]
    mix = _slot_matrix(lpos_ref[...], [w[:, kk:kk + 1] for kk in range(TOP_K)]).astype(BF16)
    halves = [jnp.dot(mix, y, preferred_element_type=F32) for y in (y_lo, y_hi)]
    o_ref[...] = x1_ref[...] + ga2_ref[0] * jnp.concatenate(halves, axis=1)


def _combine_call(tabs, x1, lpos, wts, ga2, ys, B, S):
    T = B * S
    nb = S // ROW_BLOCK
    row = lambda i: (i, 0)
    return pl.pallas_call(
        _combine_body,
        grid=(T // ROW_BLOCK,),
        in_specs=_table_specs(0, T // ROW_BLOCK) + _table_specs(1, T // ROW_BLOCK) + [
            pl.BlockSpec((ROW_BLOCK, D_MODEL), row),
            pl.BlockSpec((ROW_BLOCK, TOP_K), row),
            pl.BlockSpec((ROW_BLOCK, TOP_K), row),
            pl.BlockSpec((1, 1, D_MODEL), lambda i: (i // nb, 0, 0)),
            pl.BlockSpec(memory_space=pl.ANY)],
        out_specs=pl.BlockSpec((ROW_BLOCK, D_MODEL), row),
        scratch_shapes=[pltpu.VMEM((2, LOCAL_ROWS, D_MODEL // 2), U32),
                        pltpu.SemaphoreType.DMA((2,))],
        out_shape=jax.ShapeDtypeStruct((T, D_MODEL), F32),
        compiler_params=_cparams(("arbitrary",)),
        name="moe_combine",
    )(*tabs, *tabs, x1, lpos, wts, ga2, ys)


def _pad_heads(w, per_head, n_heads):
    k = w.shape[0]
    w = w.reshape(k, n_heads, per_head)
    w = jnp.pad(w, ((0, 0), (0, 0), (0, HEAD_PAD - per_head)))
    return w.reshape(k, n_heads * HEAD_PAD)


def _prepare(w_in, g_norm1, g_q_lat, w_uq, g_kv_lat, w_ukv, g_qk_q, g_qk_k, w_o_mla, w_o_sb,
             w_out, g_norm2, w_router, b_router, w_gate_up, b_gate_up, w_down, b_down):
    c0 = Q_LORA
    c1 = c0 + KV_LORA
    c2 = c1 + QK_ROPE
    sbw = SB_HEADS * SB_DIM
    c3 = c2 + 3 * sbw
    c4 = c3 + D_MODEL
    p = {}
    p["g_norm1"] = g_norm1.reshape(1, D_MODEL)
    p["g_norm2"] = g_norm2.reshape(1, D_MODEL)
    p["w_ql"] = w_in[:, :c0].astype(BF16)
    p["w_kvl"] = w_in[:, c0:c1].astype(BF16)
    p["w_kpe"] = jnp.pad(w_in[:, c1:c2], ((0, 0), (QK_NOPE, HEAD_PAD - QK_DIM))).astype(BF16)
    p["w_sq"] = w_in[:, c2:c2 + sbw].astype(BF16)
    p["w_sk"] = w_in[:, c2 + sbw:c2 + 2 * sbw].astype(BF16)
    p["w_sv"] = w_in[:, c2 + 2 * sbw:c3].astype(BF16)
    p["w_ga"] = w_in[:, c3:c4].astype(BF16)
    p["w_gb"] = w_in[:, c4:].astype(BF16)
    p["g_q_lat"] = g_q_lat.reshape(1, Q_LORA)
    p["g_kv_lat"] = g_kv_lat.reshape(1, KV_LORA)
    p["w_uq"] = _pad_heads(w_uq, QK_DIM, MLA_HEADS).astype(BF16)
    kv = w_ukv.reshape(KV_LORA, MLA_HEADS, QK_NOPE + V_DIM)
    p["w_uk"] = _pad_heads(kv[:, :, :QK_NOPE].reshape(KV_LORA, MLA_HEADS * QK_NOPE),
                           QK_NOPE, MLA_HEADS).astype(BF16)
    p["w_v"] = kv[:, :, QK_NOPE:].reshape(KV_LORA, MLA_HEADS * V_DIM).astype(BF16)
    p["g_qk_q"] = jnp.pad(g_qk_q, (0, HEAD_PAD - QK_DIM)).reshape(1, HEAD_PAD)
    p["g_qk_k"] = jnp.pad(g_qk_k, (0, HEAD_PAD - QK_DIM)).reshape(1, HEAD_PAD)
    p["w_o_mla"] = w_o_mla.astype(BF16)
    p["w_o_sb"] = w_o_sb.astype(BF16)
    p["w_out"] = w_out.astype(BF16)
    wr = jnp.pad(w_router, ((0, 0), (0, LANES - N_EXPERTS)))
    p["w_r_hi"] = wr.astype(BF16)
    p["w_r_lo"] = (wr - p["w_r_hi"].astype(F32)).astype(BF16)
    p["b_r"] = jnp.concatenate([b_router, jnp.full((LANES - N_EXPERTS,), NEG_BIG, F32)]).reshape(1, LANES)
    p["w_gate_up"] = w_gate_up
    p["b_gate_up"] = b_gate_up.reshape(N_EXPERTS, 1, 2 * D_EXPERT)
    p["w_down"] = w_down
    p["b_down"] = b_down.reshape(N_EXPERTS, 1, D_MODEL)
    return p


def _layer(x2, c, rope, B, S, w_ada, b_ada, *layer_weights):
    T = B * S
    p = _prepare(*layer_weights)
    mod = _ada_call(c, w_ada, b_ada)
    sh1, sc1, ga1, sh2, sc2, ga2 = [m.reshape(B, 1, D_MODEL) for m in jnp.split(mod, 6, axis=-1)]

    q, k, v, sq, sk, sv = _proj_call(x2, sh1, sc1, rope, p, B, S)
    o_mla, o_sb = _attn_call(q, k, v, sq, sk, sv, B, S)
    x1, h2p, lpos, wts, tab, totals = _merge_call(x2, (sh1, sc1, ga1, sh2, sc2), o_mla, o_sb, p, B, S)

    nblk = T // ROW_BLOCK
    rows_e = totals[0, :N_EXPERTS].astype(I32)
    tiles_e = (rows_e + EXPERT_TILE - 1) // EXPERT_TILE
    tile_end = jnp.cumsum(tiles_e)
    region = ((tile_end - tiles_e) * EXPERT_TILE).astype(I32)
    n_tiles = (T * TOP_K + nblk * N_EXPERTS * (RUN_ALIGN - 1)) // EXPERT_TILE + N_EXPERTS
    tile_ids = jnp.arange(n_tiles, dtype=I32)
    tile_e = jnp.minimum(jnp.sum((tile_ids[:, None] >= tile_end[None, :]).astype(I32), axis=1),
                         N_EXPERTS - 1).astype(I32)
    tile_v = (tile_ids < tile_end[-1]).astype(I32)
    pad = (-nblk) % TABLE_BLOCKS

    def flat(t):
        return jnp.pad(t, ((0, pad), (0, 0))).reshape(-1)

    units = tab[:, 1, :N_EXPERTS]
    fill_units = LOCAL_ROWS // RUN_ALIGN - jnp.sum(units, axis=1)
    tabs = (flat(tab[:, 0, :N_EXPERTS]), flat(tab[:, 2, :N_EXPERTS] + region[None, :]), flat(units),
            jnp.pad(fill_units, (0, (-nblk) % FILL_CHUNK)))
    table_len = TABLE_BLOCKS * N_EXPERTS
    tail_tabs = tuple(jnp.pad(t, (0, table_len - N_EXPERTS))
                      for t in (region + rows_e, (tiles_e * EXPERT_TILE - rows_e) // RUN_ALIGN))

    xs = _dispatch_call(tabs, tail_tabs, h2p, lpos, n_tiles * EXPERT_TILE)
    ys = _expert_call(tile_e, tile_v, xs, p)
    return _combine_call(tabs, x1, lpos, wts, ga2, ys, B, S)


def kernel(x, c, positions, w_ada, b_ada, g_norm1, w_in, g_q_lat, w_uq, g_kv_lat, w_ukv, g_qk_q,
           g_qk_k, w_o_mla, w_o_sb, w_out, g_norm2, w_router, b_router, w_gate_up, b_gate_up,
           w_down, b_down):
    B, S, D = x.shape
    x2 = x.reshape(B * S, D)
    rope = _rope_table(positions)
    for l in range(w_ada.shape[0]):
        x2 = _layer(x2, c, rope, B, S, w_ada[l], b_ada[l], w_in[l], g_norm1[l], g_q_lat[l],
                    w_uq[l], g_kv_lat[l], w_ukv[l], g_qk_q[l], g_qk_k[l], w_o_mla[l], w_o_sb[l],
                    w_out[l], g_norm2[l], w_router[l], b_router[l], w_gate_up[l], b_gate_up[l],
                    w_down[l], b_down[l])
    return x2.reshape(B, S, D)
```

```python
import jax
import jax.numpy as jnp
import numpy as np
from jax import lax
from jax.experimental import pallas as pl
from jax.experimental.pallas import tpu as pltpu

F32 = jnp.float32
BF16 = jnp.bfloat16
I32 = jnp.int32
U32 = jnp.uint32

D_MODEL = 1024
EPS = 1e-6
CHUNK = 64
MLA_HEADS = 8
Q_LORA = 384
KV_LORA = 256
QK_NOPE = 64
QK_ROPE = 32
V_DIM = 64
QK_DIM = QK_NOPE + QK_ROPE
ROPE_THETA = 10000.0
SB_HEADS = 8
SB_DIM = 64
N_EXPERTS = 32
TOP_K = 4
D_EXPERT = D_MODEL
SWIGLU_LIMIT = 7.0
SWIGLU_ALPHA = 1.702

LANES = 128
HEAD_PAD = LANES
TOK_BLOCK = 1024
SUB_ROWS = 256
MLA_BLOCK = 512
SB_BLOCK = 512
SB_KEY_BLOCK = 256
ROW_BLOCK = 256
EXPERT_TILE = 512
RUN_ALIGN = 8
RUN_BITS = tuple(2 ** b for b in reversed(range((ROW_BLOCK // RUN_ALIGN).bit_length())))
LOCAL_ROWS = ROW_BLOCK * TOP_K + N_EXPERTS * RUN_ALIGN
SHORT_RUN = 8
TABLE_BLOCKS = 32
FILL_CHUNK = TABLE_BLOCKS * N_EXPERTS
DUMP_OFFSETS = tuple(RUN_ALIGN * sum(RUN_BITS[:l]) for l in range(len(RUN_BITS)))
DUMP_ROWS = RUN_ALIGN * 2 * RUN_BITS[0]
TAIL_BITS = tuple(2 ** b for b in reversed(range((EXPERT_TILE // RUN_ALIGN - 1).bit_length())))
NEG_BIG = -1e30
LOG2E = 1.4426950408889634
MLA_LOGIT_SCALE = QK_DIM ** -0.5 * LOG2E
SB_LOGIT_SCALE = SB_DIM ** -0.5 * LOG2E
VMEM_LIMIT = 48 * 1024 * 1024


def _cparams(sem):
    return pltpu.CompilerParams(dimension_semantics=sem, vmem_limit_bytes=VMEM_LIMIT)


def _ada_body(c_ref, w_ref, b_ref, o_ref):
    c = c_ref[...]
    s = c * jax.nn.sigmoid(c)
    o_ref[...] = jnp.dot(s, w_ref[...], preferred_element_type=F32,
                         precision=lax.Precision.HIGHEST) + b_ref[...]


def _ada_call(c, w_ada, b_ada):
    B = c.shape[0]
    n = w_ada.shape[1]
    bn = 512
    return pl.pallas_call(
        _ada_body,
        grid=(n // bn,),
        in_specs=[pl.BlockSpec((B, D_MODEL), lambda j: (0, 0)),
                  pl.BlockSpec((D_MODEL, bn), lambda j: (0, j)),
                  pl.BlockSpec((1, bn), lambda j: (0, j))],
        out_specs=pl.BlockSpec((B, bn), lambda j: (0, j)),
        out_shape=jax.ShapeDtypeStruct((B, n), F32),
        compiler_params=_cparams(("arbitrary",)),
        name="ada_mod",
    )(c, w_ada, b_ada.reshape(1, n))


def _rms(v, width):
    return lax.rsqrt(jnp.sum(v * v, axis=-1, keepdims=True) * (1.0 / width) + EPS)


def _modulated_norm(x, g, sh, sc):
    h = x * _rms(x, D_MODEL) * g
    return h * (1.0 + sc) + sh


def _rope_body(invf_ref, pos_ref, tab_ref):
    pos = pos_ref[...]
    half = QK_ROPE // 2
    tab_ref[...] = jnp.zeros_like(tab_ref)
    for f in range(half):
        ang = pos * invf_ref[f]
        cosf = jnp.cos(ang)
        sinf = jnp.sin(ang)
        tab_ref[f] = sinf
        tab_ref[half + f] = sinf
        tab_ref[QK_NOPE + f] = cosf
        tab_ref[QK_NOPE + half + f] = cosf


def _rope_table(positions):
    T = positions.size
    assert T % (8 * LANES) == 0
    inv_freq = 1.0 / (ROPE_THETA ** (jnp.arange(0, QK_ROPE, 2, dtype=F32) / QK_ROPE))
    pos = positions.astype(F32).reshape(T // LANES, LANES)
    rb = max(r for r in (8, 16, 32, 64) if (T // LANES) % r == 0)
    table = pl.pallas_call(
        _rope_body,
        grid_spec=pltpu.PrefetchScalarGridSpec(
            num_scalar_prefetch=1,
            grid=(T // LANES // rb,),
            in_specs=[pl.BlockSpec((rb, LANES), lambda i, f: (i, 0))],
            out_specs=pl.BlockSpec((HEAD_PAD, rb, LANES), lambda i, f: (0, i, 0))),
        out_shape=jax.ShapeDtypeStruct((HEAD_PAD, T // LANES, LANES), F32),
        compiler_params=_cparams(("arbitrary",)),
        name="rope_table",
    )(inv_freq, pos)
    return table.reshape(HEAD_PAD, T).T


def _proj_body(x_ref, sh_ref, sc_ref, rope_ref, gn_ref, wql_ref, wkvl_ref, wkpe_ref,
               wsq_ref, wsk_ref, wsv_ref, gql_ref, wuq_ref, gkvl_ref, wuk_ref, wv_ref,
               gq_ref, gk_ref,
               q_ref, k_ref, v_ref, sq_ref, sk_ref, sv_ref):
    for r in range(x_ref.shape[0] // SUB_ROWS):
        rows = slice(r * SUB_ROWS, (r + 1) * SUB_ROWS)
        x = x_ref[rows, :]
        h = _modulated_norm(x, gn_ref[...], sh_ref[0], sc_ref[0]).astype(BF16)

        sq_ref[rows, :] = (jnp.dot(h, wsq_ref[...], preferred_element_type=F32) * SB_LOGIT_SCALE).astype(BF16)
        sk_ref[rows, :] = jnp.dot(h, wsk_ref[...], preferred_element_type=F32).astype(BF16)
        sv_ref[rows, :] = jnp.dot(h, wsv_ref[...], preferred_element_type=F32).astype(BF16)

        q_lat = jnp.dot(h, wql_ref[...], preferred_element_type=F32)
        kv_lat = jnp.dot(h, wkvl_ref[...], preferred_element_type=F32)
        kpe = jnp.dot(h, wkpe_ref[...], preferred_element_type=F32)

        qn = (q_lat * _rms(q_lat, Q_LORA) * gql_ref[...]).astype(BF16)
        kvn = (kv_lat * _rms(kv_lat, KV_LORA) * gkvl_ref[...]).astype(BF16)
        q = jnp.dot(qn, wuq_ref[...], preferred_element_type=F32)
        kn = jnp.dot(kvn, wuk_ref[...], preferred_element_type=F32)
        v_ref[rows, :] = jnp.dot(kvn, wv_ref[...], preferred_element_type=F32).astype(BF16)

        lane = lax.broadcasted_iota(I32, (SUB_ROWS, LANES), 1)
        tbl = rope_ref[rows, :]
        sinf = pltpu.roll(tbl, QK_NOPE, 1)
        half = QK_ROPE // 2
        cosf = jnp.where((lane >= QK_NOPE) & (lane < QK_DIM), tbl, 1.0)
        s_lo = jnp.where((lane >= QK_NOPE) & (lane < QK_NOPE + half), -sinf, 0.0)
        s_hi = jnp.where((lane >= QK_NOPE + half) & (lane < QK_DIM), sinf, 0.0)

        def rope(t):
            return (t * cosf + pltpu.roll(t, LANES - half, 1) * s_lo
                    + pltpu.roll(t, half, 1) * s_hi)

        gq = gq_ref[...] * MLA_LOGIT_SCALE
        gk = gk_ref[...]
        kpe_sq = jnp.sum(kpe * kpe, axis=-1, keepdims=True)
        kr = rope(kpe * gk)
        for hh in range(MLA_HEADS):
            sl = slice(hh * HEAD_PAD, (hh + 1) * HEAD_PAD)
            qh = q[:, sl]
            q_ref[rows, sl] = rope(qh * _rms(qh, QK_DIM) * gq).astype(BF16)
            kh = kn[:, sl]
            rk = lax.rsqrt((jnp.sum(kh * kh, axis=-1, keepdims=True) + kpe_sq) * (1.0 / QK_DIM) + EPS)
            k_ref[rows, sl] = ((kh * gk + kr) * rk).astype(BF16)


def _proj_call(x2, sh1, sc1, rope, p, B, S):
    T = B * S
    tm = TOK_BLOCK
    nb = S // tm
    row = lambda i: (i, 0)
    per_b = lambda i: (i // nb, 0, 0)
    full = lambda i: (0, 0)

    def wspec(a):
        return pl.BlockSpec(a.shape, full)

    weights = [p["g_norm1"], p["w_ql"], p["w_kvl"], p["w_kpe"], p["w_sq"], p["w_sk"], p["w_sv"],
               p["g_q_lat"], p["w_uq"], p["g_kv_lat"], p["w_uk"], p["w_v"], p["g_qk_q"],
               p["g_qk_k"]]
    out_w = [MLA_HEADS * HEAD_PAD, MLA_HEADS * HEAD_PAD, MLA_HEADS * V_DIM,
             SB_HEADS * SB_DIM, SB_HEADS * SB_DIM, SB_HEADS * SB_DIM]
    return pl.pallas_call(
        _proj_body,
        grid=(T // tm,),
        in_specs=[pl.BlockSpec((tm, D_MODEL), row),
                  pl.BlockSpec((1, 1, D_MODEL), per_b),
                  pl.BlockSpec((1, 1, D_MODEL), per_b),
                  pl.BlockSpec((tm, LANES), row)] + [wspec(a) for a in weights],
        out_specs=[pl.BlockSpec((tm, w), row) for w in out_w],
        out_shape=[jax.ShapeDtypeStruct((T, w), BF16) for w in out_w],
        compiler_params=_cparams(("arbitrary",)),
        name="mix_proj",
    )(x2, sh1, sc1, rope, *weights)


_NT = (((1,), (1,)), ((), ()))


def _mla_rows(qh, k_ref, v_ref, sl, i, tq):
    kend = (i + 1) * tq
    s = lax.dot_general(qh, k_ref[0:kend, sl], _NT, preferred_element_type=F32)
    row = lax.broadcasted_iota(I32, (tq, tq), 0)
    col = lax.broadcasted_iota(I32, (tq, tq), 1)
    s_diag = jnp.where((col // CHUNK) <= (row // CHUNK), s[:, kend - tq:], NEG_BIG)
    s = s_diag if i == 0 else jnp.concatenate([s[:, :kend - tq], s_diag], axis=1)
    pr = jnp.exp2(s - jnp.max(s, axis=-1, keepdims=True))
    l = jnp.sum(pr, axis=-1, keepdims=True)
    return jnp.dot(pr.astype(BF16), v_ref[0:kend, :], preferred_element_type=F32) / l


def _mla_block(q_ref, k_ref, v_ref, o_ref, ii):
    tq = MLA_BLOCK
    lane = lax.broadcasted_iota(I32, (tq, LANES), 1)
    rows = slice(ii * tq, (ii + 1) * tq)
    outs = []
    for hh in range(2):
        sl = slice(hh * HEAD_PAD, (hh + 1) * HEAD_PAD)
        outs.append(_mla_rows(q_ref[rows, sl], k_ref, v_ref, sl, ii, tq))
    o_ref[rows, :] = jnp.where(lane < V_DIM, outs[0], outs[1]).astype(BF16)


def _sb_rows(qh, k_ref, v_ref, i, tq, tk):
    kend = (i + 1) * tq
    nk = kend // tk
    z = lax.dot_general(qh, k_ref[0:kend, :], _NT, preferred_element_type=F32)
    row = lax.broadcasted_iota(I32, (tk, tk), 0)
    col = lax.broadcasted_iota(I32, (tk, tk), 1)
    suffix_ones = jnp.where(row >= col, 1.0, 0.0).astype(BF16)
    later = jnp.zeros((tq, 1), F32)
    a_blocks = [None] * nk
    for j in range(nk - 1, -1, -1):
        first = max(j * tk - i * tq, 0)
        diagonal = j * tk >= i * tq
        zj = z[first:, j * tk:(j + 1) * tk]
        spj = jnp.maximum(zj, 0.0) + jnp.log2(1.0 + jnp.exp2(jnp.minimum(zj, -zj)))
        if diagonal:
            visible = (lax.broadcasted_iota(I32, zj.shape, 1) < lax.broadcasted_iota(I32, zj.shape, 0))
            spj = jnp.where(visible, spj, 0.0)
        suf = jnp.dot(spj.astype(BF16), suffix_ones, preferred_element_type=F32)
        a = jnp.exp2(zj - suf - later[first:])
        if diagonal:
            a = jnp.where(visible, a, 0.0)
        a = a.astype(BF16)
        total = suf[:, 0:1]
        if first > 0:
            a = jnp.concatenate([jnp.zeros((first, tk), BF16), a], axis=0)
            total = jnp.concatenate([jnp.zeros((first, 1), F32), total], axis=0)
        a_blocks[j] = a
        later = later + total
    a = a_blocks[0] if nk == 1 else jnp.concatenate(a_blocks, axis=1)
    return jnp.dot(a, v_ref[0:kend, :], preferred_element_type=F32)


def _sb_block(q_ref, k_ref, v_ref, o_ref, ii):
    tq = SB_BLOCK
    lane = lax.broadcasted_iota(I32, (tq, LANES), 1)
    rows = slice(ii * tq, (ii + 1) * tq)
    q = q_ref[rows, :]
    outs = []
    for hh in range(2):
        in_head = (lane >= hh * SB_DIM) & (lane < (hh + 1) * SB_DIM)
        qh = jnp.where(in_head, q, jnp.zeros_like(q))
        outs.append(_sb_rows(qh, k_ref, v_ref, ii, tq, SB_KEY_BLOCK))
    o_ref[rows, :] = jnp.where(lane < SB_DIM, outs[0], outs[1]).astype(BF16)


def _attn_body(q_ref, k_ref, v_ref, sq_ref, sk_ref, sv_ref, om_ref, os_ref):
    S = q_ref.shape[0]
    per_mla = MLA_BLOCK // SB_BLOCK
    for ii in range(S // MLA_BLOCK):
        _mla_block(q_ref, k_ref, v_ref, om_ref, ii)
        for jj in range(per_mla):
            _sb_block(sq_ref, sk_ref, sv_ref, os_ref, ii * per_mla + jj)


def _attn_call(q, k, v, sq, sk, sv, B, S):
    assert MLA_HEADS == SB_HEADS and V_DIM == SB_DIM and MLA_BLOCK % SB_BLOCK == 0
    wide = MLA_HEADS * HEAD_PAD
    narrow = SB_HEADS * SB_DIM
    args = [a.reshape(B, S, wide) for a in (q, k)] + [a.reshape(B, S, narrow) for a in (v, sq, sk, sv)]
    pair = lambda b, hp: (b, 0, hp)
    wide_spec = pl.BlockSpec((None, S, 2 * HEAD_PAD), pair)
    narrow_spec = pl.BlockSpec((None, S, 2 * SB_DIM), pair)
    o_mla, o_sb = pl.pallas_call(
        _attn_body,
        grid=(B, MLA_HEADS // 2),
        in_specs=[wide_spec, wide_spec] + [narrow_spec] * 4,
        out_specs=[narrow_spec, narrow_spec],
        out_shape=[jax.ShapeDtypeStruct((B, S, narrow), BF16)] * 2,
        compiler_params=_cparams(("arbitrary", "arbitrary")),
        name="attn_pair",
    )(*args)
    return o_mla.reshape(B * S, narrow), o_sb.reshape(B * S, narrow)


def _pack_pairs(h):
    n = h.shape[1] // 2
    return _pack_halves(h[:, :n], h[:, n:])


def _pack_halves(lo, hi):
    lo = pltpu.bitcast(lo.astype(BF16).astype(F32), U32)
    hi = pltpu.bitcast(hi.astype(BF16).astype(F32), U32)
    return (lo >> 16) | (hi & jnp.uint32(0xFFFF0000))


def _unpack_pairs(w):
    lo = pltpu.bitcast(w << 16, F32).astype(BF16)
    hi = pltpu.bitcast(w & jnp.uint32(0xFFFF0000), F32).astype(BF16)
    return lo, hi


def _merge_body(x_ref, sh1_ref, sc1_ref, ga1_ref, sh2_ref, sc2_ref, om_ref, os_ref,
                gn1_ref, wga_ref, wgb_ref, wom_ref, wos_ref, wout_ref, gn2_ref,
                wrh_ref, wrl_ref, br_ref, tri_ref, su_ref,
                x1_ref, h2_ref, lpos_ref, w_ref, tab_ref, tot_ref, run_ref):
    @pl.when(pl.program_id(0) == 0)
    def _():
        run_ref[...] = jnp.zeros_like(run_ref)

    groups = x_ref.shape[0] // SUB_ROWS
    logits = []
    for r in range(groups):
        rows = slice(r * SUB_ROWS, (r + 1) * SUB_ROWS)
        x = x_ref[rows, :]
        h = _modulated_norm(x, gn1_ref[...], sh1_ref[0], sc1_ref[0]).astype(BF16)
        ga = jax.nn.sigmoid(jnp.dot(h, wga_ref[...], preferred_element_type=F32))
        gb = jax.nn.sigmoid(jnp.dot(h, wgb_ref[...], preferred_element_type=F32))
        merged = (ga * jnp.dot(om_ref[rows, :], wom_ref[...], preferred_element_type=F32)
                  + gb * jnp.dot(os_ref[rows, :], wos_ref[...], preferred_element_type=F32))
        y = jnp.dot(merged.astype(BF16), wout_ref[...], preferred_element_type=F32)
        x1 = x + ga1_ref[0] * y
        x1_ref[rows, :] = x1
        h2 = _modulated_norm(x1, gn2_ref[...], sh2_ref[0], sc2_ref[0])
        h2_ref[rows, :] = _pack_pairs(h2)
        h2_hi = h2.astype(BF16)
        h2_lo = (h2 - h2_hi.astype(F32)).astype(BF16)
        logits.append(jnp.dot(h2_hi, wrh_ref[...], preferred_element_type=F32)
                      + jnp.dot(h2_hi, wrl_ref[...], preferred_element_type=F32)
                      + jnp.dot(h2_lo, wrh_ref[...], preferred_element_type=F32)
                      + br_ref[...])
    lpos, wts, tabs, run = _route_rows(jnp.concatenate(logits, axis=0), tri_ref[...], su_ref[...],
                                       run_ref[...])
    lpos_ref[...] = lpos
    w_ref[...] = wts
    for r in range(groups):
        tab_ref[r] = tabs[r]
    run_ref[...] = run
    tot_ref[...] = run


def _merge_call(x2, mods, o_mla, o_sb, p, B, S):
    assert SUB_ROWS == ROW_BLOCK
    T = B * S
    tm = TOK_BLOCK
    nb = S // tm
    per_step = tm // ROW_BLOCK
    row = lambda i: (i, 0)
    per_b = lambda i: (i // nb, 0, 0)
    full = lambda i: (0, 0)
    tri = jnp.asarray(np.arange(ROW_BLOCK)[:, None] > np.arange(ROW_BLOCK)[None, :], dtype=BF16)
    su = jnp.asarray(np.arange(LANES)[:, None] < np.arange(LANES)[None, :], dtype=BF16)
    weights = [p["g_norm1"], p["w_ga"], p["w_gb"], p["w_o_mla"], p["w_o_sb"], p["w_out"],
               p["g_norm2"], p["w_r_hi"], p["w_r_lo"], p["b_r"], tri, su]
    return pl.pallas_call(
        _merge_body,
        grid=(T // tm,),
        in_specs=[pl.BlockSpec((tm, D_MODEL), row)]
        + [pl.BlockSpec((1, 1, D_MODEL), per_b)] * 5
        + [pl.BlockSpec((tm, MLA_HEADS * V_DIM), row), pl.BlockSpec((tm, SB_HEADS * SB_DIM), row)]
        + [pl.BlockSpec(a.shape, full) for a in weights],
        out_specs=[pl.BlockSpec((tm, D_MODEL), row), pl.BlockSpec((tm, D_MODEL // 2), row),
                   pl.BlockSpec((tm, TOP_K), row), pl.BlockSpec((tm, TOP_K), row),
                   pl.BlockSpec((per_step, 8, LANES), lambda i: (i, 0, 0)),
                   pl.BlockSpec((1, LANES), full)],
        out_shape=[jax.ShapeDtypeStruct((T, D_MODEL), F32),
                   jax.ShapeDtypeStruct((T, D_MODEL // 2), U32),
                   jax.ShapeDtypeStruct((T, TOP_K), I32),
                   jax.ShapeDtypeStruct((T, TOP_K), F32),
                   jax.ShapeDtypeStruct((T // ROW_BLOCK, 8, LANES), I32),
                   jax.ShapeDtypeStruct((1, LANES), F32)],
        scratch_shapes=[pltpu.VMEM((1, LANES), F32)],
        compiler_params=_cparams(("arbitrary",)),
        name="merge_route",
    )(x2, *mods, o_mla, o_sb, *weights)


def _route_rows(v, tri, su, run):
    tm = v.shape[0]
    lane_i = lax.broadcasted_iota(I32, (tm, LANES), 1)
    lane = lane_i.astype(F32)
    vals, idxs = [], []
    for _ in range(TOP_K):
        m = jnp.max(v, axis=-1, keepdims=True)
        idx = jnp.min(jnp.where(v == m, lane, float(LANES)), axis=-1, keepdims=True)
        vals.append(m)
        idxs.append(idx)
        v = jnp.where(lane == idx, NEG_BIG, v)
    ex = [jnp.exp(t - vals[0]) for t in vals]
    denom = ex[0] + ex[1] + ex[2] + ex[3]
    onehots = [lane == idx for idx in idxs]
    oh = jnp.zeros((tm, LANES), F32)
    for o in onehots:
        oh = oh + jnp.where(o, 1.0, 0.0)
    sub = lax.broadcasted_iota(I32, (8, LANES), 0)
    pos, tabs = [], []
    for b in range(tm // ROW_BLOCK):
        oh_b = oh[b * ROW_BLOCK:(b + 1) * ROW_BLOCK]
        units = jnp.floor((jnp.sum(oh_b, axis=0, keepdims=True) + (RUN_ALIGN - 1.0)) * (1.0 / RUN_ALIGN))
        lstart = RUN_ALIGN * jnp.dot(jnp.broadcast_to(units, (8, LANES)).astype(BF16), su,
                                     preferred_element_type=F32)[0:1]
        pos.append(jnp.dot(tri, oh_b.astype(BF16), preferred_element_type=F32) + lstart)
        tab = jnp.where(sub == 0, lstart, jnp.where(sub == 1, units, jnp.where(sub == 2, run, 0.0)))
        tabs.append(tab.astype(I32))
        run = run + RUN_ALIGN * units
    pos = jnp.concatenate(pos, axis=0)
    lp_out = jnp.zeros((tm, LANES), I32)
    w_out = jnp.zeros((tm, LANES), F32)
    for kk in range(TOP_K):
        lp = jnp.sum(jnp.where(onehots[kk], pos, 0.0), axis=-1, keepdims=True)
        lp_out = jnp.where(lane_i == kk, lp.astype(I32), lp_out)
        w_out = jnp.where(lane_i == kk, ex[kk] / denom, w_out)
    return lp_out[:, :TOP_K], w_out[:, :TOP_K], tabs, run


def _run_copies(fn, tabs, base, make_copy, bits=RUN_BITS):
    ls_ref, ds_ref, nu_ref = tabs

    def pieces(levels, units, loc, dst):
        for level, bit in levels:
            rows = bit * RUN_ALIGN
            take = (units & bit) != 0

            @pl.when(take)
            def _():
                src = 0 if ls_ref is None else pl.multiple_of(loc, RUN_ALIGN)
                fn(make_copy(src, pl.multiple_of(dst, RUN_ALIGN), rows, level))

            step = jnp.where(take, rows, 0)
            if ls_ref is not None:
                loc = loc + step
            dst = dst + step
        return loc, dst

    levels = list(enumerate(bits))
    small = [lb for lb in levels if lb[1] < SHORT_RUN]
    large = [lb for lb in levels if lb[1] >= SHORT_RUN]

    def per_expert(e, c):
        loc = 0 if ls_ref is None else ls_ref[base + e]
        dst = ds_ref[base + e]
        units = nu_ref[base + e]
        loc, dst = pieces(small, units, loc, dst)

        @pl.when(units >= SHORT_RUN)
        def _():
            pieces(large, units, loc, dst)

        return c

    lax.fori_loop(0, N_EXPERTS, per_expert, 0)


_TN = (((0,), (0,)), ((), ()))


def _slot_matrix(lpos, values):
    tm = lpos.shape[0]
    col = lax.broadcasted_iota(I32, (tm, LOCAL_ROWS), 1)
    out = jnp.zeros((tm, LOCAL_ROWS), F32)
    for kk in range(TOP_K):
        out = jnp.where(col == lpos[:, kk:kk + 1], values[kk], out)
    return out


def _start(cp):
    cp.start()


def _wait(cp):
    cp.wait()


def _table_base(blk):
    return (blk % TABLE_BLOCKS) * N_EXPERTS


def _fill_copies(units, make_copy):
    off = 0
    for level, bit in enumerate(RUN_BITS):
        rows = bit * RUN_ALIGN
        take = (units & bit) != 0

        @pl.when(take)
        def _():
            make_copy(off, rows, level).start()

        off = off + jnp.where(take, rows, 0)


def _dispatch_body(ls_ref, ds_ref, nu_ref, fu_ref, tds_ref, tnu_ref,
                   h_ref, lpos_ref, xs_hbm, loc_ref, zero_ref, sem, tail_sems):
    blk = pl.program_id(0)
    last = pl.num_programs(0) - 1
    slot = blk % 2
    dump = xs_hbm.shape[0] - DUMP_ROWS
    lo, hi = _unpack_pairs(h_ref[...])
    sel = _slot_matrix(lpos_ref[...], [1.0] * TOP_K).astype(BF16)
    x_lo = lax.dot_general(sel, lo, _TN, preferred_element_type=F32)
    x_hi = lax.dot_general(sel, hi, _TN, preferred_element_type=F32)
    loc_ref[slot] = _pack_halves(x_lo, x_hi)

    def run_copy(loc, dst, rows, level):
        return pltpu.make_async_copy(loc_ref.at[slot, pl.ds(loc, rows)], xs_hbm.at[pl.ds(dst, rows)], sem)

    def fill_copy(off, rows, level):
        return pltpu.make_async_copy(loc_ref.at[slot, pl.ds(0, rows)],
                                     xs_hbm.at[pl.ds(dump + DUMP_OFFSETS[level], rows)], sem)

    def block_wait(s):
        pltpu.make_async_copy(loc_ref.at[s], xs_hbm.at[pl.ds(0, LOCAL_ROWS)], sem).wait()

    def tail_copy(loc, dst, rows, level):
        return pltpu.make_async_copy(zero_ref.at[pl.ds(loc, rows)], xs_hbm.at[pl.ds(dst, rows)],
                                     tail_sems.at[level])

    @pl.when(blk > 0)
    def _():
        block_wait(1 - slot)

    _run_copies(_start, (ls_ref, ds_ref, nu_ref), _table_base(blk), run_copy)
    _fill_copies(fu_ref[blk % FILL_CHUNK], fill_copy)

    @pl.when(blk == last)
    def _():
        block_wait(slot)
        zero_ref[...] = jnp.zeros_like(zero_ref)
        tails = (None, tds_ref, tnu_ref)
        _run_copies(_start, tails, 0, tail_copy, TAIL_BITS)
        _run_copies(_wait, tails, 0, tail_copy, TAIL_BITS)


def _table_specs(shift, nblk):
    def block(i):
        return jnp.clip(i + shift, 0, nblk - 1)

    run = pl.BlockSpec((TABLE_BLOCKS * N_EXPERTS,), lambda i: (block(i) // TABLE_BLOCKS,),
                       memory_space=pltpu.SMEM)
    fill = pl.BlockSpec((FILL_CHUNK,), lambda i: (block(i) // FILL_CHUNK,), memory_space=pltpu.SMEM)
    return [run, run, run, fill]


def _dispatch_call(tabs, tail_tabs, h2p, lpos, n_rows):
    T = h2p.shape[0]
    whole = pl.BlockSpec((TABLE_BLOCKS * N_EXPERTS,), lambda i: (0,), memory_space=pltpu.SMEM)
    return pl.pallas_call(
        _dispatch_body,
        grid=(T // ROW_BLOCK,),
        in_specs=_table_specs(0, T // ROW_BLOCK) + [whole, whole] + [
            pl.BlockSpec((ROW_BLOCK, D_MODEL // 2), lambda i: (i, 0)),
            pl.BlockSpec((ROW_BLOCK, TOP_K), lambda i: (i, 0))],
        out_specs=pl.BlockSpec(memory_space=pl.ANY),
        scratch_shapes=[pltpu.VMEM((2, LOCAL_ROWS, D_MODEL // 2), U32),
                        pltpu.VMEM((TAIL_BITS[0] * RUN_ALIGN, D_MODEL // 2), U32),
                        pltpu.SemaphoreType.DMA(()),
                        pltpu.SemaphoreType.DMA((len(TAIL_BITS),))],
        out_shape=jax.ShapeDtypeStruct((n_rows + DUMP_ROWS, D_MODEL // 2), U32),
        compiler_params=_cparams(("arbitrary",)),
        name="moe_dispatch",
    )(*tabs, *tail_tabs, h2p, lpos)


def _expert_body(te_ref, tv_ref, xs_ref, wgu_ref, bgu_ref, wd_ref, bd_ref, y_ref, wgu_bf, wd_bf):
    i = pl.program_id(0)

    @pl.when((i == 0) | (te_ref[i] != te_ref[jnp.maximum(i - 1, 0)]))
    def _():
        quarter = D_MODEL // 4
        for c in range(4):
            rows = slice(c * quarter, (c + 1) * quarter)
            wgu_bf[rows, :] = wgu_ref[rows, :].astype(BF16)
            wd_bf[rows, :] = wd_ref[rows, :].astype(BF16)

    @pl.when(tv_ref[i] == 1)
    def _():
        x = jnp.concatenate(_unpack_pairs(xs_ref[...]), axis=1)
        gu = jnp.dot(x, wgu_bf[...], preferred_element_type=F32) + bgu_ref[...]
        gate = jnp.minimum(gu[:, :D_EXPERT], SWIGLU_LIMIT)
        up = jnp.clip(gu[:, D_EXPERT:], -SWIGLU_LIMIT, SWIGLU_LIMIT)
        hid = (up + 1.0) * (gate * jax.nn.sigmoid(SWIGLU_ALPHA * gate))
        y = jnp.dot(hid.astype(BF16), wd_bf[...], preferred_element_type=F32) + bd_ref[...]
        y_ref[...] = _pack_pairs(y)

    @pl.when(tv_ref[i] == 0)
    def _():
        y_ref[...] = jnp.zeros_like(y_ref)


def _expert_call(tile_e, tile_v, xs, p):
    tmx = EXPERT_TILE
    n_rows = tile_e.shape[0] * tmx
    by_e = lambda i, te, tv: (te[i], 0, 0)
    return pl.pallas_call(
        _expert_body,
        grid_spec=pltpu.PrefetchScalarGridSpec(
            num_scalar_prefetch=2,
            grid=(n_rows // tmx,),
            in_specs=[pl.BlockSpec((tmx, D_MODEL // 2), lambda i, te, tv: (i * tv[i], 0)),
                      pl.BlockSpec((None, D_MODEL, 2 * D_EXPERT), by_e),
                      pl.BlockSpec((None, 1, 2 * D_EXPERT), by_e),
                      pl.BlockSpec((None, D_EXPERT, D_MODEL), by_e),
                      pl.BlockSpec((None, 1, D_MODEL), by_e)],
            out_specs=pl.BlockSpec((tmx, D_MODEL // 2), lambda i, te, tv: (i, 0)),
            scratch_shapes=[pltpu.VMEM((D_MODEL, 2 * D_EXPERT), BF16),
                            pltpu.VMEM((D_EXPERT, D_MODEL), BF16)]),
        out_shape=jax.ShapeDtypeStruct((n_rows, D_MODEL // 2), U32),
        compiler_params=_cparams(("arbitrary",)),
        name="moe_experts",
    )(tile_e, tile_v, xs, p["w_gate_up"], p["b_gate_up"], p["w_down"], p["b_down"])


def _combine_body(ls_ref, ds_ref, nu_ref, fu_ref, nls_ref, nds_ref, nnu_ref, nfu_ref,
                  x1_ref, lpos_ref, w_ref, ga2_ref, ys_hbm, o_ref, loc_ref, sems):
    blk = pl.program_id(0)
    last = pl.num_programs(0) - 1
    slot = blk % 2

    def fetch(tabs, fill_ref, b, s):
        def run_copy(loc, src, rows, level):
            return pltpu.make_async_copy(ys_hbm.at[pl.ds(src, rows)], loc_ref.at[s, pl.ds(loc, rows)],
                                         sems.at[s])

        units = fill_ref[b % FILL_CHUNK]
        used = LOCAL_ROWS - units * RUN_ALIGN

        def fill_copy(off, rows, level):
            return pltpu.make_async_copy(ys_hbm.at[pl.ds(0, rows)],
                                         loc_ref.at[s, pl.ds(pl.multiple_of(used + off, RUN_ALIGN), rows)],
                                         sems.at[s])

        _run_copies(_start, tabs, _table_base(b), run_copy)
        _fill_copies(units, fill_copy)

    @pl.when(blk == 0)
    def _():
        fetch((ls_ref, ds_ref, nu_ref), fu_ref, blk, slot)

    @pl.when(blk < last)
    def _():
        fetch((nls_ref, nds_ref, nnu_ref), nfu_ref, blk + 1, 1 - slot)

    pltpu.make_async_copy(ys_hbm.at[pl.ds(0, LOCAL_ROWS)], loc_ref.at[slot], sems.at[slot]).wait()

    y_lo, y_hi = _unpack_pairs(loc_ref[slot])
    w = w_ref[...]
    mix = _slot_matrix(lpos_ref[...], [w[:, kk:kk + 1] for kk in range(TOP_K)]).astype(BF16)
    halves = [jnp.dot(mix, y, preferred_element_type=F32) for y in (y_lo, y_hi)]
    o_ref[...] = x1_ref[...] + ga2_ref[0] * jnp.concatenate(halves, axis=1)


def _combine_call(tabs, x1, lpos, wts, ga2, ys, B, S):
    T = B * S
    nb = S // ROW_BLOCK
    row = lambda i: (i, 0)
    return pl.pallas_call(
        _combine_body,
        grid=(T // ROW_BLOCK,),
        in_specs=_table_specs(0, T // ROW_BLOCK) + _table_specs(1, T // ROW_BLOCK) + [
            pl.BlockSpec((ROW_BLOCK, D_MODEL), row),
            pl.BlockSpec((ROW_BLOCK, TOP_K), row),
            pl.BlockSpec((ROW_BLOCK, TOP_K), row),
            pl.BlockSpec((1, 1, D_MODEL), lambda i: (i // nb, 0, 0)),
            pl.BlockSpec(memory_space=pl.ANY)],
        out_specs=pl.BlockSpec((ROW_BLOCK, D_MODEL), row),
        scratch_shapes=[pltpu.VMEM((2, LOCAL_ROWS, D_MODEL // 2), U32),
                        pltpu.SemaphoreType.DMA((2,))],
        out_shape=jax.ShapeDtypeStruct((T, D_MODEL), F32),
        compiler_params=_cparams(("arbitrary",)),
        name="moe_combine",
    )(*tabs, *tabs, x1, lpos, wts, ga2, ys)


def _pad_heads(w, per_head, n_heads):
    k = w.shape[0]
    w = w.reshape(k, n_heads, per_head)
    w = jnp.pad(w, ((0, 0), (0, 0), (0, HEAD_PAD - per_head)))
    return w.reshape(k, n_heads * HEAD_PAD)


def _prepare(w_in, g_norm1, g_q_lat, w_uq, g_kv_lat, w_ukv, g_qk_q, g_qk_k, w_o_mla, w_o_sb,
             w_out, g_norm2, w_router, b_router, w_gate_up, b_gate_up, w_down, b_down):
    c0 = Q_LORA
    c1 = c0 + KV_LORA
    c2 = c1 + QK_ROPE
    sbw = SB_HEADS * SB_DIM
    c3 = c2 + 3 * sbw
    c4 = c3 + D_MODEL
    p = {}
    p["g_norm1"] = g_norm1.reshape(1, D_MODEL)
    p["g_norm2"] = g_norm2.reshape(1, D_MODEL)
    p["w_ql"] = w_in[:, :c0].astype(BF16)
    p["w_kvl"] = w_in[:, c0:c1].astype(BF16)
    p["w_kpe"] = jnp.pad(w_in[:, c1:c2], ((0, 0), (QK_NOPE, HEAD_PAD - QK_DIM))).astype(BF16)
    p["w_sq"] = w_in[:, c2:c2 + sbw].astype(BF16)
    p["w_sk"] = w_in[:, c2 + sbw:c2 + 2 * sbw].astype(BF16)
    p["w_sv"] = w_in[:, c2 + 2 * sbw:c3].astype(BF16)
    p["w_ga"] = w_in[:, c3:c4].astype(BF16)
    p["w_gb"] = w_in[:, c4:].astype(BF16)
    p["g_q_lat"] = g_q_lat.reshape(1, Q_LORA)
    p["g_kv_lat"] = g_kv_lat.reshape(1, KV_LORA)
    p["w_uq"] = _pad_heads(w_uq, QK_DIM, MLA_HEADS).astype(BF16)
    kv = w_ukv.reshape(KV_LORA, MLA_HEADS, QK_NOPE + V_DIM)
    p["w_uk"] = _pad_heads(kv[:, :, :QK_NOPE].reshape(KV_LORA, MLA_HEADS * QK_NOPE),
                           QK_NOPE, MLA_HEADS).astype(BF16)
    p["w_v"] = kv[:, :, QK_NOPE:].reshape(KV_LORA, MLA_HEADS * V_DIM).astype(BF16)
    p["g_qk_q"] = jnp.pad(g_qk_q, (0, HEAD_PAD - QK_DIM)).reshape(1, HEAD_PAD)
    p["g_qk_k"] = jnp.pad(g_qk_k, (0, HEAD_PAD - QK_DIM)).reshape(1, HEAD_PAD)
    p["w_o_mla"] = w_o_mla.astype(BF16)
    p["w_o_sb"] = w_o_sb.astype(BF16)
    p["w_out"] = w_out.astype(BF16)
    wr = jnp.pad(w_router, ((0, 0), (0, LANES - N_EXPERTS)))
    p["w_r_hi"] = wr.astype(BF16)
    p["w_r_lo"] = (wr - p["w_r_hi"].astype(F32)).astype(BF16)
    p["b_r"] = jnp.concatenate([b_router, jnp.full((LANES - N_EXPERTS,), NEG_BIG, F32)]).reshape(1, LANES)
    p["w_gate_up"] = w_gate_up
    p["b_gate_up"] = b_gate_up.reshape(N_EXPERTS, 1, 2 * D_EXPERT)
    p["w_down"] = w_down
    p["b_down"] = b_down.reshape(N_EXPERTS, 1, D_MODEL)
    return p


def _layer(x2, c, rope, B, S, w_ada, b_ada, *layer_weights):
    T = B * S
    p = _prepare(*layer_weights)
    mod = _ada_call(c, w_ada, b_ada)
    sh1, sc1, ga1, sh2, sc2, ga2 = [m.reshape(B, 1, D_MODEL) for m in jnp.split(mod, 6, axis=-1)]

    q, k, v, sq, sk, sv = _proj_call(x2, sh1, sc1, rope, p, B, S)
    o_mla, o_sb = _attn_call(q, k, v, sq, sk, sv, B, S)
    x1, h2p, lpos, wts, tab, totals = _merge_call(x2, (sh1, sc1, ga1, sh2, sc2), o_mla, o_sb, p, B, S)

    nblk = T // ROW_BLOCK
    rows_e = totals[0, :N_EXPERTS].astype(I32)
    tiles_e = (rows_e + EXPERT_TILE - 1) // EXPERT_TILE
    tile_end = jnp.cumsum(tiles_e)
    region = ((tile_end - tiles_e) * EXPERT_TILE).astype(I32)
    n_tiles = (T * TOP_K + nblk * N_EXPERTS * (RUN_ALIGN - 1)) // EXPERT_TILE + N_EXPERTS
    tile_ids = jnp.arange(n_tiles, dtype=I32)
    tile_e = jnp.minimum(jnp.sum((tile_ids[:, None] >= tile_end[None, :]).astype(I32), axis=1),
                         N_EXPERTS - 1).astype(I32)
    tile_v = (tile_ids < tile_end[-1]).astype(I32)
    pad = (-nblk) % TABLE_BLOCKS

    def flat(t):
        return jnp.pad(t, ((0, pad), (0, 0))).reshape(-1)

    units = tab[:, 1, :N_EXPERTS]
    fill_units = LOCAL_ROWS // RUN_ALIGN - jnp.sum(units, axis=1)
    tabs = (flat(tab[:, 0, :N_EXPERTS]), flat(tab[:, 2, :N_EXPERTS] + region[None, :]), flat(units),
            jnp.pad(fill_units, (0, (-nblk) % FILL_CHUNK)))
    table_len = TABLE_BLOCKS * N_EXPERTS
    tail_tabs = tuple(jnp.pad(t, (0, table_len - N_EXPERTS))
                      for t in (region + rows_e, (tiles_e * EXPERT_TILE - rows_e) // RUN_ALIGN))

    xs = _dispatch_call(tabs, tail_tabs, h2p, lpos, n_tiles * EXPERT_TILE)
    ys = _expert_call(tile_e, tile_v, xs, p)
    return _combine_call(tabs, x1, lpos, wts, ga2, ys, B, S)


def kernel(x, c, positions, w_ada, b_ada, g_norm1, w_in, g_q_lat, w_uq, g_kv_lat, w_ukv, g_qk_q,
           g_qk_k, w_o_mla, w_o_sb, w_out, g_norm2, w_router, b_router, w_gate_up, b_gate_up,
           w_down, b_down):
    B, S, D = x.shape
    x2 = x.reshape(B * S, D)
    rope = _rope_table(positions)
    for l in range(w_ada.shape[0]):
        x2 = _layer(x2, c, rope, B, S, w_ada[l], b_ada[l], w_in[l], g_norm1[l], g_q_lat[l],
                    w_uq[l], g_kv_lat[l], w_ukv[l], g_qk_q[l], g_qk_k[l], w_o_mla[l], w_o_sb[l],
                    w_out[l], g_norm2[l], w_router[l], b_router[l], w_gate_up[l], b_gate_up[l],
                    w_down[l], b_down[l])
    return x2.reshape(B, S, D)
```

```python
import jax
import jax.numpy as jnp
import numpy as np
from jax import lax
from jax.experimental import pallas as pl
from jax.experimental.pallas import tpu as pltpu

F32 = jnp.float32
BF16 = jnp.bfloat16
I32 = jnp.int32
U32 = jnp.uint32

D_MODEL = 1024
EPS = 1e-6
CHUNK = 64
MLA_HEADS = 8
Q_LORA = 384
KV_LORA = 256
QK_NOPE = 64
QK_ROPE = 32
V_DIM = 64
QK_DIM = QK_NOPE + QK_ROPE
ROPE_THETA = 10000.0
SB_HEADS = 8
SB_DIM = 64
N_EXPERTS = 32
TOP_K = 4
D_EXPERT = D_MODEL
SWIGLU_LIMIT = 7.0
SWIGLU_ALPHA = 1.702

LANES = 128
HEAD_PAD = LANES
TOK_BLOCK = 1024
SUB_ROWS = 256
MLA_BLOCK = 512
SB_BLOCK = 512
SB_KEY_BLOCK = 256
ROW_BLOCK = 256
EXPERT_TILE = 512
RUN_ALIGN = 8
RUN_BITS = tuple(2 ** b for b in reversed(range((ROW_BLOCK // RUN_ALIGN).bit_length())))
LOCAL_ROWS = ROW_BLOCK * TOP_K + N_EXPERTS * RUN_ALIGN
SHORT_RUN = 8
TABLE_BLOCKS = 32
FILL_CHUNK = TABLE_BLOCKS * N_EXPERTS
DUMP_OFFSETS = tuple(RUN_ALIGN * sum(RUN_BITS[:l]) for l in range(len(RUN_BITS)))
DUMP_ROWS = RUN_ALIGN * 2 * RUN_BITS[0]
TAIL_BITS = tuple(2 ** b for b in reversed(range((EXPERT_TILE // RUN_ALIGN - 1).bit_length())))
NEG_BIG = -1e30
LOG2E = 1.4426950408889634
MLA_LOGIT_SCALE = QK_DIM ** -0.5 * LOG2E
SB_LOGIT_SCALE = SB_DIM ** -0.5 * LOG2E
VMEM_LIMIT = 48 * 1024 * 1024


def _cparams(sem):
    return pltpu.CompilerParams(dimension_semantics=sem, vmem_limit_bytes=VMEM_LIMIT)


def _ada_body(c_ref, w_ref, b_ref, o_ref):
    c = c_ref[...]
    s = c * jax.nn.sigmoid(c)
    o_ref[...] = jnp.dot(s, w_ref[...], preferred_element_type=F32,
                         precision=lax.Precision.HIGHEST) + b_ref[...]


def _ada_call(c, w_ada, b_ada):
    B = c.shape[0]
    n = w_ada.shape[1]
    bn = 512
    return pl.pallas_call(
        _ada_body,
        grid=(n // bn,),
        in_specs=[pl.BlockSpec((B, D_MODEL), lambda j: (0, 0)),
                  pl.BlockSpec((D_MODEL, bn), lambda j: (0, j)),
                  pl.BlockSpec((1, bn), lambda j: (0, j))],
        out_specs=pl.BlockSpec((B, bn), lambda j: (0, j)),
        out_shape=jax.ShapeDtypeStruct((B, n), F32),
        compiler_params=_cparams(("arbitrary",)),
        name="ada_mod",
    )(c, w_ada, b_ada.reshape(1, n))


def _rms(v, width):
    return lax.rsqrt(jnp.sum(v * v, axis=-1, keepdims=True) * (1.0 / width) + EPS)


def _modulated_norm(x, g, sh, sc):
    h = x * _rms(x, D_MODEL) * g
    return h * (1.0 + sc) + sh


def _rope_body(invf_ref, pos_ref, tab_ref):
    pos = pos_ref[...]
    half = QK_ROPE // 2
    tab_ref[...] = jnp.zeros_like(tab_ref)
    for f in range(half):
        ang = pos * invf_ref[f]
        cosf = jnp.cos(ang)
        sinf = jnp.sin(ang)
        tab_ref[f] = sinf
        tab_ref[half + f] = sinf
        tab_ref[QK_NOPE + f] = cosf
        tab_ref[QK_NOPE + half + f] = cosf


def _rope_table(positions):
    T = positions.size
    assert T % (8 * LANES) == 0
    inv_freq = 1.0 / (ROPE_THETA ** (jnp.arange(0, QK_ROPE, 2, dtype=F32) / QK_ROPE))
    pos = positions.astype(F32).reshape(T // LANES, LANES)
    rb = max(r for r in (8, 16, 32, 64) if (T // LANES) % r == 0)
    table = pl.pallas_call(
        _rope_body,
        grid_spec=pltpu.PrefetchScalarGridSpec(
            num_scalar_prefetch=1,
            grid=(T // LANES // rb,),
            in_specs=[pl.BlockSpec((rb, LANES), lambda i, f: (i, 0))],
            out_specs=pl.BlockSpec((HEAD_PAD, rb, LANES), lambda i, f: (0, i, 0))),
        out_shape=jax.ShapeDtypeStruct((HEAD_PAD, T // LANES, LANES), F32),
        compiler_params=_cparams(("arbitrary",)),
        name="rope_table",
    )(inv_freq, pos)
    return table.reshape(HEAD_PAD, T).T


def _proj_body(x_ref, sh_ref, sc_ref, rope_ref, gn_ref, wql_ref, wkvl_ref, wkpe_ref,
               wsq_ref, wsk_ref, wsv_ref, gql_ref, wuq_ref, gkvl_ref, wuk_ref, wv_ref,
               gq_ref, gk_ref,
               q_ref, k_ref, v_ref, sq_ref, sk_ref, sv_ref):
    for r in range(x_ref.shape[0] // SUB_ROWS):
        rows = slice(r * SUB_ROWS, (r + 1) * SUB_ROWS)
        x = x_ref[rows, :]
        h = _modulated_norm(x, gn_ref[...], sh_ref[0], sc_ref[0]).astype(BF16)

        sq_ref[rows, :] = (jnp.dot(h, wsq_ref[...], preferred_element_type=F32) * SB_LOGIT_SCALE).astype(BF16)
        sk_ref[rows, :] = jnp.dot(h, wsk_ref[...], preferred_element_type=F32).astype(BF16)
        sv_ref[rows, :] = jnp.dot(h, wsv_ref[...], preferred_element_type=F32).astype(BF16)

        q_lat = jnp.dot(h, wql_ref[...], preferred_element_type=F32)
        kv_lat = jnp.dot(h, wkvl_ref[...], preferred_element_type=F32)
        kpe = jnp.dot(h, wkpe_ref[...], preferred_element_type=F32)

        qn = (q_lat * _rms(q_lat, Q_LORA) * gql_ref[...]).astype(BF16)
        kvn = (kv_lat * _rms(kv_lat, KV_LORA) * gkvl_ref[...]).astype(BF16)
        q = jnp.dot(qn, wuq_ref[...], preferred_element_type=F32)
        kn = jnp.dot(kvn, wuk_ref[...], preferred_element_type=F32)
        v_ref[rows, :] = jnp.dot(kvn, wv_ref[...], preferred_element_type=F32).astype(BF16)

        lane = lax.broadcasted_iota(I32, (SUB_ROWS, LANES), 1)
        tbl = rope_ref[rows, :]
        sinf = pltpu.roll(tbl, QK_NOPE, 1)
        half = QK_ROPE // 2
        cosf = jnp.where((lane >= QK_NOPE) & (lane < QK_DIM), tbl, 1.0)
        s_lo = jnp.where((lane >= QK_NOPE) & (lane < QK_NOPE + half), -sinf, 0.0)
        s_hi = jnp.where((lane >= QK_NOPE + half) & (lane < QK_DIM), sinf, 0.0)

        def rope(t):
            return (t * cosf + pltpu.roll(t, LANES - half, 1) * s_lo
                    + pltpu.roll(t, half, 1) * s_hi)

        gq = gq_ref[...] * MLA_LOGIT_SCALE
        gk = gk_ref[...]
        kpe_sq = jnp.sum(kpe * kpe, axis=-1, keepdims=True)
        kr = rope(kpe * gk)
        for hh in range(MLA_HEADS):
            sl = slice(hh * HEAD_PAD, (hh + 1) * HEAD_PAD)
            qh = q[:, sl]
            q_ref[rows, sl] = rope(qh * _rms(qh, QK_DIM) * gq).astype(BF16)
            kh = kn[:, sl]
            rk = lax.rsqrt((jnp.sum(kh * kh, axis=-1, keepdims=True) + kpe_sq) * (1.0 / QK_DIM) + EPS)
            k_ref[rows, sl] = ((kh * gk + kr) * rk).astype(BF16)


def _proj_call(x2, sh1, sc1, rope, p, B, S):
    T = B * S
    tm = TOK_BLOCK
    nb = S // tm
    row = lambda i: (i, 0)
    per_b = lambda i: (i // nb, 0, 0)
    full = lambda i: (0, 0)

    def wspec(a):
        return pl.BlockSpec(a.shape, full)

    weights = [p["g_norm1"], p["w_ql"], p["w_kvl"], p["w_kpe"], p["w_sq"], p["w_sk"], p["w_sv"],
               p["g_q_lat"], p["w_uq"], p["g_kv_lat"], p["w_uk"], p["w_v"], p["g_qk_q"],
               p["g_qk_k"]]
    out_w = [MLA_HEADS * HEAD_PAD, MLA_HEADS * HEAD_PAD, MLA_HEADS * V_DIM,
             SB_HEADS * SB_DIM, SB_HEADS * SB_DIM, SB_HEADS * SB_DIM]
    return pl.pallas_call(
        _proj_body,
        grid=(T // tm,),
        in_specs=[pl.BlockSpec((tm, D_MODEL), row),
                  pl.BlockSpec((1, 1, D_MODEL), per_b),
                  pl.BlockSpec((1, 1, D_MODEL), per_b),
                  pl.BlockSpec((tm, LANES), row)] + [wspec(a) for a in weights],
        out_specs=[pl.BlockSpec((tm, w), row) for w in out_w],
        out_shape=[jax.ShapeDtypeStruct((T, w), BF16) for w in out_w],
        compiler_params=_cparams(("arbitrary",)),
        name="mix_proj",
    )(x2, sh1, sc1, rope, *weights)


_NT = (((1,), (1,)), ((), ()))


def _mla_rows(qh, k_ref, v_ref, sl, hh, i, tq):
    kend = (i + 1) * tq
    s = lax.dot_general(qh, k_ref[0:kend, sl], _NT, preferred_element_type=F32)
    row = lax.broadcasted_iota(I32, (tq, tq), 0)
    col = lax.broadcasted_iota(I32, (tq, tq), 1)
    s_diag = jnp.where((col // CHUNK) <= (row // CHUNK), s[:, kend - tq:], NEG_BIG)
    s = s_diag if i == 0 else jnp.concatenate([s[:, :kend - tq], s_diag], axis=1)
    pr = jnp.exp2(s - jnp.max(s, axis=-1, keepdims=True)).astype(BF16)
    vb = v_ref[0:kend, :].astype(F32)
    lane = lax.broadcasted_iota(I32, vb.shape, 1)
    own = (lane >= hh * V_DIM) & (lane < (hh + 1) * V_DIM)
    acc = jnp.dot(pr, jnp.where(own, vb, 1.0).astype(BF16), preferred_element_type=F32)
    return acc / pltpu.roll(acc, V_DIM, 1)


def _mla_block(q_ref, k_ref, v_ref, o_ref, ii):
    tq = MLA_BLOCK
    lane = lax.broadcasted_iota(I32, (tq, LANES), 1)
    rows = slice(ii * tq, (ii + 1) * tq)
    outs = []
    for hh in range(2):
        sl = slice(hh * HEAD_PAD, (hh + 1) * HEAD_PAD)
        outs.append(_mla_rows(q_ref[rows, sl], k_ref, v_ref, sl, hh, ii, tq))
    o_ref[rows, :] = jnp.where(lane < V_DIM, outs[0], outs[1]).astype(BF16)


def _sb_rows(qh, k_ref, v_ref, i, tq, tk):
    kend = (i + 1) * tq
    nk = kend // tk
    z = lax.dot_general(qh, k_ref[0:kend, :], _NT, preferred_element_type=F32)
    row = lax.broadcasted_iota(I32, (tk, tk), 0)
    col = lax.broadcasted_iota(I32, (tk, tk), 1)
    suffix_ones = jnp.where(row >= col, 1.0, 0.0).astype(BF16)
    later = jnp.zeros((tq, 1), F32)
    a_blocks = [None] * nk
    for j in range(nk - 1, -1, -1):
        first = max(j * tk - i * tq, 0)
        diagonal = j * tk >= i * tq
        zj = z[first:, j * tk:(j + 1) * tk]
        spj = jnp.maximum(zj, 0.0) + jnp.log2(1.0 + jnp.exp2(jnp.minimum(zj, -zj)))
        if diagonal:
            visible = (lax.broadcasted_iota(I32, zj.shape, 1) < lax.broadcasted_iota(I32, zj.shape, 0))
            spj = jnp.where(visible, spj, 0.0)
        suf = jnp.dot(spj.astype(BF16), suffix_ones, preferred_element_type=F32)
        a = jnp.exp2(zj - suf - later[first:])
        if diagonal:
            a = jnp.where(visible, a, 0.0)
        a = a.astype(BF16)
        total = suf[:, 0:1]
        if first > 0:
            a = jnp.concatenate([jnp.zeros((first, tk), BF16), a], axis=0)
            total = jnp.concatenate([jnp.zeros((first, 1), F32), total], axis=0)
        a_blocks[j] = a
        later = later + total
    a = a_blocks[0] if nk == 1 else jnp.concatenate(a_blocks, axis=1)
    return jnp.dot(a, v_ref[0:kend, :], preferred_element_type=F32)


def _sb_block(q_ref, k_ref, v_ref, o_ref, ii):
    tq = SB_BLOCK
    lane = lax.broadcasted_iota(I32, (tq, LANES), 1)
    rows = slice(ii * tq, (ii + 1) * tq)
    q = q_ref[rows, :]
    outs = []
    for hh in range(2):
        in_head = (lane >= hh * SB_DIM) & (lane < (hh + 1) * SB_DIM)
        qh = jnp.where(in_head, q, jnp.zeros_like(q))
        outs.append(_sb_rows(qh, k_ref, v_ref, ii, tq, SB_KEY_BLOCK))
    o_ref[rows, :] = jnp.where(lane < SB_DIM, outs[0], outs[1]).astype(BF16)


def _attn_body(q_ref, k_ref, v_ref, sq_ref, sk_ref, sv_ref, om_ref, os_ref):
    S = q_ref.shape[0]
    per_mla = MLA_BLOCK // SB_BLOCK
    for ii in range(S // MLA_BLOCK):
        _mla_block(q_ref, k_ref, v_ref, om_ref, ii)
        for jj in range(per_mla):
            _sb_block(sq_ref, sk_ref, sv_ref, os_ref, ii * per_mla + jj)


def _attn_call(q, k, v, sq, sk, sv, B, S):
    assert MLA_HEADS == SB_HEADS and V_DIM == SB_DIM and MLA_BLOCK % SB_BLOCK == 0
    wide = MLA_HEADS * HEAD_PAD
    narrow = SB_HEADS * SB_DIM
    args = [a.reshape(B, S, wide) for a in (q, k)] + [a.reshape(B, S, narrow) for a in (v, sq, sk, sv)]
    pair = lambda b, hp: (b, 0, hp)
    wide_spec = pl.BlockSpec((None, S, 2 * HEAD_PAD), pair)
    narrow_spec = pl.BlockSpec((None, S, 2 * SB_DIM), pair)
    o_mla, o_sb = pl.pallas_call(
        _attn_body,
        grid=(B, MLA_HEADS // 2),
        in_specs=[wide_spec, wide_spec] + [narrow_spec] * 4,
        out_specs=[narrow_spec, narrow_spec],
        out_shape=[jax.ShapeDtypeStruct((B, S, narrow), BF16)] * 2,
        compiler_params=_cparams(("arbitrary", "arbitrary")),
        name="attn_pair",
    )(*args)
    return o_mla.reshape(B * S, narrow), o_sb.reshape(B * S, narrow)


def _pack_pairs(h):
    n = h.shape[1] // 2
    return _pack_halves(h[:, :n], h[:, n:])


def _pack_halves(lo, hi):
    lo = pltpu.bitcast(lo.astype(BF16).astype(F32), U32)
    hi = pltpu.bitcast(hi.astype(BF16).astype(F32), U32)
    return (lo >> 16) | (hi & jnp.uint32(0xFFFF0000))


def _unpack_pairs(w):
    lo = pltpu.bitcast(w << 16, F32).astype(BF16)
    hi = pltpu.bitcast(w & jnp.uint32(0xFFFF0000), F32).astype(BF16)
    return lo, hi


def _merge_body(x_ref, sh1_ref, sc1_ref, ga1_ref, sh2_ref, sc2_ref, om_ref, os_ref,
                gn1_ref, wga_ref, wgb_ref, wom_ref, wos_ref, wout_ref, gn2_ref,
                wrh_ref, wrl_ref, br_ref, tri_ref, su_ref,
                x1_ref, h2_ref, lpos_ref, w_ref, tab_ref, tot_ref, run_ref):
    @pl.when(pl.program_id(0) == 0)
    def _():
        run_ref[...] = jnp.zeros_like(run_ref)

    groups = x_ref.shape[0] // SUB_ROWS
    logits = []
    for r in range(groups):
        rows = slice(r * SUB_ROWS, (r + 1) * SUB_ROWS)
        x = x_ref[rows, :]
        h = _modulated_norm(x, gn1_ref[...], sh1_ref[0], sc1_ref[0]).astype(BF16)
        ga = jax.nn.sigmoid(jnp.dot(h, wga_ref[...], preferred_element_type=F32))
        gb = jax.nn.sigmoid(jnp.dot(h, wgb_ref[...], preferred_element_type=F32))
        merged = (ga * jnp.dot(om_ref[rows, :], wom_ref[...], preferred_element_type=F32)
                  + gb * jnp.dot(os_ref[rows, :], wos_ref[...], preferred_element_type=F32))
        y = jnp.dot(merged.astype(BF16), wout_ref[...], preferred_element_type=F32)
        x1 = x + ga1_ref[0] * y
        x1_ref[rows, :] = x1
        h2 = _modulated_norm(x1, gn2_ref[...], sh2_ref[0], sc2_ref[0])
        h2_ref[rows, :] = _pack_pairs(h2)
        h2_hi = h2.astype(BF16)
        h2_lo = (h2 - h2_hi.astype(F32)).astype(BF16)
        logits.append(jnp.dot(h2_hi, wrh_ref[...], preferred_element_type=F32)
                      + jnp.dot(h2_hi, wrl_ref[...], preferred_element_type=F32)
                      + jnp.dot(h2_lo, wrh_ref[...], preferred_element_type=F32)
                      + br_ref[...])
    lpos, wts, tabs, run = _route_rows(jnp.concatenate(logits, axis=0), tri_ref[...], su_ref[...],
                                       run_ref[...])
    lpos_ref[...] = lpos
    w_ref[...] = wts
    for r in range(groups):
        tab_ref[r] = tabs[r]
    run_ref[...] = run
    tot_ref[...] = run


def _merge_call(x2, mods, o_mla, o_sb, p, B, S):
    assert SUB_ROWS == ROW_BLOCK
    T = B * S
    tm = TOK_BLOCK
    nb = S // tm
    per_step = tm // ROW_BLOCK
    row = lambda i: (i, 0)
    per_b = lambda i: (i // nb, 0, 0)
    full = lambda i: (0, 0)
    tri = jnp.asarray(np.arange(ROW_BLOCK)[:, None] > np.arange(ROW_BLOCK)[None, :], dtype=BF16)
    su = jnp.asarray(np.arange(LANES)[:, None] < np.arange(LANES)[None, :], dtype=BF16)
    weights = [p["g_norm1"], p["w_ga"], p["w_gb"], p["w_o_mla"], p["w_o_sb"], p["w_out"],
               p["g_norm2"], p["w_r_hi"], p["w_r_lo"], p["b_r"], tri, su]
    return pl.pallas_call(
        _merge_body,
        grid=(T // tm,),
        in_specs=[pl.BlockSpec((tm, D_MODEL), row)]
        + [pl.BlockSpec((1, 1, D_MODEL), per_b)] * 5
        + [pl.BlockSpec((tm, MLA_HEADS * V_DIM), row), pl.BlockSpec((tm, SB_HEADS * SB_DIM), row)]
        + [pl.BlockSpec(a.shape, full) for a in weights],
        out_specs=[pl.BlockSpec((tm, D_MODEL), row), pl.BlockSpec((tm, D_MODEL // 2), row),
                   pl.BlockSpec((tm, TOP_K), row), pl.BlockSpec((tm, TOP_K), row),
                   pl.BlockSpec((per_step, 8, LANES), lambda i: (i, 0, 0)),
                   pl.BlockSpec((1, LANES), full)],
        out_shape=[jax.ShapeDtypeStruct((T, D_MODEL), F32),
                   jax.ShapeDtypeStruct((T, D_MODEL // 2), U32),
                   jax.ShapeDtypeStruct((T, TOP_K), I32),
                   jax.ShapeDtypeStruct((T, TOP_K), F32),
                   jax.ShapeDtypeStruct((T // ROW_BLOCK, 8, LANES), I32),
                   jax.ShapeDtypeStruct((1, LANES), F32)],
        scratch_shapes=[pltpu.VMEM((1, LANES), F32)],
        compiler_params=_cparams(("arbitrary",)),
        name="merge_route",
    )(x2, *mods, o_mla, o_sb, *weights)


def _route_rows(v, tri, su, run):
    tm = v.shape[0]
    lane_i = lax.broadcasted_iota(I32, (tm, LANES), 1)
    lane = lane_i.astype(F32)
    vals, idxs = [], []
    for _ in range(TOP_K):
        m = jnp.max(v, axis=-1, keepdims=True)
        idx = jnp.min(jnp.where(v == m, lane, float(LANES)), axis=-1, keepdims=True)
        vals.append(m)
        idxs.append(idx)
        v = jnp.where(lane == idx, NEG_BIG, v)
    ex = [jnp.exp(t - vals[0]) for t in vals]
    denom = ex[0] + ex[1] + ex[2] + ex[3]
    onehots = [lane == idx for idx in idxs]
    oh = jnp.zeros((tm, LANES), F32)
    for o in onehots:
        oh = oh + jnp.where(o, 1.0, 0.0)
    sub = lax.broadcasted_iota(I32, (8, LANES), 0)
    pos, tabs = [], []
    for b in range(tm // ROW_BLOCK):
        oh_b = oh[b * ROW_BLOCK:(b + 1) * ROW_BLOCK]
        units = jnp.floor((jnp.sum(oh_b, axis=0, keepdims=True) + (RUN_ALIGN - 1.0)) * (1.0 / RUN_ALIGN))
        lstart = RUN_ALIGN * jnp.dot(jnp.broadcast_to(units, (8, LANES)).astype(BF16), su,
                                     preferred_element_type=F32)[0:1]
        pos.append(jnp.dot(tri, oh_b.astype(BF16), preferred_element_type=F32) + lstart)
        tab = jnp.where(sub == 0, lstart, jnp.where(sub == 1, units, jnp.where(sub == 2, run, 0.0)))
        tabs.append(tab.astype(I32))
        run = run + RUN_ALIGN * units
    pos = jnp.concatenate(pos, axis=0)
    lp_out = jnp.zeros((tm, LANES), I32)
    w_out = jnp.zeros((tm, LANES), F32)
    for kk in range(TOP_K):
        lp = jnp.sum(jnp.where(onehots[kk], pos, 0.0), axis=-1, keepdims=True)
        lp_out = jnp.where(lane_i == kk, lp.astype(I32), lp_out)
        w_out = jnp.where(lane_i == kk, ex[kk] / denom, w_out)
    return lp_out[:, :TOP_K], w_out[:, :TOP_K], tabs, run


def _run_copies(fn, tabs, base, make_copy, bits=RUN_BITS):
    ls_ref, ds_ref, nu_ref = tabs

    def pieces(levels, units, loc, dst):
        for level, bit in levels:
            rows = bit * RUN_ALIGN
            take = (units & bit) != 0

            @pl.when(take)
            def _():
                src = 0 if ls_ref is None else pl.multiple_of(loc, RUN_ALIGN)
                fn(make_copy(src, pl.multiple_of(dst, RUN_ALIGN), rows, level))

            step = jnp.where(take, rows, 0)
            if ls_ref is not None:
                loc = loc + step
            dst = dst + step
        return loc, dst

    levels = list(enumerate(bits))
    small = [lb for lb in levels if lb[1] < SHORT_RUN]
    large = [lb for lb in levels if lb[1] >= SHORT_RUN]

    def per_expert(e, c):
        loc = 0 if ls_ref is None else ls_ref[base + e]
        dst = ds_ref[base + e]
        units = nu_ref[base + e]
        loc, dst = pieces(small, units, loc, dst)

        @pl.when(units >= SHORT_RUN)
        def _():
            pieces(large, units, loc, dst)

        return c

    lax.fori_loop(0, N_EXPERTS, per_expert, 0)


_TN = (((0,), (0,)), ((), ()))


def _slot_matrix(lpos, values):
    tm = lpos.shape[0]
    col = lax.broadcasted_iota(I32, (tm, LOCAL_ROWS), 1)
    out = jnp.zeros((tm, LOCAL_ROWS), F32)
    for kk in range(TOP_K):
        out = jnp.where(col == lpos[:, kk:kk + 1], values[kk], out)
    return out


def _start(cp):
    cp.start()


def _wait(cp):
    cp.wait()


def _table_base(blk):
    return (blk % TABLE_BLOCKS) * N_EXPERTS


def _fill_copies(units, make_copy):
    off = 0
    for level, bit in enumerate(RUN_BITS):
        rows = bit * RUN_ALIGN
        take = (units & bit) != 0

        @pl.when(take)
        def _():
            make_copy(off, rows, level).start()

        off = off + jnp.where(take, rows, 0)


def _dispatch_body(ls_ref, ds_ref, nu_ref, fu_ref, tds_ref, tnu_ref,
                   h_ref, lpos_ref, xs_hbm, loc_ref, zero_ref, sem, tail_sems):
    blk = pl.program_id(0)
    last = pl.num_programs(0) - 1
    slot = blk % 2
    dump = xs_hbm.shape[0] - DUMP_ROWS
    lo, hi = _unpack_pairs(h_ref[...])
    sel = _slot_matrix(lpos_ref[...], [1.0] * TOP_K).astype(BF16)
    x_lo = lax.dot_general(sel, lo, _TN, preferred_element_type=F32)
    x_hi = lax.dot_general(sel, hi, _TN, preferred_element_type=F32)
    loc_ref[slot] = _pack_halves(x_lo, x_hi)

    def run_copy(loc, dst, rows, level):
        return pltpu.make_async_copy(loc_ref.at[slot, pl.ds(loc, rows)], xs_hbm.at[pl.ds(dst, rows)], sem)

    def fill_copy(off, rows, level):
        return pltpu.make_async_copy(loc_ref.at[slot, pl.ds(0, rows)],
                                     xs_hbm.at[pl.ds(dump + DUMP_OFFSETS[level], rows)], sem)

    def block_wait(s):
        pltpu.make_async_copy(loc_ref.at[s], xs_hbm.at[pl.ds(0, LOCAL_ROWS)], sem).wait()

    def tail_copy(loc, dst, rows, level):
        return pltpu.make_async_copy(zero_ref.at[pl.ds(loc, rows)], xs_hbm.at[pl.ds(dst, rows)],
                                     tail_sems.at[level])

    @pl.when(blk > 0)
    def _():
        block_wait(1 - slot)

    _run_copies(_start, (ls_ref, ds_ref, nu_ref), _table_base(blk), run_copy)
    _fill_copies(fu_ref[blk % FILL_CHUNK], fill_copy)

    @pl.when(blk == last)
    def _():
        block_wait(slot)
        zero_ref[...] = jnp.zeros_like(zero_ref)
        tails = (None, tds_ref, tnu_ref)
        _run_copies(_start, tails, 0, tail_copy, TAIL_BITS)
        _run_copies(_wait, tails, 0, tail_copy, TAIL_BITS)


def _table_specs(shift, nblk):
    def block(i):
        return jnp.clip(i + shift, 0, nblk - 1)

    run = pl.BlockSpec((TABLE_BLOCKS * N_EXPERTS,), lambda i: (block(i) // TABLE_BLOCKS,),
                       memory_space=pltpu.SMEM)
    fill = pl.BlockSpec((FILL_CHUNK,), lambda i: (block(i) // FILL_CHUNK,), memory_space=pltpu.SMEM)
    return [run, run, run, fill]


def _dispatch_call(tabs, tail_tabs, h2p, lpos, n_rows):
    T = h2p.shape[0]
    whole = pl.BlockSpec((TABLE_BLOCKS * N_EXPERTS,), lambda i: (0,), memory_space=pltpu.SMEM)
    return pl.pallas_call(
        _dispatch_body,
        grid=(T // ROW_BLOCK,),
        in_specs=_table_specs(0, T // ROW_BLOCK) + [whole, whole] + [
            pl.BlockSpec((ROW_BLOCK, D_MODEL // 2), lambda i: (i, 0)),
            pl.BlockSpec((ROW_BLOCK, TOP_K), lambda i: (i, 0))],
        out_specs=pl.BlockSpec(memory_space=pl.ANY),
        scratch_shapes=[pltpu.VMEM((2, LOCAL_ROWS, D_MODEL // 2), U32),
                        pltpu.VMEM((TAIL_BITS[0] * RUN_ALIGN, D_MODEL // 2), U32),
                        pltpu.SemaphoreType.DMA(()),
                        pltpu.SemaphoreType.DMA((len(TAIL_BITS),))],
        out_shape=jax.ShapeDtypeStruct((n_rows + DUMP_ROWS, D_MODEL // 2), U32),
        compiler_params=_cparams(("arbitrary",)),
        name="moe_dispatch",
    )(*tabs, *tail_tabs, h2p, lpos)


def _expert_body(te_ref, tv_ref, xs_ref, wgu_ref, bgu_ref, wd_ref, bd_ref, y_ref, wgu_bf, wd_bf):
    i = pl.program_id(0)

    @pl.when((i == 0) | (te_ref[i] != te_ref[jnp.maximum(i - 1, 0)]))
    def _():
        quarter = D_MODEL // 4
        for c in range(4):
            rows = slice(c * quarter, (c + 1) * quarter)
            wgu_bf[rows, :] = wgu_ref[rows, :].astype(BF16)
            wd_bf[rows, :] = wd_ref[rows, :].astype(BF16)

    @pl.when(tv_ref[i] == 1)
    def _():
        x = jnp.concatenate(_unpack_pairs(xs_ref[...]), axis=1)
        gu = jnp.dot(x, wgu_bf[...], preferred_element_type=F32) + bgu_ref[...]
        gate = jnp.minimum(gu[:, :D_EXPERT], SWIGLU_LIMIT)
        up = jnp.clip(gu[:, D_EXPERT:], -SWIGLU_LIMIT, SWIGLU_LIMIT)
        hid = (up + 1.0) * (gate * jax.nn.sigmoid(SWIGLU_ALPHA * gate))
        y = jnp.dot(hid.astype(BF16), wd_bf[...], preferred_element_type=F32) + bd_ref[...]
        y_ref[...] = _pack_pairs(y)

    @pl.when(tv_ref[i] == 0)
    def _():
        y_ref[...] = jnp.zeros_like(y_ref)


def _expert_call(tile_e, tile_v, xs, p):
    tmx = EXPERT_TILE
    n_rows = tile_e.shape[0] * tmx
    by_e = lambda i, te, tv: (te[i], 0, 0)
    return pl.pallas_call(
        _expert_body,
        grid_spec=pltpu.PrefetchScalarGridSpec(
            num_scalar_prefetch=2,
            grid=(n_rows // tmx,),
            in_specs=[pl.BlockSpec((tmx, D_MODEL // 2), lambda i, te, tv: (i * tv[i], 0)),
                      pl.BlockSpec((None, D_MODEL, 2 * D_EXPERT), by_e),
                      pl.BlockSpec((None, 1, 2 * D_EXPERT), by_e),
                      pl.BlockSpec((None, D_EXPERT, D_MODEL), by_e),
                      pl.BlockSpec((None, 1, D_MODEL), by_e)],
            out_specs=pl.BlockSpec((tmx, D_MODEL // 2), lambda i, te, tv: (i, 0)),
            scratch_shapes=[pltpu.VMEM((D_MODEL, 2 * D_EXPERT), BF16),
                            pltpu.VMEM((D_EXPERT, D_MODEL), BF16)]),
        out_shape=jax.ShapeDtypeStruct((n_rows, D_MODEL // 2), U32),
        compiler_params=_cparams(("arbitrary",)),
        name="moe_experts",
    )(tile_e, tile_v, xs, p["w_gate_up"], p["b_gate_up"], p["w_down"], p["b_down"])


def _combine_body(ls_ref, ds_ref, nu_ref, fu_ref, nls_ref, nds_ref, nnu_ref, nfu_ref,
                  x1_ref, lpos_ref, w_ref, ga2_ref, ys_hbm, o_ref, loc_ref, sems):
    blk = pl.program_id(0)
    last = pl.num_programs(0) - 1
    slot = blk % 2

    def fetch(tabs, fill_ref, b, s):
        def run_copy(loc, src, rows, level):
            return pltpu.make_async_copy(ys_hbm.at[pl.ds(src, rows)], loc_ref.at[s, pl.ds(loc, rows)],
                                         sems.at[s])

        units = fill_ref[b % FILL_CHUNK]
        used = LOCAL_ROWS - units * RUN_ALIGN

        def fill_copy(off, rows, level):
            return pltpu.make_async_copy(ys_hbm.at[pl.ds(0, rows)],
                                         loc_ref.at[s, pl.ds(pl.multiple_of(used + off, RUN_ALIGN), rows)],
                                         sems.at[s])

        _run_copies(_start, tabs, _table_base(b), run_copy)
        _fill_copies(units, fill_copy)

    @pl.when(blk == 0)
    def _():
        fetch((ls_ref, ds_ref, nu_ref), fu_ref, blk, slot)

    @pl.when(blk < last)
    def _():
        fetch((nls_ref, nds_ref, nnu_ref), nfu_ref, blk + 1, 1 - slot)

    pltpu.make_async_copy(ys_hbm.at[pl.ds(0, LOCAL_ROWS)], loc_ref.at[slot], sems.at[slot]).wait()

    y_lo, y_hi = _unpack_pairs(loc_ref[slot])
    w = w_ref[...]
    mix = _slot_matrix(lpos_ref[...], [w[:, kk:kk + 1] for kk in range(TOP_K)]).astype(BF16)
    halves = [jnp.dot(mix, y, preferred_element_type=F32) for y in (y_lo, y_hi)]
    o_ref[...] = x1_ref[...] + ga2_ref[0] * jnp.concatenate(halves, axis=1)


def _combine_call(tabs, x1, lpos, wts, ga2, ys, B, S):
    T = B * S
    nb = S // ROW_BLOCK
    row = lambda i: (i, 0)
    return pl.pallas_call(
        _combine_body,
        grid=(T // ROW_BLOCK,),
        in_specs=_table_specs(0, T // ROW_BLOCK) + _table_specs(1, T // ROW_BLOCK) + [
            pl.BlockSpec((ROW_BLOCK, D_MODEL), row),
            pl.BlockSpec((ROW_BLOCK, TOP_K), row),
            pl.BlockSpec((ROW_BLOCK, TOP_K), row),
            pl.BlockSpec((1, 1, D_MODEL), lambda i: (i // nb, 0, 0)),
            pl.BlockSpec(memory_space=pl.ANY)],
        out_specs=pl.BlockSpec((ROW_BLOCK, D_MODEL), row),
        scratch_shapes=[pltpu.VMEM((2, LOCAL_ROWS, D_MODEL // 2), U32),
                        pltpu.SemaphoreType.DMA((2,))],
        out_shape=jax.ShapeDtypeStruct((T, D_MODEL), F32),
        compiler_params=_cparams(("arbitrary",)),
        name="moe_combine",
    )(*tabs, *tabs, x1, lpos, wts, ga2, ys)


def _pad_heads(w, per_head, n_heads):
    k = w.shape[0]
    w = w.reshape(k, n_heads, per_head)
    w = jnp.pad(w, ((0, 0), (0, 0), (0, HEAD_PAD - per_head)))
    return w.reshape(k, n_heads * HEAD_PAD)


def _prepare(w_in, g_norm1, g_q_lat, w_uq, g_kv_lat, w_ukv, g_qk_q, g_qk_k, w_o_mla, w_o_sb,
             w_out, g_norm2, w_router, b_router, w_gate_up, b_gate_up, w_down, b_down):
    c0 = Q_LORA
    c1 = c0 + KV_LORA
    c2 = c1 + QK_ROPE
    sbw = SB_HEADS * SB_DIM
    c3 = c2 + 3 * sbw
    c4 = c3 + D_MODEL
    p = {}
    p["g_norm1"] = g_norm1.reshape(1, D_MODEL)
    p["g_norm2"] = g_norm2.reshape(1, D_MODEL)
    p["w_ql"] = w_in[:, :c0].astype(BF16)
    p["w_kvl"] = w_in[:, c0:c1].astype(BF16)
    p["w_kpe"] = jnp.pad(w_in[:, c1:c2], ((0, 0), (QK_NOPE, HEAD_PAD - QK_DIM))).astype(BF16)
    p["w_sq"] = w_in[:, c2:c2 + sbw].astype(BF16)
    p["w_sk"] = w_in[:, c2 + sbw:c2 + 2 * sbw].astype(BF16)
    p["w_sv"] = w_in[:, c2 + 2 * sbw:c3].astype(BF16)
    p["w_ga"] = w_in[:, c3:c4].astype(BF16)
    p["w_gb"] = w_in[:, c4:].astype(BF16)
    p["g_q_lat"] = g_q_lat.reshape(1, Q_LORA)
    p["g_kv_lat"] = g_kv_lat.reshape(1, KV_LORA)
    p["w_uq"] = _pad_heads(w_uq, QK_DIM, MLA_HEADS).astype(BF16)
    kv = w_ukv.reshape(KV_LORA, MLA_HEADS, QK_NOPE + V_DIM)
    p["w_uk"] = _pad_heads(kv[:, :, :QK_NOPE].reshape(KV_LORA, MLA_HEADS * QK_NOPE),
                           QK_NOPE, MLA_HEADS).astype(BF16)
    p["w_v"] = kv[:, :, QK_NOPE:].reshape(KV_LORA, MLA_HEADS * V_DIM).astype(BF16)
    p["g_qk_q"] = jnp.pad(g_qk_q, (0, HEAD_PAD - QK_DIM)).reshape(1, HEAD_PAD)
    p["g_qk_k"] = jnp.pad(g_qk_k, (0, HEAD_PAD - QK_DIM)).reshape(1, HEAD_PAD)
    p["w_o_mla"] = w_o_mla.astype(BF16)
    p["w_o_sb"] = w_o_sb.astype(BF16)
    p["w_out"] = w_out.astype(BF16)
    wr = jnp.pad(w_router, ((0, 0), (0, LANES - N_EXPERTS)))
    p["w_r_hi"] = wr.astype(BF16)
    p["w_r_lo"] = (wr - p["w_r_hi"].astype(F32)).astype(BF16)
    p["b_r"] = jnp.concatenate([b_router, jnp.full((LANES - N_EXPERTS,), NEG_BIG, F32)]).reshape(1, LANES)
    p["w_gate_up"] = w_gate_up
    p["b_gate_up"] = b_gate_up.reshape(N_EXPERTS, 1, 2 * D_EXPERT)
    p["w_down"] = w_down
    p["b_down"] = b_down.reshape(N_EXPERTS, 1, D_MODEL)
    return p


def _layer(x2, c, rope, B, S, w_ada, b_ada, *layer_weights):
    T = B * S
    p = _prepare(*layer_weights)
    mod = _ada_call(c, w_ada, b_ada)
    sh1, sc1, ga1, sh2, sc2, ga2 = [m.reshape(B, 1, D_MODEL) for m in jnp.split(mod, 6, axis=-1)]

    q, k, v, sq, sk, sv = _proj_call(x2, sh1, sc1, rope, p, B, S)
    o_mla, o_sb = _attn_call(q, k, v, sq, sk, sv, B, S)
    x1, h2p, lpos, wts, tab, totals = _merge_call(x2, (sh1, sc1, ga1, sh2, sc2), o_mla, o_sb, p, B, S)

    nblk = T // ROW_BLOCK
    rows_e = totals[0, :N_EXPERTS].astype(I32)
    tiles_e = (rows_e + EXPERT_TILE - 1) // EXPERT_TILE
    tile_end = jnp.cumsum(tiles_e)
    region = ((tile_end - tiles_e) * EXPERT_TILE).astype(I32)
    n_tiles = (T * TOP_K + nblk * N_EXPERTS * (RUN_ALIGN - 1)) // EXPERT_TILE + N_EXPERTS
    tile_ids = jnp.arange(n_tiles, dtype=I32)
    tile_e = jnp.minimum(jnp.sum((tile_ids[:, None] >= tile_end[None, :]).astype(I32), axis=1),
                         N_EXPERTS - 1).astype(I32)
    tile_v = (tile_ids < tile_end[-1]).astype(I32)
    pad = (-nblk) % TABLE_BLOCKS

    def flat(t):
        return jnp.pad(t, ((0, pad), (0, 0))).reshape(-1)

    units = tab[:, 1, :N_EXPERTS]
    fill_units = LOCAL_ROWS // RUN_ALIGN - jnp.sum(units, axis=1)
    tabs = (flat(tab[:, 0, :N_EXPERTS]), flat(tab[:, 2, :N_EXPERTS] + region[None, :]), flat(units),
            jnp.pad(fill_units, (0, (-nblk) % FILL_CHUNK)))
    table_len = TABLE_BLOCKS * N_EXPERTS
    tail_tabs = tuple(jnp.pad(t, (0, table_len - N_EXPERTS))
                      for t in (region + rows_e, (tiles_e * EXPERT_TILE - rows_e) // RUN_ALIGN))

    xs = _dispatch_call(tabs, tail_tabs, h2p, lpos, n_tiles * EXPERT_TILE)
    ys = _expert_call(tile_e, tile_v, xs, p)
    return _combine_call(tabs, x1, lpos, wts, ga2, ys, B, S)


def kernel(x, c, positions, w_ada, b_ada, g_norm1, w_in, g_q_lat, w_uq, g_kv_lat, w_ukv, g_qk_q,
           g_qk_k, w_o_mla, w_o_sb, w_out, g_norm2, w_router, b_router, w_gate_up, b_gate_up,
           w_down, b_down):
    B, S, D = x.shape
    x2 = x.reshape(B * S, D)
    rope = _rope_table(positions)
    for l in range(w_ada.shape[0]):
        x2 = _layer(x2, c, rope, B, S, w_ada[l], b_ada[l], w_in[l], g_norm1[l], g_q_lat[l],
                    w_uq[l], g_kv_lat[l], w_ukv[l], g_qk_q[l], g_qk_k[l], w_o_mla[l], w_o_sb[l],
                    w_out[l], g_norm2[l], w_router[l], b_router[l], w_gate_up[l], b_gate_up[l],
                    w_down[l], b_down[l])
    return x2.reshape(B, S, D)
```

```python
import jax
import jax.numpy as jnp
import numpy as np
from jax import lax
from jax.experimental import pallas as pl
from jax.experimental.pallas import tpu as pltpu

F32 = jnp.float32
BF16 = jnp.bfloat16
I32 = jnp.int32
U32 = jnp.uint32

D_MODEL = 1024
EPS = 1e-6
CHUNK = 64
MLA_HEADS = 8
Q_LORA = 384
KV_LORA = 256
QK_NOPE = 64
QK_ROPE = 32
V_DIM = 64
QK_DIM = QK_NOPE + QK_ROPE
ROPE_THETA = 10000.0
SB_HEADS = 8
SB_DIM = 64
N_EXPERTS = 32
TOP_K = 4
D_EXPERT = D_MODEL
SWIGLU_LIMIT = 7.0
SWIGLU_ALPHA = 1.702

LANES = 128
HEAD_PAD = LANES
TOK_BLOCK = 1024
SUB_ROWS = 256
MLA_BLOCK = 512
SB_BLOCK = 512
SB_KEY_BLOCK = 256
ROW_BLOCK = 256
EXPERT_TILE = 512
RUN_ALIGN = 8
RUN_BITS = tuple(2 ** b for b in reversed(range((ROW_BLOCK // RUN_ALIGN).bit_length())))
LOCAL_ROWS = ROW_BLOCK * TOP_K + N_EXPERTS * RUN_ALIGN
SHORT_RUN = 8
TABLE_BLOCKS = 32
FILL_CHUNK = TABLE_BLOCKS * N_EXPERTS
DUMP_OFFSETS = tuple(RUN_ALIGN * sum(RUN_BITS[:l]) for l in range(len(RUN_BITS)))
DUMP_ROWS = RUN_ALIGN * 2 * RUN_BITS[0]
TAIL_BITS = tuple(2 ** b for b in reversed(range((EXPERT_TILE // RUN_ALIGN - 1).bit_length())))
NEG_BIG = -1e30
LOG2E = 1.4426950408889634
MLA_LOGIT_SCALE = QK_DIM ** -0.5 * LOG2E
SB_LOGIT_SCALE = SB_DIM ** -0.5 * LOG2E
VMEM_LIMIT = 48 * 1024 * 1024


def _cparams(sem):
    return pltpu.CompilerParams(dimension_semantics=sem, vmem_limit_bytes=VMEM_LIMIT)


def _ada_body(c_ref, w_ref, b_ref, o_ref):
    c = c_ref[...]
    s = c * jax.nn.sigmoid(c)
    o_ref[...] = jnp.dot(s, w_ref[...], preferred_element_type=F32,
                         precision=lax.Precision.HIGHEST) + b_ref[...]


def _ada_call(c, w_ada, b_ada):
    B = c.shape[0]
    n = w_ada.shape[1]
    bn = 512
    return pl.pallas_call(
        _ada_body,
        grid=(n // bn,),
        in_specs=[pl.BlockSpec((B, D_MODEL), lambda j: (0, 0)),
                  pl.BlockSpec((D_MODEL, bn), lambda j: (0, j)),
                  pl.BlockSpec((1, bn), lambda j: (0, j))],
        out_specs=pl.BlockSpec((B, bn), lambda j: (0, j)),
        out_shape=jax.ShapeDtypeStruct((B, n), F32),
        compiler_params=_cparams(("arbitrary",)),
        name="ada_mod",
    )(c, w_ada, b_ada.reshape(1, n))


def _rms(v, width):
    return lax.rsqrt(jnp.sum(v * v, axis=-1, keepdims=True) * (1.0 / width) + EPS)


def _modulated_norm(x, g, sh, sc):
    h = x * _rms(x, D_MODEL) * g
    return h * (1.0 + sc) + sh


def _rope_body(invf_ref, pos_ref, tab_ref):
    pos = pos_ref[...]
    half = QK_ROPE // 2
    tab_ref[...] = jnp.zeros_like(tab_ref)
    for f in range(half):
        ang = pos * invf_ref[f]
        cosf = jnp.cos(ang)
        sinf = jnp.sin(ang)
        tab_ref[f] = sinf
        tab_ref[half + f] = sinf
        tab_ref[QK_NOPE + f] = cosf
        tab_ref[QK_NOPE + half + f] = cosf


def _rope_table(positions):
    T = positions.size
    assert T % (8 * LANES) == 0
    inv_freq = 1.0 / (ROPE_THETA ** (jnp.arange(0, QK_ROPE, 2, dtype=F32) / QK_ROPE))
    pos = positions.astype(F32).reshape(T // LANES, LANES)
    rb = max(r for r in (8, 16, 32, 64) if (T // LANES) % r == 0)
    table = pl.pallas_call(
        _rope_body,
        grid_spec=pltpu.PrefetchScalarGridSpec(
            num_scalar_prefetch=1,
            grid=(T // LANES // rb,),
            in_specs=[pl.BlockSpec((rb, LANES), lambda i, f: (i, 0))],
            out_specs=pl.BlockSpec((HEAD_PAD, rb, LANES), lambda i, f: (0, i, 0))),
        out_shape=jax.ShapeDtypeStruct((HEAD_PAD, T // LANES, LANES), F32),
        compiler_params=_cparams(("arbitrary",)),
        name="rope_table",
    )(inv_freq, pos)
    return table.reshape(HEAD_PAD, T).T


def _proj_body(x_ref, sh_ref, sc_ref, rope_ref, gn_ref, wql_ref, wkvl_ref, wkpe_ref,
               wsq_ref, wsk_ref, wsv_ref, gql_ref, wuq_ref, gkvl_ref, wuk_ref, wv_ref,
               gq_ref, gk_ref,
               q_ref, k_ref, v_ref, sq_ref, sk_ref, sv_ref):
    for r in range(x_ref.shape[0] // SUB_ROWS):
        rows = slice(r * SUB_ROWS, (r + 1) * SUB_ROWS)
        x = x_ref[rows, :]
        h = _modulated_norm(x, gn_ref[...], sh_ref[0], sc_ref[0]).astype(BF16)

        sq_ref[rows, :] = (jnp.dot(h, wsq_ref[...], preferred_element_type=F32) * SB_LOGIT_SCALE).astype(BF16)
        sk_ref[rows, :] = jnp.dot(h, wsk_ref[...], preferred_element_type=F32).astype(BF16)
        sv_ref[rows, :] = jnp.dot(h, wsv_ref[...], preferred_element_type=F32).astype(BF16)

        q_lat = jnp.dot(h, wql_ref[...], preferred_element_type=F32)
        kv_lat = jnp.dot(h, wkvl_ref[...], preferred_element_type=F32)
        kpe = jnp.dot(h, wkpe_ref[...], preferred_element_type=F32)

        qn = (q_lat * _rms(q_lat, Q_LORA) * gql_ref[...]).astype(BF16)
        kvn = (kv_lat * _rms(kv_lat, KV_LORA) * gkvl_ref[...]).astype(BF16)
        q = jnp.dot(qn, wuq_ref[...], preferred_element_type=F32)
        kn = jnp.dot(kvn, wuk_ref[...], preferred_element_type=F32)
        v_ref[rows, :] = jnp.dot(kvn, wv_ref[...], preferred_element_type=F32).astype(BF16)

        lane = lax.broadcasted_iota(I32, (SUB_ROWS, LANES), 1)
        tbl = rope_ref[rows, :]
        sinf = pltpu.roll(tbl, QK_NOPE, 1)
        half = QK_ROPE // 2
        cosf = jnp.where((lane >= QK_NOPE) & (lane < QK_DIM), tbl, 1.0)
        s_lo = jnp.where((lane >= QK_NOPE) & (lane < QK_NOPE + half), -sinf, 0.0)
        s_hi = jnp.where((lane >= QK_NOPE + half) & (lane < QK_DIM), sinf, 0.0)

        def rope(t):
            return (t * cosf + pltpu.roll(t, LANES - half, 1) * s_lo
                    + pltpu.roll(t, half, 1) * s_hi)

        gq = gq_ref[...] * MLA_LOGIT_SCALE
        gk = gk_ref[...]
        kpe_sq = jnp.sum(kpe * kpe, axis=-1, keepdims=True)
        kr = rope(kpe * gk)
        for hh in range(MLA_HEADS):
            sl = slice(hh * HEAD_PAD, (hh + 1) * HEAD_PAD)
            qh = q[:, sl]
            q_ref[rows, sl] = rope(qh * _rms(qh, QK_DIM) * gq).astype(BF16)
            kh = kn[:, sl]
            rk = lax.rsqrt((jnp.sum(kh * kh, axis=-1, keepdims=True) + kpe_sq) * (1.0 / QK_DIM) + EPS)
            k_ref[rows, sl] = ((kh * gk + kr) * rk).astype(BF16)


def _proj_call(x2, sh1, sc1, rope, p, B, S):
    T = B * S
    tm = TOK_BLOCK
    nb = S // tm
    row = lambda i: (i, 0)
    per_b = lambda i: (i // nb, 0, 0)
    full = lambda i: (0, 0)

    def wspec(a):
        return pl.BlockSpec(a.shape, full)

    weights = [p["g_norm1"], p["w_ql"], p["w_kvl"], p["w_kpe"], p["w_sq"], p["w_sk"], p["w_sv"],
               p["g_q_lat"], p["w_uq"], p["g_kv_lat"], p["w_uk"], p["w_v"], p["g_qk_q"],
               p["g_qk_k"]]
    out_w = [MLA_HEADS * HEAD_PAD, MLA_HEADS * HEAD_PAD, MLA_HEADS * V_DIM,
             SB_HEADS * SB_DIM, SB_HEADS * SB_DIM, SB_HEADS * SB_DIM]
    return pl.pallas_call(
        _proj_body,
        grid=(T // tm,),
        in_specs=[pl.BlockSpec((tm, D_MODEL), row),
                  pl.BlockSpec((1, 1, D_MODEL), per_b),
                  pl.BlockSpec((1, 1, D_MODEL), per_b),
                  pl.BlockSpec((tm, LANES), row)] + [wspec(a) for a in weights],
        out_specs=[pl.BlockSpec((tm, w), row) for w in out_w],
        out_shape=[jax.ShapeDtypeStruct((T, w), BF16) for w in out_w],
        compiler_params=_cparams(("arbitrary",)),
        name="mix_proj",
    )(x2, sh1, sc1, rope, *weights)


_NT = (((1,), (1,)), ((), ()))


def _mla_rows(qh, k_ref, v_ref, sl, hh, i, tq):
    kend = (i + 1) * tq
    s = lax.dot_general(qh, k_ref[0:kend, sl], _NT, preferred_element_type=F32)
    row = lax.broadcasted_iota(I32, (tq, tq), 0)
    col = lax.broadcasted_iota(I32, (tq, tq), 1)
    s_diag = jnp.where((col // CHUNK) <= (row // CHUNK), s[:, kend - tq:], NEG_BIG)
    s = s_diag if i == 0 else jnp.concatenate([s[:, :kend - tq], s_diag], axis=1)
    pr = jnp.exp2(s - jnp.max(s, axis=-1, keepdims=True)).astype(BF16)
    vb = v_ref[0:kend, :].astype(F32)
    lane = lax.broadcasted_iota(I32, vb.shape, 1)
    own = (lane >= hh * V_DIM) & (lane < (hh + 1) * V_DIM)
    acc = jnp.dot(pr, jnp.where(own, vb, 1.0).astype(BF16), preferred_element_type=F32)
    return acc / pltpu.roll(acc, V_DIM, 1)


def _mla_block(q_ref, k_ref, v_ref, o_ref, ii):
    tq = MLA_BLOCK
    lane = lax.broadcasted_iota(I32, (tq, LANES), 1)
    rows = slice(ii * tq, (ii + 1) * tq)
    outs = []
    for hh in range(2):
        sl = slice(hh * HEAD_PAD, (hh + 1) * HEAD_PAD)
        outs.append(_mla_rows(q_ref[rows, sl], k_ref, v_ref, sl, hh, ii, tq))
    o_ref[rows, :] = jnp.where(lane < V_DIM, outs[0], outs[1]).astype(BF16)


def _sb_rows(qh, k_ref, v_ref, i, tq, tk):
    kend = (i + 1) * tq
    nk = kend // tk
    z = lax.dot_general(qh, k_ref[0:kend, :], _NT, preferred_element_type=F32)
    row = lax.broadcasted_iota(I32, (tk, tk), 0)
    col = lax.broadcasted_iota(I32, (tk, tk), 1)
    suffix_ones = jnp.where(row >= col, 1.0, 0.0).astype(BF16)
    later = jnp.zeros((tq, 1), F32)
    a_blocks = [None] * nk
    for j in range(nk - 1, -1, -1):
        first = max(j * tk - i * tq, 0)
        diagonal = j * tk >= i * tq
        zj = z[first:, j * tk:(j + 1) * tk]
        spj = jnp.maximum(zj, 0.0) + jnp.log2(1.0 + jnp.exp2(jnp.minimum(zj, -zj)))
        if diagonal:
            visible = (lax.broadcasted_iota(I32, zj.shape, 1) < lax.broadcasted_iota(I32, zj.shape, 0))
            spj = jnp.where(visible, spj, 0.0)
        suf = jnp.dot(spj.astype(BF16), suffix_ones, preferred_element_type=F32)
        a = jnp.exp2(zj - suf - later[first:])
        if diagonal:
            a = jnp.where(visible, a, 0.0)
        a = a.astype(BF16)
        total = suf[:, 0:1]
        if first > 0:
            a = jnp.concatenate([jnp.zeros((first, tk), BF16), a], axis=0)
            total = jnp.concatenate([jnp.zeros((first, 1), F32), total], axis=0)
        a_blocks[j] = a
        later = later + total
    a = a_blocks[0] if nk == 1 else jnp.concatenate(a_blocks, axis=1)
    return jnp.dot(a, v_ref[0:kend, :], preferred_element_type=F32)


def _sb_block(q_ref, k_ref, v_ref, o_ref, ii):
    tq = SB_BLOCK
    lane = lax.broadcasted_iota(I32, (tq, LANES), 1)
    rows = slice(ii * tq, (ii + 1) * tq)
    q = q_ref[rows, :]
    outs = []
    for hh in range(2):
        in_head = (lane >= hh * SB_DIM) & (lane < (hh + 1) * SB_DIM)
        qh = jnp.where(in_head, q, jnp.zeros_like(q))
        outs.append(_sb_rows(qh, k_ref, v_ref, ii, tq, SB_KEY_BLOCK))
    o_ref[rows, :] = jnp.where(lane < SB_DIM, outs[0], outs[1]).astype(BF16)


def _attn_body(q_ref, k_ref, v_ref, sq_ref, sk_ref, sv_ref, om_ref, os_ref):
    S = q_ref.shape[0]
    per_mla = MLA_BLOCK // SB_BLOCK
    for ii in range(S // MLA_BLOCK):
        _mla_block(q_ref, k_ref, v_ref, om_ref, ii)
        for jj in range(per_mla):
            _sb_block(sq_ref, sk_ref, sv_ref, os_ref, ii * per_mla + jj)


def _attn_call(q, k, v, sq, sk, sv, B, S):
    assert MLA_HEADS == SB_HEADS and V_DIM == SB_DIM and MLA_BLOCK % SB_BLOCK == 0
    wide = MLA_HEADS * HEAD_PAD
    narrow = SB_HEADS * SB_DIM
    args = [a.reshape(B, S, wide) for a in (q, k)] + [a.reshape(B, S, narrow) for a in (v, sq, sk, sv)]
    pair = lambda b, hp: (b, 0, hp)
    wide_spec = pl.BlockSpec((None, S, 2 * HEAD_PAD), pair)
    narrow_spec = pl.BlockSpec((None, S, 2 * SB_DIM), pair)
    o_mla, o_sb = pl.pallas_call(
        _attn_body,
        grid=(B, MLA_HEADS // 2),
        in_specs=[wide_spec, wide_spec] + [narrow_spec] * 4,
        out_specs=[narrow_spec, narrow_spec],
        out_shape=[jax.ShapeDtypeStruct((B, S, narrow), BF16)] * 2,
        compiler_params=_cparams(("arbitrary", "arbitrary")),
        name="attn_pair",
    )(*args)
    return o_mla.reshape(B * S, narrow), o_sb.reshape(B * S, narrow)


def _pack_pairs(h):
    n = h.shape[1] // 2
    return _pack_halves(h[:, :n], h[:, n:])


def _pack_halves(lo, hi):
    lo = pltpu.bitcast(lo.astype(BF16).astype(F32), U32)
    hi = pltpu.bitcast(hi.astype(BF16).astype(F32), U32)
    return (lo >> 16) | (hi & jnp.uint32(0xFFFF0000))


def _unpack_pairs(w):
    lo = pltpu.bitcast(w << 16, F32).astype(BF16)
    hi = pltpu.bitcast(w & jnp.uint32(0xFFFF0000), F32).astype(BF16)
    return lo, hi


def _merge_body(x_ref, sh1_ref, sc1_ref, ga1_ref, sh2_ref, sc2_ref, om_ref, os_ref,
                gn1_ref, wga_ref, wgb_ref, wom_ref, wos_ref, wout_ref, gn2_ref,
                wrh_ref, wrl_ref, br_ref, tri_ref, su_ref,
                x1_ref, h2_ref, lpos_ref, w_ref, tab_ref, tot_ref, run_ref):
    @pl.when(pl.program_id(0) == 0)
    def _():
        run_ref[...] = jnp.zeros_like(run_ref)

    groups = x_ref.shape[0] // SUB_ROWS
    logits = []
    for r in range(groups):
        rows = slice(r * SUB_ROWS, (r + 1) * SUB_ROWS)
        x = x_ref[rows, :]
        h = _modulated_norm(x, gn1_ref[...], sh1_ref[0], sc1_ref[0]).astype(BF16)
        ga = jax.nn.sigmoid(jnp.dot(h, wga_ref[...], preferred_element_type=F32))
        gb = jax.nn.sigmoid(jnp.dot(h, wgb_ref[...], preferred_element_type=F32))
        merged = (ga * jnp.dot(om_ref[rows, :], wom_ref[...], preferred_element_type=F32)
                  + gb * jnp.dot(os_ref[rows, :], wos_ref[...], preferred_element_type=F32))
        y = jnp.dot(merged.astype(BF16), wout_ref[...], preferred_element_type=F32)
        x1 = x + ga1_ref[0] * y
        x1_ref[rows, :] = x1
        h2 = _modulated_norm(x1, gn2_ref[...], sh2_ref[0], sc2_ref[0])
        h2_ref[rows, :] = _pack_pairs(h2)
        h2_hi = h2.astype(BF16)
        h2_lo = (h2 - h2_hi.astype(F32)).astype(BF16)
        logits.append(jnp.dot(h2_hi, wrh_ref[...], preferred_element_type=F32)
                      + jnp.dot(h2_hi, wrl_ref[...], preferred_element_type=F32)
                      + jnp.dot(h2_lo, wrh_ref[...], preferred_element_type=F32)
                      + br_ref[...])
    lpos, wts, tabs, run = _route_rows(jnp.concatenate(logits, axis=0), tri_ref[...], su_ref[...],
                                       run_ref[...])
    lpos_ref[...] = lpos
    w_ref[...] = wts
    for r in range(groups):
        tab_ref[r] = tabs[r]
    run_ref[...] = run
    tot_ref[...] = run


def _merge_call(x2, mods, o_mla, o_sb, p, B, S):
    assert SUB_ROWS == ROW_BLOCK
    T = B * S
    tm = TOK_BLOCK
    nb = S // tm
    per_step = tm // ROW_BLOCK
    row = lambda i: (i, 0)
    per_b = lambda i: (i // nb, 0, 0)
    full = lambda i: (0, 0)
    tri = jnp.asarray(np.arange(ROW_BLOCK)[:, None] > np.arange(ROW_BLOCK)[None, :], dtype=BF16)
    su = jnp.asarray(np.arange(LANES)[:, None] < np.arange(LANES)[None, :], dtype=BF16)
    weights = [p["g_norm1"], p["w_ga"], p["w_gb"], p["w_o_mla"], p["w_o_sb"], p["w_out"],
               p["g_norm2"], p["w_r_hi"], p["w_r_lo"], p["b_r"], tri, su]
    return pl.pallas_call(
        _merge_body,
        grid=(T // tm,),
        in_specs=[pl.BlockSpec((tm, D_MODEL), row)]
        + [pl.BlockSpec((1, 1, D_MODEL), per_b)] * 5
        + [pl.BlockSpec((tm, MLA_HEADS * V_DIM), row), pl.BlockSpec((tm, SB_HEADS * SB_DIM), row)]
        + [pl.BlockSpec(a.shape, full) for a in weights],
        out_specs=[pl.BlockSpec((tm, D_MODEL), row), pl.BlockSpec((tm, D_MODEL // 2), row),
                   pl.BlockSpec((tm, TOP_K), row), pl.BlockSpec((tm, TOP_K), row),
                   pl.BlockSpec((per_step, 8, LANES), lambda i: (i, 0, 0)),
                   pl.BlockSpec((1, LANES), full)],
        out_shape=[jax.ShapeDtypeStruct((T, D_MODEL), F32),
                   jax.ShapeDtypeStruct((T, D_MODEL // 2), U32),
                   jax.ShapeDtypeStruct((T, TOP_K), I32),
                   jax.ShapeDtypeStruct((T, TOP_K), F32),
                   jax.ShapeDtypeStruct((T // ROW_BLOCK, 8, LANES), I32),
                   jax.ShapeDtypeStruct((1, LANES), F32)],
        scratch_shapes=[pltpu.VMEM((1, LANES), F32)],
        compiler_params=_cparams(("arbitrary",)),
        name="merge_route",
    )(x2, *mods, o_mla, o_sb, *weights)


def _route_rows(v, tri, su, run):
    tm = v.shape[0]
    lane_i = lax.broadcasted_iota(I32, (tm, LANES), 1)
    lane = lane_i.astype(F32)
    vals, idxs = [], []
    for _ in range(TOP_K):
        m = jnp.max(v, axis=-1, keepdims=True)
        idx = jnp.min(jnp.where(v == m, lane, float(LANES)), axis=-1, keepdims=True)
        vals.append(m)
        idxs.append(idx)
        v = jnp.where(lane == idx, NEG_BIG, v)
    ex = [jnp.exp(t - vals[0]) for t in vals]
    denom = ex[0] + ex[1] + ex[2] + ex[3]
    onehots = [lane == idx for idx in idxs]
    oh = jnp.zeros((tm, LANES), F32)
    for o in onehots:
        oh = oh + jnp.where(o, 1.0, 0.0)
    sub = lax.broadcasted_iota(I32, (8, LANES), 0)
    pos, tabs = [], []
    for b in range(tm // ROW_BLOCK):
        oh_b = oh[b * ROW_BLOCK:(b + 1) * ROW_BLOCK]
        units = jnp.floor((jnp.sum(oh_b, axis=0, keepdims=True) + (RUN_ALIGN - 1.0)) * (1.0 / RUN_ALIGN))
        lstart = RUN_ALIGN * jnp.dot(jnp.broadcast_to(units, (8, LANES)).astype(BF16), su,
                                     preferred_element_type=F32)[0:1]
        pos.append(jnp.dot(tri, oh_b.astype(BF16), preferred_element_type=F32) + lstart)
        tab = jnp.where(sub == 0, lstart, jnp.where(sub == 1, units, jnp.where(sub == 2, run, 0.0)))
        tabs.append(tab.astype(I32))
        run = run + RUN_ALIGN * units
    pos = jnp.concatenate(pos, axis=0)
    lp_out = jnp.zeros((tm, LANES), I32)
    w_out = jnp.zeros((tm, LANES), F32)
    for kk in range(TOP_K):
        lp = jnp.sum(jnp.where(onehots[kk], pos, 0.0), axis=-1, keepdims=True)
        lp_out = jnp.where(lane_i == kk, lp.astype(I32), lp_out)
        w_out = jnp.where(lane_i == kk, ex[kk] / denom, w_out)
    return lp_out[:, :TOP_K], w_out[:, :TOP_K], tabs, run


def _run_copies(fn, tabs, base, make_copy, bits=RUN_BITS):
    ls_ref, ds_ref, nu_ref = tabs

    def pieces(levels, units, loc, dst):
        for level, bit in levels:
            rows = bit * RUN_ALIGN
            take = (units & bit) != 0

            @pl.when(take)
            def _():
                src = 0 if ls_ref is None else pl.multiple_of(loc, RUN_ALIGN)
                fn(make_copy(src, pl.multiple_of(dst, RUN_ALIGN), rows, level))

            step = jnp.where(take, rows, 0)
            if ls_ref is not None:
                loc = loc + step
            dst = dst + step
        return loc, dst

    levels = list(enumerate(bits))
    small = [lb for lb in levels if lb[1] < SHORT_RUN]
    large = [lb for lb in levels if lb[1] >= SHORT_RUN]

    def per_expert(e, c):
        loc = 0 if ls_ref is None else ls_ref[base + e]
        dst = ds_ref[base + e]
        units = nu_ref[base + e]
        loc, dst = pieces(small, units, loc, dst)

        @pl.when(units >= SHORT_RUN)
        def _():
            pieces(large, units, loc, dst)

        return c

    lax.fori_loop(0, N_EXPERTS, per_expert, 0, unroll=2)


_TN = (((0,), (0,)), ((), ()))


def _slot_matrix(lpos, values):
    tm = lpos.shape[0]
    col = lax.broadcasted_iota(I32, (tm, LOCAL_ROWS), 1)
    out = jnp.zeros((tm, LOCAL_ROWS), F32)
    for kk in range(TOP_K):
        out = jnp.where(col == lpos[:, kk:kk + 1], values[kk], out)
    return out


def _start(cp):
    cp.start()


def _wait(cp):
    cp.wait()


def _table_base(blk):
    return (blk % TABLE_BLOCKS) * N_EXPERTS


def _fill_copies(units, make_copy):
    off = 0
    for level, bit in enumerate(RUN_BITS):
        rows = bit * RUN_ALIGN
        take = (units & bit) != 0

        @pl.when(take)
        def _():
            make_copy(off, rows, level).start()

        off = off + jnp.where(take, rows, 0)


def _dispatch_body(ls_ref, ds_ref, nu_ref, fu_ref, tds_ref, tnu_ref,
                   h_ref, lpos_ref, xs_hbm, loc_ref, zero_ref, sem, tail_sems):
    blk = pl.program_id(0)
    last = pl.num_programs(0) - 1
    slot = blk % 2
    dump = xs_hbm.shape[0] - DUMP_ROWS
    lo, hi = _unpack_pairs(h_ref[...])
    sel = _slot_matrix(lpos_ref[...], [1.0] * TOP_K).astype(BF16)
    x_lo = lax.dot_general(sel, lo, _TN, preferred_element_type=F32)
    x_hi = lax.dot_general(sel, hi, _TN, preferred_element_type=F32)
    loc_ref[slot] = _pack_halves(x_lo, x_hi)

    def run_copy(loc, dst, rows, level):
        return pltpu.make_async_copy(loc_ref.at[slot, pl.ds(loc, rows)], xs_hbm.at[pl.ds(dst, rows)], sem)

    def fill_copy(off, rows, level):
        return pltpu.make_async_copy(loc_ref.at[slot, pl.ds(0, rows)],
                                     xs_hbm.at[pl.ds(dump + DUMP_OFFSETS[level], rows)], sem)

    def block_wait(s):
        pltpu.make_async_copy(loc_ref.at[s], xs_hbm.at[pl.ds(0, LOCAL_ROWS)], sem).wait()

    def tail_copy(loc, dst, rows, level):
        return pltpu.make_async_copy(zero_ref.at[pl.ds(loc, rows)], xs_hbm.at[pl.ds(dst, rows)],
                                     tail_sems.at[level])

    @pl.when(blk > 0)
    def _():
        block_wait(1 - slot)

    _run_copies(_start, (ls_ref, ds_ref, nu_ref), _table_base(blk), run_copy)
    _fill_copies(fu_ref[blk % FILL_CHUNK], fill_copy)

    @pl.when(blk == last)
    def _():
        block_wait(slot)
        zero_ref[...] = jnp.zeros_like(zero_ref)
        tails = (None, tds_ref, tnu_ref)
        _run_copies(_start, tails, 0, tail_copy, TAIL_BITS)
        _run_copies(_wait, tails, 0, tail_copy, TAIL_BITS)


def _table_specs(shift, nblk):
    def block(i):
        return jnp.clip(i + shift, 0, nblk - 1)

    run = pl.BlockSpec((TABLE_BLOCKS * N_EXPERTS,), lambda i: (block(i) // TABLE_BLOCKS,),
                       memory_space=pltpu.SMEM)
    fill = pl.BlockSpec((FILL_CHUNK,), lambda i: (block(i) // FILL_CHUNK,), memory_space=pltpu.SMEM)
    return [run, run, run, fill]


def _dispatch_call(tabs, tail_tabs, h2p, lpos, n_rows):
    T = h2p.shape[0]
    whole = pl.BlockSpec((TABLE_BLOCKS * N_EXPERTS,), lambda i: (0,), memory_space=pltpu.SMEM)
    return pl.pallas_call(
        _dispatch_body,
        grid=(T // ROW_BLOCK,),
        in_specs=_table_specs(0, T // ROW_BLOCK) + [whole, whole] + [
            pl.BlockSpec((ROW_BLOCK, D_MODEL // 2), lambda i: (i, 0)),
            pl.BlockSpec((ROW_BLOCK, TOP_K), lambda i: (i, 0))],
        out_specs=pl.BlockSpec(memory_space=pl.ANY),
        scratch_shapes=[pltpu.VMEM((2, LOCAL_ROWS, D_MODEL // 2), U32),
                        pltpu.VMEM((TAIL_BITS[0] * RUN_ALIGN, D_MODEL // 2), U32),
                        pltpu.SemaphoreType.DMA(()),
                        pltpu.SemaphoreType.DMA((len(TAIL_BITS),))],
        out_shape=jax.ShapeDtypeStruct((n_rows + DUMP_ROWS, D_MODEL // 2), U32),
        compiler_params=_cparams(("arbitrary",)),
        name="moe_dispatch",
    )(*tabs, *tail_tabs, h2p, lpos)


def _expert_body(te_ref, tv_ref, xs_ref, wgu_ref, bgu_ref, wd_ref, bd_ref, y_ref, wgu_bf, wd_bf):
    i = pl.program_id(0)

    @pl.when((i == 0) | (te_ref[i] != te_ref[jnp.maximum(i - 1, 0)]))
    def _():
        quarter = D_MODEL // 4
        for c in range(4):
            rows = slice(c * quarter, (c + 1) * quarter)
            wgu_bf[rows, :] = wgu_ref[rows, :].astype(BF16)
            wd_bf[rows, :] = wd_ref[rows, :].astype(BF16)

    @pl.when(tv_ref[i] == 1)
    def _():
        x = jnp.concatenate(_unpack_pairs(xs_ref[...]), axis=1)
        gu = jnp.dot(x, wgu_bf[...], preferred_element_type=F32) + bgu_ref[...]
        gate = jnp.minimum(gu[:, :D_EXPERT], SWIGLU_LIMIT)
        up = jnp.clip(gu[:, D_EXPERT:], -SWIGLU_LIMIT, SWIGLU_LIMIT)
        hid = (up + 1.0) * (gate * jax.nn.sigmoid(SWIGLU_ALPHA * gate))
        y = jnp.dot(hid.astype(BF16), wd_bf[...], preferred_element_type=F32) + bd_ref[...]
        y_ref[...] = _pack_pairs(y)

    @pl.when(tv_ref[i] == 0)
    def _():
        y_ref[...] = jnp.zeros_like(y_ref)


def _expert_call(tile_e, tile_v, xs, p):
    tmx = EXPERT_TILE
    n_rows = tile_e.shape[0] * tmx
    by_e = lambda i, te, tv: (te[i], 0, 0)
    return pl.pallas_call(
        _expert_body,
        grid_spec=pltpu.PrefetchScalarGridSpec(
            num_scalar_prefetch=2,
            grid=(n_rows // tmx,),
            in_specs=[pl.BlockSpec((tmx, D_MODEL // 2), lambda i, te, tv: (i * tv[i], 0)),
                      pl.BlockSpec((None, D_MODEL, 2 * D_EXPERT), by_e),
                      pl.BlockSpec((None, 1, 2 * D_EXPERT), by_e),
                      pl.BlockSpec((None, D_EXPERT, D_MODEL), by_e),
                      pl.BlockSpec((None, 1, D_MODEL), by_e)],
            out_specs=pl.BlockSpec((tmx, D_MODEL // 2), lambda i, te, tv: (i, 0)),
            scratch_shapes=[pltpu.VMEM((D_MODEL, 2 * D_EXPERT), BF16),
                            pltpu.VMEM((D_EXPERT, D_MODEL), BF16)]),
        out_shape=jax.ShapeDtypeStruct((n_rows, D_MODEL // 2), U32),
        compiler_params=_cparams(("arbitrary",)),
        name="moe_experts",
    )(tile_e, tile_v, xs, p["w_gate_up"], p["b_gate_up"], p["w_down"], p["b_down"])


def _combine_body(ls_ref, ds_ref, nu_ref, fu_ref, nls_ref, nds_ref, nnu_ref, nfu_ref,
                  x1_ref, lpos_ref, w_ref, ga2_ref, ys_hbm, o_ref, loc_ref, sems):
    blk = pl.program_id(0)
    last = pl.num_programs(0) - 1
    slot = blk % 2

    def fetch(tabs, fill_ref, b, s):
        def run_copy(loc, src, rows, level):
            return pltpu.make_async_copy(ys_hbm.at[pl.ds(src, rows)], loc_ref.at[s, pl.ds(loc, rows)],
                                         sems.at[s])

        units = fill_ref[b % FILL_CHUNK]
        used = LOCAL_ROWS - units * RUN_ALIGN

        def fill_copy(off, rows, level):
            return pltpu.make_async_copy(ys_hbm.at[pl.ds(0, rows)],
                                         loc_ref.at[s, pl.ds(pl.multiple_of(used + off, RUN_ALIGN), rows)],
                                         sems.at[s])

        _run_copies(_start, tabs, _table_base(b), run_copy)
        _fill_copies(units, fill_copy)

    @pl.when(blk == 0)
    def _():
        fetch((ls_ref, ds_ref, nu_ref), fu_ref, blk, slot)

    @pl.when(blk < last)
    def _():
        fetch((nls_ref, nds_ref, nnu_ref), nfu_ref, blk + 1, 1 - slot)

    pltpu.make_async_copy(ys_hbm.at[pl.ds(0, LOCAL_ROWS)], loc_ref.at[slot], sems.at[slot]).wait()

    y_lo, y_hi = _unpack_pairs(loc_ref[slot])
    w = w_ref[...]
    mix = _slot_matrix(lpos_ref[...], [w[:, kk:kk + 1] for kk in range(TOP_K)]).astype(BF16)
    halves = [jnp.dot(mix, y, preferred_element_type=F32) for y in (y_lo, y_hi)]
    o_ref[...] = x1_ref[...] + ga2_ref[0] * jnp.concatenate(halves, axis=1)


def _combine_call(tabs, x1, lpos, wts, ga2, ys, B, S):
    T = B * S
    nb = S // ROW_BLOCK
    row = lambda i: (i, 0)
    return pl.pallas_call(
        _combine_body,
        grid=(T // ROW_BLOCK,),
        in_specs=_table_specs(0, T // ROW_BLOCK) + _table_specs(1, T // ROW_BLOCK) + [
            pl.BlockSpec((ROW_BLOCK, D_MODEL), row),
            pl.BlockSpec((ROW_BLOCK, TOP_K), row),
            pl.BlockSpec((ROW_BLOCK, TOP_K), row),
            pl.BlockSpec((1, 1, D_MODEL), lambda i: (i // nb, 0, 0)),
            pl.BlockSpec(memory_space=pl.ANY)],
        out_specs=pl.BlockSpec((ROW_BLOCK, D_MODEL), row),
        scratch_shapes=[pltpu.VMEM((2, LOCAL_ROWS, D_MODEL // 2), U32),
                        pltpu.SemaphoreType.DMA((2,))],
        out_shape=jax.ShapeDtypeStruct((T, D_MODEL), F32),
        compiler_params=_cparams(("arbitrary",)),
        name="moe_combine",
    )(*tabs, *tabs, x1, lpos, wts, ga2, ys)


def _pad_heads(w, per_head, n_heads):
    k = w.shape[0]
    w = w.reshape(k, n_heads, per_head)
    w = jnp.pad(w, ((0, 0), (0, 0), (0, HEAD_PAD - per_head)))
    return w.reshape(k, n_heads * HEAD_PAD)


def _prepare(w_in, g_norm1, g_q_lat, w_uq, g_kv_lat, w_ukv, g_qk_q, g_qk_k, w_o_mla, w_o_sb,
             w_out, g_norm2, w_router, b_router, w_gate_up, b_gate_up, w_down, b_down):
    c0 = Q_LORA
    c1 = c0 + KV_LORA
    c2 = c1 + QK_ROPE
    sbw = SB_HEADS * SB_DIM
    c3 = c2 + 3 * sbw
    c4 = c3 + D_MODEL
    p = {}
    p["g_norm1"] = g_norm1.reshape(1, D_MODEL)
    p["g_norm2"] = g_norm2.reshape(1, D_MODEL)
    p["w_ql"] = w_in[:, :c0].astype(BF16)
    p["w_kvl"] = w_in[:, c0:c1].astype(BF16)
    p["w_kpe"] = jnp.pad(w_in[:, c1:c2], ((0, 0), (QK_NOPE, HEAD_PAD - QK_DIM))).astype(BF16)
    p["w_sq"] = w_in[:, c2:c2 + sbw].astype(BF16)
    p["w_sk"] = w_in[:, c2 + sbw:c2 + 2 * sbw].astype(BF16)
    p["w_sv"] = w_in[:, c2 + 2 * sbw:c3].astype(BF16)
    p["w_ga"] = w_in[:, c3:c4].astype(BF16)
    p["w_gb"] = w_in[:, c4:].astype(BF16)
    p["g_q_lat"] = g_q_lat.reshape(1, Q_LORA)
    p["g_kv_lat"] = g_kv_lat.reshape(1, KV_LORA)
    p["w_uq"] = _pad_heads(w_uq, QK_DIM, MLA_HEADS).astype(BF16)
    kv = w_ukv.reshape(KV_LORA, MLA_HEADS, QK_NOPE + V_DIM)
    p["w_uk"] = _pad_heads(kv[:, :, :QK_NOPE].reshape(KV_LORA, MLA_HEADS * QK_NOPE),
                           QK_NOPE, MLA_HEADS).astype(BF16)
    p["w_v"] = kv[:, :, QK_NOPE:].reshape(KV_LORA, MLA_HEADS * V_DIM).astype(BF16)
    p["g_qk_q"] = jnp.pad(g_qk_q, (0, HEAD_PAD - QK_DIM)).reshape(1, HEAD_PAD)
    p["g_qk_k"] = jnp.pad(g_qk_k, (0, HEAD_PAD - QK_DIM)).reshape(1, HEAD_PAD)
    p["w_o_mla"] = w_o_mla.astype(BF16)
    p["w_o_sb"] = w_o_sb.astype(BF16)
    p["w_out"] = w_out.astype(BF16)
    wr = jnp.pad(w_router, ((0, 0), (0, LANES - N_EXPERTS)))
    p["w_r_hi"] = wr.astype(BF16)
    p["w_r_lo"] = (wr - p["w_r_hi"].astype(F32)).astype(BF16)
    p["b_r"] = jnp.concatenate([b_router, jnp.full((LANES - N_EXPERTS,), NEG_BIG, F32)]).reshape(1, LANES)
    p["w_gate_up"] = w_gate_up
    p["b_gate_up"] = b_gate_up.reshape(N_EXPERTS, 1, 2 * D_EXPERT)
    p["w_down"] = w_down
    p["b_down"] = b_down.reshape(N_EXPERTS, 1, D_MODEL)
    return p


def _layer(x2, c, rope, B, S, w_ada, b_ada, *layer_weights):
    T = B * S
    p = _prepare(*layer_weights)
    mod = _ada_call(c, w_ada, b_ada)
    sh1, sc1, ga1, sh2, sc2, ga2 = [m.reshape(B, 1, D_MODEL) for m in jnp.split(mod, 6, axis=-1)]

    q, k, v, sq, sk, sv = _proj_call(x2, sh1, sc1, rope, p, B, S)
    o_mla, o_sb = _attn_call(q, k, v, sq, sk, sv, B, S)
    x1, h2p, lpos, wts, tab, totals = _merge_call(x2, (sh1, sc1, ga1, sh2, sc2), o_mla, o_sb, p, B, S)

    nblk = T // ROW_BLOCK
    rows_e = totals[0, :N_EXPERTS].astype(I32)
    tiles_e = (rows_e + EXPERT_TILE - 1) // EXPERT_TILE
    tile_end = jnp.cumsum(tiles_e)
    region = ((tile_end - tiles_e) * EXPERT_TILE).astype(I32)
    n_tiles = (T * TOP_K + nblk * N_EXPERTS * (RUN_ALIGN - 1)) // EXPERT_TILE + N_EXPERTS
    tile_ids = jnp.arange(n_tiles, dtype=I32)
    tile_e = jnp.minimum(jnp.sum((tile_ids[:, None] >= tile_end[None, :]).astype(I32), axis=1),
                         N_EXPERTS - 1).astype(I32)
    tile_v = (tile_ids < tile_end[-1]).astype(I32)
    pad = (-nblk) % TABLE_BLOCKS

    def flat(t):
        return jnp.pad(t, ((0, pad), (0, 0))).reshape(-1)

    units = tab[:, 1, :N_EXPERTS]
    fill_units = LOCAL_ROWS // RUN_ALIGN - jnp.sum(units, axis=1)
    tabs = (flat(tab[:, 0, :N_EXPERTS]), flat(tab[:, 2, :N_EXPERTS] + region[None, :]), flat(units),
            jnp.pad(fill_units, (0, (-nblk) % FILL_CHUNK)))
    table_len = TABLE_BLOCKS * N_EXPERTS
    tail_tabs = tuple(jnp.pad(t, (0, table_len - N_EXPERTS))
                      for t in (region + rows_e, (tiles_e * EXPERT_TILE - rows_e) // RUN_ALIGN))

    xs = _dispatch_call(tabs, tail_tabs, h2p, lpos, n_tiles * EXPERT_TILE)
    ys = _expert_call(tile_e, tile_v, xs, p)
    return _combine_call(tabs, x1, lpos, wts, ga2, ys, B, S)


def kernel(x, c, positions, w_ada, b_ada, g_norm1, w_in, g_q_lat, w_uq, g_kv_lat, w_ukv, g_qk_q,
           g_qk_k, w_o_mla, w_o_sb, w_out, g_norm2, w_router, b_router, w_gate_up, b_gate_up,
           w_down, b_down):
    B, S, D = x.shape
    x2 = x.reshape(B * S, D)
    rope = _rope_table(positions)
    for l in range(w_ada.shape[0]):
        x2 = _layer(x2, c, rope, B, S, w_ada[l], b_ada[l], w_in[l], g_norm1[l], g_q_lat[l],
                    w_uq[l], g_kv_lat[l], w_ukv[l], g_qk_q[l], g_qk_k[l], w_o_mla[l], w_o_sb[l],
                    w_out[l], g_norm2[l], w_router[l], b_router[l], w_gate_up[l], b_gate_up[l],
                    w_down[l], b_down[l])
    return x2.reshape(B, S, D)
```
